```python
import math
import jax, jax.numpy as jnp
from jax import lax
import numpy as np

D_MODEL = 1024
BATCH = 8
SEQ = 2048
DEPTH = 1

CHUNK = 64
N_META = 16
LEAD_PAD = 112
QBLOCK = 128
MIX_WIDTH = D_MODEL
RET_HEADS = 4
RET_DIM = MIX_WIDTH // 2 // RET_HEADS
DSA_HEADS = 8
DSA_OUT_DIM = MIX_WIDTH // 2 // DSA_HEADS
DSA_KV_DIM = 128
DSA_QK_DIM = DSA_KV_DIM
IDX_HEADS = 8
IDX_DIM = 64
DSA_TOPK_MAX = 256
ROPE_THETA = 10000.0
PEER_HEADS = 8
PEER_NKEYS = 128
PEER_N_EXPERTS = PEER_NKEYS * PEER_NKEYS
PEER_KEY_DIM = 256
PEER_TOPK = 16
PEER_BLOCK = 128
EPS = 1e-6
NEG = -1e30

SPLIT_SIZES = (RET_HEADS * RET_DIM,) * 4 + (DSA_HEADS * DSA_QK_DIM, DSA_KV_DIM, DSA_KV_DIM,
                                            IDX_HEADS * IDX_DIM, IDX_DIM, IDX_HEADS)
IN_WIDTH = sum(SPLIT_SIZES)

kernel_name = "hybrid_retention_dsa_peer_meta"


def rmsnorm(x, g):
    xf = x.astype(jnp.float32)
    y = xf * lax.rsqrt(jnp.mean(xf * xf, axis=-1, keepdims=True) + EPS)
    return (y * g.astype(jnp.float32)).astype(x.dtype)


def rope(x, pos):
    d = x.shape[-1]
    inv = ROPE_THETA ** (-jnp.arange(0, d // 2, dtype=jnp.float32) * 2.0 / d)
    ang = pos.astype(jnp.float32)[:, None] * inv[None, :]
    ang = ang.reshape((ang.shape[0],) + (1,) * (x.ndim - 3) + (d // 2,))
    cos = jnp.cos(ang).astype(x.dtype)
    sin = jnp.sin(ang).astype(x.dtype)
    x1, x2 = jnp.split(x, 2, axis=-1)
    return jnp.concatenate([x1 * cos - x2 * sin, x2 * cos + x1 * sin], axis=-1)


def retention(q, k, v, g):
    B, Tp, H, dh = q.shape
    NC = Tp // CHUNK
    log_gamma = jnp.log(1.0 - 2.0 ** (-5.0 - jnp.arange(H, dtype=jnp.float32)))
    i = jnp.arange(CHUNK, dtype=jnp.float32)
    dmat = jnp.exp(log_gamma[:, None, None] * jnp.abs(i[:, None] - i[None, :]))
    zeta = jnp.exp(log_gamma[:, None] * (CHUNK - 1.0 - i)[None, :])
    xi = jnp.exp(log_gamma[:, None] * (i + 1.0)[None, :])
    g_chunk = jnp.exp(log_gamma * CHUNK)
    qc = q.reshape(B, NC, CHUNK, H, dh)
    kc = k.reshape(B, NC, CHUNK, H, dh)
    vc = v.reshape(B, NC, CHUNK, H, dh)
    s = jnp.einsum('bnihd,bnjhd->bnhij', qc, kc) * dmat.astype(q.dtype)
    o_intra = jnp.einsum('bnhij,bnjhe->bnihe', s, vc)
    u = jnp.einsum('bnjhd,bnjhe,hj->nbhde', kc, vc, zeta.astype(q.dtype))
    gc = g_chunk.astype(u.dtype)[None, :, None, None]

    def step(r, u_n):
        return gc * r + u_n, r

    _, r_prev = lax.scan(step, jnp.zeros(u.shape[1:], u.dtype), u)
    o_cross = jnp.einsum('bnihd,hi,nbhde->bnihe', qc, xi.astype(q.dtype), r_prev)
    o = (o_intra + o_cross).reshape(B, Tp, H, dh).astype(jnp.float32)
    mu = jnp.mean(o, axis=-1, keepdims=True)
    var = jnp.mean(jnp.square(o - mu), axis=-1, keepdims=True)
    on = (o - mu) * lax.rsqrt(var + EPS)
    gate = jax.nn.silu(g.astype(jnp.float32))
    return (gate * on).reshape(B, Tp, H * dh).astype(q.dtype)


def gather_rows(t, idx):
    return jax.vmap(lambda tb, ib: tb[ib])(t, idx)


def dsa_attention(q, k, v, qi, ki, wi, topk):
    B, Tp, H, _ = q.shape
    NB = Tp // QBLOCK
    kpos = jnp.arange(Tp)
    kchunk = kpos // CHUNK
    kreal = kpos >= LEAD_PAD

    def to_blocks(t):
        return jnp.moveaxis(t.reshape((B, NB, QBLOCK) + t.shape[2:]), 1, 0)

    def block(args):
        qb, qib, wib, b = args
        qchunk = (b * QBLOCK + jnp.arange(QBLOCK)) // CHUNK
        allowed = kreal[None, :] & (kchunk[None, :] <= qchunk[:, None])
        isc = jax.nn.relu(jnp.einsum('bqhd,bkd->bqhk', qib, ki).astype(jnp.float32))
        isc = jnp.einsum('bqhk,bqh->bqk', isc, wib.astype(jnp.float32))
        isc = jnp.where(allowed[None], isc, NEG)
        _, sel = lax.top_k(isc, topk)
        sel_ok = allowed[jnp.arange(QBLOCK)[None, :, None], sel]
        k_sel = gather_rows(k, sel)
        v_sel = gather_rows(v, sel)
        s = jnp.einsum('bqhd,bqkd->bqhk', qb, k_sel).astype(jnp.float32)
        s = jnp.where(sel_ok[:, :, None, :], s, NEG)
        p = jax.nn.softmax(s, axis=-1).astype(v.dtype)
        return jnp.einsum('bqhk,bqkd->bqhd', p, v_sel)

    o = lax.map(block, (to_blocks(q), to_blocks(qi), to_blocks(wi), jnp.arange(NB)))
    return jnp.moveaxis(o, 0, 1).reshape(B, Tp, H, v.shape[-1])


def token_mixer(hn, w_in, w_uv, w_out):
    B, T, _ = hn.shape
    n_frames = T - N_META
    topk = min(DSA_TOPK_MAX, n_frames // 4)
    hp = jnp.pad(hn, ((0, 0), (LEAD_PAD, 0), (0, 0)))
    Tp = T + LEAD_PAD
    pos = jnp.arange(Tp) - LEAD_PAD
    proj = hp @ w_in
    cuts = [int(c) for c in np.cumsum(SPLIT_SIZES)[:-1]]
    rq, rk, rv, rg, dq, dk, dv, iq, ik, iw = jnp.split(proj, cuts, axis=-1)
    rhead = (B, Tp, RET_HEADS, RET_DIM)
    rq = rope(rq.reshape(rhead), pos)
    rk = rope(rk.reshape(rhead), pos) * (RET_DIM ** -0.5)
    ret = retention(rq, rk, rv.reshape(rhead), rg.reshape(rhead))
    dq = rope(dq.reshape(B, Tp, DSA_HEADS, DSA_QK_DIM), pos) * (DSA_QK_DIM ** -0.5)
    dk = rope(dk, pos)
    iq = rope(iq.reshape(B, Tp, IDX_HEADS, IDX_DIM), pos) * (IDX_DIM ** -0.5)
    ik = rope(ik, pos)
    iw = iw * (IDX_HEADS ** -0.5)
    att = dsa_attention(dq, dk, dv, iq, ik, iw, topk)
    att = jnp.einsum('bthc,hcd->bthd', att, w_uv).reshape(B, Tp, DSA_HEADS * DSA_OUT_DIM)
    mixed = jnp.concatenate([ret, att], axis=-1)[:, LEAD_PAD:]
    return mixed @ w_out


def peer(x, w_q, sub_keys, u_emb, v_emb):
    N, D = x.shape
    q = jnp.einsum('nd,dhk->nhk', x, w_q)
    q1, q2 = jnp.split(q, 2, axis=-1)
    s1 = jnp.einsum('nhk,hmk->nhm', q1, sub_keys[:, 0]).astype(jnp.float32)
    s2 = jnp.einsum('nhk,hmk->nhm', q2, sub_keys[:, 1]).astype(jnp.float32)
    v1, i1 = lax.top_k(s1, PEER_TOPK)
    v2, i2 = lax.top_k(s2, PEER_TOPK)
    cand = (v1[..., :, None] + v2[..., None, :]).reshape(N, PEER_HEADS, PEER_TOPK * PEER_TOPK)
    cidx = (i1[..., :, None] * PEER_NKEYS + i2[..., None, :]).reshape(N, PEER_HEADS, PEER_TOPK * PEER_TOPK)
    sc, sel = lax.top_k(cand, PEER_TOPK)
    eidx = jnp.take_along_axis(cidx, sel, axis=-1).reshape(N, PEER_HEADS * PEER_TOPK)
    gate = jax.nn.softmax(sc, axis=-1).reshape(N, PEER_HEADS * PEER_TOPK).astype(x.dtype)
    n_pad = (-N) % PEER_BLOCK
    nb = (N + n_pad) // PEER_BLOCK
    xb = jnp.pad(x, ((0, n_pad), (0, 0))).reshape(nb, PEER_BLOCK, D)
    eb = jnp.pad(eidx, ((0, n_pad), (0, 0))).reshape(nb, PEER_BLOCK, -1)
    gb = jnp.pad(gate, ((0, n_pad), (0, 0))).reshape(nb, PEER_BLOCK, -1)

    def block(args):
        xk, ek, gk = args
        a = jnp.einsum('nd,ned->ne', xk, u_emb[ek])
        coef = jax.nn.gelu(a, approximate=False) * gk
        return jnp.einsum('ne,ned->nd', coef, v_emb[ek])

    out = lax.map(block, (xb, eb, gb)).reshape(nb * PEER_BLOCK, D)
    return out[:N]


def setup_inputs(seed: int = 0) -> dict:
    key = jax.random.key(seed)
    ks = jax.random.split(key, 12)
    f32 = jnp.float32
    x = jax.random.normal(ks[0], (BATCH, SEQ, D_MODEL), f32)
    meta_tokens = jax.random.normal(ks[1], (N_META, D_MODEL), f32)
    norm_mix = 1.0 + 0.02 * jax.random.normal(ks[2], (DEPTH, D_MODEL), f32)
    w_in = jax.random.normal(ks[3], (DEPTH, D_MODEL, IN_WIDTH), f32) * D_MODEL ** -0.5
    w_uv = jax.random.normal(ks[4], (DEPTH, DSA_HEADS, DSA_KV_DIM, DSA_OUT_DIM), f32) * DSA_KV_DIM ** -0.5
    w_out = jax.random.normal(ks[5], (DEPTH, MIX_WIDTH, D_MODEL), f32) * MIX_WIDTH ** -0.5
    norm_ffn = 1.0 + 0.02 * jax.random.normal(ks[6], (DEPTH, D_MODEL), f32)
    peer_wq = jax.random.normal(ks[7], (DEPTH, D_MODEL, PEER_HEADS, PEER_KEY_DIM), f32) * D_MODEL ** -0.5
    peer_subkeys = jax.random.normal(ks[8], (DEPTH, PEER_HEADS, 2, PEER_NKEYS, PEER_KEY_DIM // 2), f32) * (PEER_KEY_DIM // 2) ** -0.5
    peer_u = jax.random.normal(ks[9], (DEPTH, PEER_N_EXPERTS, D_MODEL), f32) * D_MODEL ** -0.5
    peer_v = jax.random.normal(ks[10], (DEPTH, PEER_N_EXPERTS, D_MODEL), f32) * PEER_HEADS ** -0.5
    norm_final = 1.0 + 0.02 * jax.random.normal(ks[11], (D_MODEL,), f32)
    return {"x": x, "meta_tokens": meta_tokens, "norm_mix": norm_mix, "w_in": w_in,
            "w_uv": w_uv, "w_out": w_out, "norm_ffn": norm_ffn, "peer_wq": peer_wq,
            "peer_subkeys": peer_subkeys, "peer_u": peer_u, "peer_v": peer_v,
            "norm_final": norm_final}


def reference(x, meta_tokens, norm_mix, w_in, w_uv, w_out, norm_ffn, peer_wq, peer_subkeys,
              peer_u, peer_v, norm_final):
    B, S, D = x.shape
    meta = jnp.broadcast_to(meta_tokens.astype(x.dtype)[None], (B, N_META, D))
    h = jnp.concatenate([meta, x], axis=1)
    for l in range(DEPTH):
        h = h + token_mixer(rmsnorm(h, norm_mix[l]), w_in[l], w_uv[l], w_out[l])
        hn = rmsnorm(h, norm_ffn[l])
        h = h + peer(hn.reshape(-1, D), peer_wq[l], peer_subkeys[l], peer_u[l], peer_v[l]).reshape(h.shape)
    h = rmsnorm(h, norm_final)
    return h[:, N_META:]
```

```python
import functools

import jax
import jax.numpy as jnp
from jax import lax
from jax.experimental import pallas as pl
from jax.experimental.pallas import tpu as pltpu

F32 = jnp.float32
BF16 = jnp.bfloat16
I32 = jnp.int32

D_MODEL = 1024
N_META = 16
CHUNK = 64
RET_HEADS = 4
RET_DIM = 128
DSA_HEADS = 8
DSA_DIM = 128
DSA_OUT_DIM = 64
IDX_HEADS = 8
IDX_DIM = 64
DSA_TOPK_MAX = 256
ROPE_THETA = 10000.0
PEER_HEADS = 8
PEER_NKEYS = 128
PEER_TOPK = 16
EPS = 1e-6
NEG = -1e30
INT_MIN = -(2 ** 31)

IN_WIDTH = 3912
IN_WIDTH_PAD = 3968
LANES = 128
RET_BLOCK = 256
DSA_BLOCK = 256
VMEM_LIMIT = 56 * 1024 * 1024

NT = (((1,), (1,)), ((), ()))
TN = (((0,), (0,)), ((), ()))


def _rms(x, g):
    return x * lax.rsqrt(jnp.mean(x * x, axis=-1, keepdims=True) + EPS) * g


def _params(sem):
    return pltpu.CompilerParams(dimension_semantics=sem, vmem_limit_bytes=VMEM_LIMIT)


def _proj_body(x_ref, g_ref, w_ref, tab_ref, rq_ref, rk_ref, rv_ref, rg_ref, dq_ref, dk_ref,
               dv_ref, iq_ref, misc_ref):
    xn = _rms(x_ref[...], g_ref[...]).astype(BF16)
    cos_a, sin_a, cos_b, sin_b1, sin_b2, cos_m, sin_m1, sin_m2 = (tab_ref[i] for i in range(8))

    def mm(lo, n):
        return jnp.dot(xn, w_ref[:, lo:lo + n], preferred_element_type=F32)

    def rope_a(y):
        return y * cos_a + pltpu.roll(y, 64, 1) * sin_a

    def rope_b(y, c, s1, s2):
        return y * c + pltpu.roll(y, 96, 1) * s1 + pltpu.roll(y, 32, 1) * s2

    y = mm(0, 512)
    for h in range(RET_HEADS):
        rq_ref[:, h * 128:(h + 1) * 128] = rope_a(y[:, h * 128:(h + 1) * 128]).astype(BF16)
    y = mm(512, 512)
    for h in range(RET_HEADS):
        rk_ref[:, h * 128:(h + 1) * 128] = (
            rope_a(y[:, h * 128:(h + 1) * 128]) * (RET_DIM ** -0.5)).astype(BF16)
    rv_ref[...] = mm(1024, 512).astype(BF16)
    rg_ref[...] = mm(1536, 512)
    y = mm(2048, 1024)
    for h in range(DSA_HEADS):
        dq_ref[h] = (rope_a(y[:, h * 128:(h + 1) * 128]) * (DSA_DIM ** -0.5)).astype(BF16)
    dk_ref[...] = rope_a(mm(3072, 128)).astype(BF16)
    dv_ref[...] = mm(3200, 128).astype(BF16)
    y = mm(3328, 512)
    for p in range(IDX_HEADS // 2):
        o = (rope_b(y[:, p * 128:(p + 1) * 128], cos_b, sin_b1, sin_b2) * (IDX_DIM ** -0.5)).astype(BF16)
        iq_ref[2 * p] = o[:, :64]
        iq_ref[2 * p + 1] = o[:, 64:]
    misc_ref[...] = rope_b(mm(3840, 128), cos_m, sin_m1, sin_m2)


def _proj(x2d, g, w, tab, tm, tab_blocks):
    n = x2d.shape[0]
    row = lambda i: (i, 0)
    head = lambda i: (0, i, 0)
    out_shape = (
        jax.ShapeDtypeStruct((n, 512), BF16), jax.ShapeDtypeStruct((n, 512), BF16),
        jax.ShapeDtypeStruct((n, 512), BF16), jax.ShapeDtypeStruct((n, 512), F32),
        jax.ShapeDtypeStruct((DSA_HEADS, n, 128), BF16), jax.ShapeDtypeStruct((n, 128), BF16),
        jax.ShapeDtypeStruct((n, 128), BF16), jax.ShapeDtypeStruct((IDX_HEADS, n, 64), BF16),
        jax.ShapeDtypeStruct((n, 128), F32))
    out_specs = (
        pl.BlockSpec((tm, 512), row), pl.BlockSpec((tm, 512), row), pl.BlockSpec((tm, 512), row),
        pl.BlockSpec((tm, 512), row), pl.BlockSpec((DSA_HEADS, tm, 128), head),
        pl.BlockSpec((tm, 128), row), pl.BlockSpec((tm, 128), row),
        pl.BlockSpec((IDX_HEADS, tm, 64), head), pl.BlockSpec((tm, 128), row))
    return pl.pallas_call(
        _proj_body,
        grid=(n // tm,),
        in_specs=[pl.BlockSpec((tm, D_MODEL), row),
                  pl.BlockSpec((1, D_MODEL), lambda i: (0, 0)),
                  pl.BlockSpec((D_MODEL, IN_WIDTH_PAD), lambda i: (0, 0)),
                  pl.BlockSpec((8, tm, 128), lambda i: (0, i % tab_blocks, 0))],
        out_specs=out_specs,
        out_shape=out_shape,
        compiler_params=_params(("arbitrary",)),
        name="proj",
    )(x2d, g, w, tab)


def _rope_tables(pos):
    p = pos.shape[0]
    inv = ROPE_THETA ** (-jnp.arange(0, 64, dtype=F32) * 2.0 / 128)
    ang = pos[:, None] * inv[None, :]
    c, s = jnp.cos(ang), jnp.sin(ang)
    cos_a = jnp.concatenate([c, c], -1)
    sin_a = jnp.concatenate([-s, s], -1)
    inv = ROPE_THETA ** (-jnp.arange(0, 32, dtype=F32) * 2.0 / 64)
    ang = pos[:, None] * inv[None, :]
    c, s = jnp.cos(ang), jnp.sin(ang)
    z = jnp.zeros_like(s)
    z64 = jnp.zeros((p, 64), F32)
    cos_b = jnp.concatenate([c, c, c, c], -1)
    sin_b1 = jnp.concatenate([-s, z, -s, z], -1)
    sin_b2 = jnp.concatenate([z, s, z, s], -1)
    cos_m = jnp.concatenate([c, c, jnp.full((p, 64), IDX_HEADS ** -0.5, F32)], -1)
    sin_m1 = jnp.concatenate([-s, z, z64], -1)
    sin_m2 = jnp.concatenate([z, s, z64], -1)
    return jnp.stack([cos_a, sin_a, cos_b, sin_b1, sin_b2, cos_m, sin_m1, sin_m2])


def _ret_body(rq_ref, rk_ref, rv_ref, rg_ref, mk_ref, mv_ref, dmat_ref, xi_ref, zeta_ref,
              mzeta_ref, gsc_ref, out_ref, r_ref):
    @pl.when(pl.program_id(1) == 0)
    def _():
        for h in range(RET_HEADS):
            hs = slice(h * 128, (h + 1) * 128)
            kz = (mk_ref[:, hs].astype(F32) * mzeta_ref[h]).astype(BF16)
            r_ref[h] = lax.dot_general(kz, mv_ref[:, hs], TN, preferred_element_type=F32)

    for h in range(RET_HEADS):
        hs = slice(h * 128, (h + 1) * 128)
        q = rq_ref[:, hs]
        k = rk_ref[:, hs]
        v = rv_ref[:, hs]
        s = lax.dot_general(q, k, NT, preferred_element_type=F32) * dmat_ref[h]
        o = jnp.dot(s.astype(BF16), v, preferred_element_type=F32)
        r = r_ref[h]
        qx = (q.astype(F32) * xi_ref[h]).astype(BF16)
        o = o + jnp.dot(qx, r.astype(BF16), preferred_element_type=F32)
        kz = (k.astype(F32) * zeta_ref[h]).astype(BF16)
        u = lax.dot_general(kz, v, TN, preferred_element_type=F32)
        r_ref[h] = r * gsc_ref[h] + u
        mu = jnp.mean(o, axis=-1, keepdims=True)
        d = o - mu
        var = jnp.mean(d * d, axis=-1, keepdims=True)
        on = d * lax.rsqrt(var + EPS)
        g = rg_ref[:, hs]
        out_ref[:, hs] = (g * jax.nn.sigmoid(g) * on).astype(BF16)


def _retention(rq, rk, rv, rg, mk, mv, batch, seq):
    n = rq.shape[0]
    nb = seq // RET_BLOCK
    lg = jnp.log(1.0 - 2.0 ** (-5.0 - jnp.arange(RET_HEADS, dtype=F32)))
    i = jnp.arange(RET_BLOCK, dtype=F32)
    ci = jnp.arange(RET_BLOCK) // CHUNK
    vis = (ci[None, :] <= ci[:, None])
    dmat = jnp.where(vis[None], jnp.exp(lg[:, None, None] * jnp.abs(i[:, None] - i[None, :])), 0.0)
    xi = jnp.broadcast_to(jnp.exp(lg[:, None] * (i + 1.0)[None, :])[:, :, None], (RET_HEADS, RET_BLOCK, 128))
    zeta = jnp.broadcast_to(jnp.exp(lg[:, None] * (RET_BLOCK - 1.0 - i)[None, :])[:, :, None],
                            (RET_HEADS, RET_BLOCK, 128))
    im = jnp.arange(N_META, dtype=F32)
    mzeta = jnp.broadcast_to(jnp.exp(lg[:, None] * (N_META - 1.0 - im)[None, :])[:, :, None],
                             (RET_HEADS, N_META, 128))
    gsc = jnp.broadcast_to(jnp.exp(lg * RET_BLOCK)[:, None, None], (RET_HEADS, 1, 128))
    row = lambda b, s: (b * nb + s, 0)
    c2 = lambda b, s: (0, 0)
    c3 = lambda b, s: (0, 0, 0)
    return pl.pallas_call(
        _ret_body,
        grid=(batch, nb),
        in_specs=[pl.BlockSpec((RET_BLOCK, 512), row), pl.BlockSpec((RET_BLOCK, 512), row),
                  pl.BlockSpec((RET_BLOCK, 512), row), pl.BlockSpec((RET_BLOCK, 512), row),
                  pl.BlockSpec((N_META, 512), c2), pl.BlockSpec((N_META, 512), c2),
                  pl.BlockSpec((RET_HEADS, RET_BLOCK, RET_BLOCK), c3),
                  pl.BlockSpec((RET_HEADS, RET_BLOCK, 128), c3),
                  pl.BlockSpec((RET_HEADS, RET_BLOCK, 128), c3),
                  pl.BlockSpec((RET_HEADS, N_META, 128), c3),
                  pl.BlockSpec((RET_HEADS, 1, 128), c3)],
        out_specs=pl.BlockSpec((RET_BLOCK, 512), row),
        out_shape=jax.ShapeDtypeStruct((n, 512), BF16),
        scratch_shapes=[pltpu.VMEM((RET_HEADS, 128, 128), F32)],
        compiler_params=_params(("arbitrary", "arbitrary")),
        name="retention",
    )(rq, rk, rv, rg, mk, mv, dmat, xi, zeta, mzeta, gsc)


def _aligned(off):
    return off if isinstance(off, int) else pl.multiple_of(off, LANES)


def _sort_key(x):
    b = lax.bitcast_convert_type(x, I32)
    return jnp.where(b < 0, b ^ jnp.int32(0x7FFFFFFF), b)


def _dsa_body(qa_ref, qi_ref, misc_ref, k_ref, v_ref, ki_ref, wuv_ref, out_ref,
              skey_ref, wb_ref, m_ref, l_ref, acc_ref, *, topk):
    qb = pl.program_id(1)
    tq = DSA_BLOCK
    nh = DSA_HEADS

    misc = misc_ref[...]
    for h in range(IDX_HEADS):
        wb_ref[h] = jnp.broadcast_to(misc[:, 64 + h:65 + h], (tq, LANES))
    qi = qi_ref[...].reshape(IDX_HEADS * tq, IDX_DIM)

    def isc_tile(koff, width, allowed):
        kt = ki_ref[pl.ds(koff, width), :]
        z = lax.dot_general(qi, kt, NT, preferred_element_type=F32)
        z = jnp.maximum(z, 0.0).reshape(IDX_HEADS, tq, width)
        for c in range(width // LANES):
            cs = slice(c * LANES, (c + 1) * LANES)
            isc = z[0][:, cs] * wb_ref[0]
            for h in range(1, IDX_HEADS):
                isc = isc + z[h][:, cs] * wb_ref[h]
            key = _sort_key(isc)
            if allowed is not None:
                key = jnp.where(allowed[:, cs], key, jnp.int32(INT_MIN))
            skey_ref[:, pl.ds(_aligned(koff + c * LANES), LANES)] = key

    lane = lax.broadcasted_iota(I32, (tq, LANES), 1)
    isc_tile(0, LANES, lane >= LANES - N_META)

    def body_a(t, c):
        isc_tile(pl.multiple_of(LANES + t * tq, LANES), tq, None)
        return c

    lax.fori_loop(0, qb, body_a, 0)
    rowc = lax.broadcasted_iota(I32, (tq, tq), 0) // CHUNK
    colc = lax.broadcasted_iota(I32, (tq, tq), 1) // CHUNK
    diag_ok = colc <= rowc
    diag_off = pl.multiple_of(LANES + qb * tq, LANES)
    isc_tile(diag_off, tq, diag_ok)

    nslab = 1 + 2 * (qb + 1)
    kf = float(topk)

    def count_ge(cand):
        def body(j, a):
            sk = skey_ref[:, pl.ds(pl.multiple_of(j * LANES, LANES), LANES)]
            return a + jnp.where(sk >= cand, 1.0, 0.0)
        a = lax.fori_loop(0, nslab, body, jnp.zeros((tq, LANES), F32))
        return jnp.sum(a, axis=1, keepdims=True)

    zero = jnp.zeros((tq, LANES), I32)
    base = jnp.where(count_ge(zero) >= kf, zero, jnp.int32(INT_MIN))

    def bit_body(it, base):
        cand = base | lax.shift_left(jnp.int32(1), 30 - it)
        return jnp.where(count_ge(cand) >= kf, cand, base)

    thr = lax.fori_loop(0, 31, bit_body, base)
    thr = jnp.maximum(thr, jnp.int32(INT_MIN + 1))

    m_ref[...] = jnp.full(m_ref.shape, NEG, F32)
    l_ref[...] = jnp.zeros(l_ref.shape, F32)
    acc_ref[...] = jnp.zeros(acc_ref.shape, F32)
    qa = qa_ref[...].reshape(nh * tq, DSA_DIM)

    def att_tile(koff, width):
        kt = k_ref[pl.ds(koff, width), :]
        vt = v_ref[pl.ds(koff, width), :]
        s = lax.dot_general(qa, kt, NT, preferred_element_type=F32).reshape(nh, tq, width)
        reps = width // LANES
        thr_w = thr if reps == 1 else jnp.concatenate([thr] * reps, axis=1)
        sel = skey_ref[:, pl.ds(koff, width)] >= thr_w
        s = jnp.where(sel[None], s, NEG).reshape(nh * tq, width)
        m_prev = m_ref[...]
        m_new = jnp.maximum(m_prev, jnp.max(s, axis=1, keepdims=True))
        alpha = jnp.exp(m_prev - m_new)
        m_w = m_new if reps == 1 else jnp.concatenate([m_new] * reps, axis=1)
        p = jnp.exp(s - m_w)
        l_ref[...] = alpha * l_ref[...] + jnp.sum(p, axis=1, keepdims=True)
        acc_ref[...] = alpha * acc_ref[...] + jnp.dot(p.astype(BF16), vt, preferred_element_type=F32)
        m_ref[...] = m_new

    att_tile(0, LANES)

    def body_c(t, c):
        att_tile(pl.multiple_of(LANES + t * tq, LANES), tq)
        return c

    lax.fori_loop(0, qb + 1, body_c, 0)

    o = (acc_ref[...] / l_ref[...]).astype(BF16).reshape(nh, tq, DSA_DIM)
    for h in range(nh):
        out_ref[:, h * DSA_OUT_DIM:(h + 1) * DSA_OUT_DIM] = jnp.dot(
            o[h], wuv_ref[h], preferred_element_type=F32).astype(BF16)


def _dsa(dq, iq, misc, kpad, vpad, kipad, wuv, batch, seq, topk):
    n = misc.shape[0]
    nq = seq // DSA_BLOCK
    tp = kpad.shape[1]
    head = lambda b, q: (0, b * nq + q, 0)
    row = lambda b, q: (b * nq + q, 0)
    per_b = lambda b, q: (b, 0, 0)
    return pl.pallas_call(
        functools.partial(_dsa_body, topk=topk),
        grid=(batch, nq),
        in_specs=[pl.BlockSpec((DSA_HEADS, DSA_BLOCK, DSA_DIM), head),
                  pl.BlockSpec((IDX_HEADS, DSA_BLOCK, IDX_DIM), head),
                  pl.BlockSpec((DSA_BLOCK, 128), row),
                  pl.BlockSpec((None, tp, DSA_DIM), per_b),
                  pl.BlockSpec((None, tp, DSA_DIM), per_b),
                  pl.BlockSpec((None, tp, IDX_DIM), per_b),
                  pl.BlockSpec((DSA_HEADS, DSA_DIM, DSA_OUT_DIM), lambda b, q: (0, 0, 0))],
        out_specs=pl.BlockSpec((DSA_BLOCK, 512), row),
        out_shape=jax.ShapeDtypeStruct((n, 512), BF16),
        scratch_shapes=[pltpu.VMEM((DSA_BLOCK, tp), I32),
                        pltpu.VMEM((IDX_HEADS, DSA_BLOCK, LANES), F32),
                        pltpu.VMEM((DSA_HEADS * DSA_BLOCK, LANES), F32),
                        pltpu.VMEM((DSA_HEADS * DSA_BLOCK, LANES), F32),
                        pltpu.VMEM((DSA_HEADS * DSA_BLOCK, DSA_DIM), F32)],
        compiler_params=_params(("arbitrary", "arbitrary")),
        name="dsa",
    )(dq, iq, misc, kpad, vpad, kipad, wuv)


def _post_body(ret_ref, att_ref, x_ref, w_ref, g_ref, h1_ref, hnt_ref):
    y = jnp.dot(ret_ref[...], w_ref[0:512, :], preferred_element_type=F32)
    y = y + jnp.dot(att_ref[...], w_ref[512:1024, :], preferred_element_type=F32)
    h1 = x_ref[...] + y
    h1_ref[...] = h1
    hnt_ref[...] = _rms(h1, g_ref[...]).T.astype(BF16)


def _post(ret, att, x2d, w_out, g, tm):
    n = x2d.shape[0]
    row = lambda i: (i, 0)
    return pl.pallas_call(
        _post_body,
        grid=(n // tm,),
        in_specs=[pl.BlockSpec((tm, 512), row), pl.BlockSpec((tm, 512), row),
                  pl.BlockSpec((tm, D_MODEL), row),
                  pl.BlockSpec((D_MODEL, D_MODEL), lambda i: (0, 0)),
                  pl.BlockSpec((1, D_MODEL), lambda i: (0, 0))],
        out_specs=(pl.BlockSpec((tm, D_MODEL), row), pl.BlockSpec((D_MODEL, tm), lambda i: (0, i))),
        out_shape=(jax.ShapeDtypeStruct((n, D_MODEL), F32), jax.ShapeDtypeStruct((D_MODEL, n), BF16)),
        compiler_params=_params(("arbitrary",)),
        name="post",
    )(ret, att, x2d, w_out, g)


def _top_desc(s, k, dst_ref):
    cur = s
    for r in range(k):
        m = jnp.max(cur, axis=0, keepdims=True)
        dst_ref[r:r + 1, :] = m
        if r + 1 < k:
            cur = jnp.where(cur == m, -jnp.inf, cur)


def _psel_body(hnt_ref, wq_ref, sk_ref, theta_ref, e1_ref, s2_ref, e2_ref, v1_ref, v2_ref, vc_ref):
    hnt = hnt_ref[...]
    for h in range(PEER_HEADS):
        q = jnp.dot(wq_ref[h * 256:(h + 1) * 256, :], hnt, preferred_element_type=F32)
        s1 = jnp.dot(sk_ref[h, 0], q[:128].astype(BF16), preferred_element_type=F32)
        s2 = jnp.dot(sk_ref[h, 1], q[128:].astype(BF16), preferred_element_type=F32)
        _top_desc(s1, PEER_TOPK, v1_ref)
        _top_desc(s2, PEER_TOPK, v2_ref)
        v1 = v1_ref[...]
        v2 = v2_ref[...]
        cand = jnp.concatenate([v2 + v1[0:1]] + [v2[0:8] + v1[a:a + 1] for a in range(1, 8)]
                               + [v1[8:16] + v2[0:1]], axis=0)
        _top_desc(cand, PEER_TOPK, vc_ref)
        tau = vc_ref[PEER_TOPK - 1:PEER_TOPK, :]
        top = vc_ref[0:1, :]
        zsum = jnp.sum(jnp.where(cand >= tau, jnp.exp(cand - top), 0.0), axis=0, keepdims=True)
        theta = jnp.full(s1.shape, jnp.inf, F32)
        for b in range(PEER_TOPK):
            v2b = v2[b:b + 1]
            theta = jnp.minimum(theta, jnp.where(s1 + v2b >= tau, v2b, jnp.inf))
        theta_ref[h] = theta
        e1_ref[h] = jnp.exp(s1 - v1[0:1]) / zsum
        s2_ref[h] = s2
        e2_ref[h] = jnp.exp(s2 - v2[0:1])


def _psel(hnt, wq_t, sk, tn):
    n = hnt.shape[1]
    spec = pl.BlockSpec((PEER_HEADS, PEER_NKEYS, tn), lambda i: (0, 0, i))
    shp = jax.ShapeDtypeStruct((PEER_HEADS, PEER_NKEYS, n), F32)
    return pl.pallas_call(
        _psel_body,
        grid=(n // tn,),
        in_specs=[pl.BlockSpec((D_MODEL, tn), lambda i: (0, i)),
                  pl.BlockSpec((PEER_HEADS * 256, D_MODEL), lambda i: (0, 0)),
                  pl.BlockSpec((PEER_HEADS, 2, PEER_NKEYS, 128), lambda i: (0, 0, 0, 0))],
        out_specs=(spec, spec, spec, spec),
        out_shape=(shp, shp, shp, shp),
        scratch_shapes=[pltpu.VMEM((PEER_TOPK, tn), F32)] * 3,
        compiler_params=_params(("arbitrary",)),
        name="psel",
    )(hnt, wq_t, sk)


def _pdense_body(hnt_ref, u_ref, vt_ref, theta_ref, e1_ref, s2_ref, e2_ref, h1_ref, g_ref,
                 out_ref, acc_ref, *, eb):
    j = pl.program_id(1)

    @pl.when(j == 0)
    def _():
        acc_ref[...] = jnp.zeros(acc_ref.shape, F32)

    hnt = hnt_ref[...]
    coefs = []
    for c in range(eb // PEER_NKEYS):
        i1 = j * (eb // PEER_NKEYS) + c
        a = jnp.dot(u_ref[c * 128:(c + 1) * 128, :], hnt, preferred_element_type=F32)
        gate = None
        for h in range(PEER_HEADS):
            th = theta_ref[h, pl.ds(i1, 1), :]
            e1 = e1_ref[h, pl.ds(i1, 1), :]
            term = jnp.where(s2_ref[h] >= th, e2_ref[h] * e1, 0.0)
            gate = term if gate is None else gate + term
        gelu = 0.5 * a * (1.0 + lax.erf(a * (0.5 ** 0.5)))
        coefs.append((gelu * gate).astype(BF16))
    coef = coefs[0] if len(coefs) == 1 else jnp.concatenate(coefs, axis=0)
    acc_ref[...] += jnp.dot(vt_ref[...], coef, preferred_element_type=F32)

    @pl.when(j == pl.num_programs(1) - 1)
    def _():
        out_ref[...] = _rms(h1_ref[...] + acc_ref[...].T, g_ref[...])


def _pdense(hnt, u, vt, theta, e1, s2, e2, h1, g, tn, eb):
    n = hnt.shape[1]
    ne = u.shape[0]
    sel = pl.BlockSpec((PEER_HEADS, PEER_NKEYS, tn), lambda i, j: (0, 0, i))
    return pl.pallas_call(
        functools.partial(_pdense_body, eb=eb),
        grid=(n // tn, ne // eb),
        in_specs=[pl.BlockSpec((D_MODEL, tn), lambda i, j: (0, i)),
                  pl.BlockSpec((eb, D_MODEL), lambda i, j: (j, 0)),
                  pl.BlockSpec((D_MODEL, eb), lambda i, j: (0, j)),
                  sel, sel, sel, sel,
                  pl.BlockSpec((tn, D_MODEL), lambda i, j: (i, 0)),
                  pl.BlockSpec((1, D_MODEL), lambda i, j: (0, 0))],
        out_specs=pl.BlockSpec((tn, D_MODEL), lambda i, j: (i, 0)),
        out_shape=jax.ShapeDtypeStruct((n, D_MODEL), F32),
        scratch_shapes=[pltpu.VMEM((D_MODEL, tn), F32)],
        compiler_params=_params(("arbitrary", "arbitrary")),
        name="pdense",
    )(hnt, u, vt, theta, e1, s2, e2, h1, g)


def kernel(x, meta_tokens, norm_mix, w_in, w_uv, w_out, norm_ffn, peer_wq, peer_subkeys,
           peer_u, peer_v, norm_final):
    batch, seq, d = x.shape
    assert d == D_MODEL and norm_mix.shape[0] == 1 and seq % DSA_BLOCK == 0
    n = batch * seq
    topk = min(DSA_TOPK_MAX, seq // 4)
    tm = 512 if seq % 512 == 0 else 256

    w = jnp.pad(w_in[0], ((0, 0), (0, IN_WIDTH_PAD - IN_WIDTH))).astype(BF16)
    g_mix = norm_mix[0][None, :]
    x2d = x.reshape(n, d)
    pos_f = jnp.arange(seq, dtype=F32) + float(N_META)
    pos_m = jnp.arange(N_META, dtype=F32)

    rq, rk, rv, rg, dq, dk, dv, iq, misc = _proj(x2d, g_mix, w, _rope_tables(pos_f), tm, seq // tm)
    _, mk, mv, _, _, mdk, mdv, _, mmisc = _proj(meta_tokens, g_mix, w, _rope_tables(pos_m), N_META, 1)

    ret = _retention(rq, rk, rv, rg, mk, mv, batch, seq)

    def keys(frames, meta):
        width = frames.shape[-1]
        lead = jnp.concatenate([jnp.zeros((LANES - N_META, width), frames.dtype), meta], axis=0)
        lead = jnp.broadcast_to(lead[None], (batch, LANES, width))
        return jnp.concatenate([lead, frames.reshape(batch, seq, width)], axis=1)

    kpad = keys(dk, mdk)
    vpad = keys(dv, mdv)
    kipad = keys(misc[:, :IDX_DIM].astype(BF16), mmisc[:, :IDX_DIM].astype(BF16))
    att = _dsa(dq, iq, misc, kpad, vpad, kipad, w_uv[0].astype(BF16), batch, seq, topk)

    h1, hnt = _post(ret, att, x2d, w_out[0].astype(BF16), norm_ffn[0][None, :], tm)

    wq_t = peer_wq[0].reshape(d, PEER_HEADS * 256).T.astype(BF16)
    theta, e1, s2, e2 = _psel(hnt, wq_t, peer_subkeys[0].astype(BF16), tm)
    out = _pdense(hnt, peer_u[0].astype(BF16), peer_v[0].T.astype(BF16), theta, e1, s2, e2, h1,
                  norm_final[None, :], tm, 256)
    return out.reshape(batch, seq, d)
```

```python
import functools

import jax
import jax.numpy as jnp
from jax import lax
from jax.experimental import pallas as pl
from jax.experimental.pallas import tpu as pltpu

F32 = jnp.float32
BF16 = jnp.bfloat16
I32 = jnp.int32

D_MODEL = 1024
N_META = 16
CHUNK = 64
RET_HEADS = 4
RET_DIM = 128
DSA_HEADS = 8
DSA_DIM = 128
DSA_OUT_DIM = 64
IDX_HEADS = 8
IDX_DIM = 64
DSA_TOPK_MAX = 256
ROPE_THETA = 10000.0
PEER_HEADS = 8
PEER_NKEYS = 128
PEER_TOPK = 16
EPS = 1e-6
NEG = -1e30
INT_MIN = -(2 ** 31)

IN_WIDTH = 3912
IN_WIDTH_PAD = 3968
LANES = 128
RET_BLOCK = 256
PDENSE_EXPERTS = 1024
PDENSE_ROWS = 512
PDENSE_GROUP = 2
DSA_BLOCK = 256
VMEM_LIMIT = 56 * 1024 * 1024

NT = (((1,), (1,)), ((), ()))
TN = (((0,), (0,)), ((), ()))


def _rms(x, g):
    return x * lax.rsqrt(jnp.mean(x * x, axis=-1, keepdims=True) + EPS) * g


def _params(sem):
    return pltpu.CompilerParams(dimension_semantics=sem, vmem_limit_bytes=VMEM_LIMIT)


def _proj_body(x_ref, g_ref, w_ref, tab_ref, rq_ref, rk_ref, rv_ref, rg_ref, dq_ref, dk_ref,
               dv_ref, iq_ref, misc_ref):
    xn = _rms(x_ref[...], g_ref[...]).astype(BF16)
    cos_a, sin_a, cos_b, sin_b1, sin_b2, cos_m, sin_m1, sin_m2 = (tab_ref[i] for i in range(8))

    def mm(lo, n):
        return jnp.dot(xn, w_ref[:, lo:lo + n], preferred_element_type=F32)

    def rope_a(y):
        return y * cos_a + pltpu.roll(y, 64, 1) * sin_a

    def rope_b(y, c, s1, s2):
        return y * c + pltpu.roll(y, 96, 1) * s1 + pltpu.roll(y, 32, 1) * s2

    y = mm(0, 512)
    for h in range(RET_HEADS):
        rq_ref[:, h * 128:(h + 1) * 128] = rope_a(y[:, h * 128:(h + 1) * 128]).astype(BF16)
    y = mm(512, 512)
    for h in range(RET_HEADS):
        rk_ref[:, h * 128:(h + 1) * 128] = (
            rope_a(y[:, h * 128:(h + 1) * 128]) * (RET_DIM ** -0.5)).astype(BF16)
    rv_ref[...] = mm(1024, 512).astype(BF16)
    rg_ref[...] = mm(1536, 512)
    y = mm(2048, 1024)
    for h in range(DSA_HEADS):
        dq_ref[h] = (rope_a(y[:, h * 128:(h + 1) * 128]) * (DSA_DIM ** -0.5)).astype(BF16)
    dk_ref[...] = rope_a(mm(3072, 128)).astype(BF16)
    dv_ref[...] = mm(3200, 128).astype(BF16)
    y = mm(3328, 512)
    for p in range(IDX_HEADS // 2):
        o = (rope_b(y[:, p * 128:(p + 1) * 128], cos_b, sin_b1, sin_b2) * (IDX_DIM ** -0.5)).astype(BF16)
        iq_ref[2 * p] = o[:, :64]
        iq_ref[2 * p + 1] = o[:, 64:]
    misc_ref[...] = rope_b(mm(3840, 128), cos_m, sin_m1, sin_m2)


def _proj(x2d, g, w, tab, tm, tab_blocks):
    n = x2d.shape[0]
    row = lambda i: (i, 0)
    head = lambda i: (0, i, 0)
    out_shape = (
        jax.ShapeDtypeStruct((n, 512), BF16), jax.ShapeDtypeStruct((n, 512), BF16),
        jax.ShapeDtypeStruct((n, 512), BF16), jax.ShapeDtypeStruct((n, 512), F32),
        jax.ShapeDtypeStruct((DSA_HEADS, n, 128), BF16), jax.ShapeDtypeStruct((n, 128), BF16),
        jax.ShapeDtypeStruct((n, 128), BF16), jax.ShapeDtypeStruct((IDX_HEADS, n, 64), BF16),
        jax.ShapeDtypeStruct((n, 128), F32))
    out_specs = (
        pl.BlockSpec((tm, 512), row), pl.BlockSpec((tm, 512), row), pl.BlockSpec((tm, 512), row),
        pl.BlockSpec((tm, 512), row), pl.BlockSpec((DSA_HEADS, tm, 128), head),
        pl.BlockSpec((tm, 128), row), pl.BlockSpec((tm, 128), row),
        pl.BlockSpec((IDX_HEADS, tm, 64), head), pl.BlockSpec((tm, 128), row))
    return pl.pallas_call(
        _proj_body,
        grid=(n // tm,),
        in_specs=[pl.BlockSpec((tm, D_MODEL), row),
                  pl.BlockSpec((1, D_MODEL), lambda i: (0, 0)),
                  pl.BlockSpec((D_MODEL, IN_WIDTH_PAD), lambda i: (0, 0)),
                  pl.BlockSpec((8, tm, 128), lambda i: (0, i % tab_blocks, 0))],
        out_specs=out_specs,
        out_shape=out_shape,
        compiler_params=_params(("arbitrary",)),
        name="proj",
    )(x2d, g, w, tab)


def _rope_tables(pos):
    p = pos.shape[0]
    inv = ROPE_THETA ** (-jnp.arange(0, 64, dtype=F32) * 2.0 / 128)
    ang = pos[:, None] * inv[None, :]
    c, s = jnp.cos(ang), jnp.sin(ang)
    cos_a = jnp.concatenate([c, c], -1)
    sin_a = jnp.concatenate([-s, s], -1)
    inv = ROPE_THETA ** (-jnp.arange(0, 32, dtype=F32) * 2.0 / 64)
    ang = pos[:, None] * inv[None, :]
    c, s = jnp.cos(ang), jnp.sin(ang)
    z = jnp.zeros_like(s)
    z64 = jnp.zeros((p, 64), F32)
    cos_b = jnp.concatenate([c, c, c, c], -1)
    sin_b1 = jnp.concatenate([-s, z, -s, z], -1)
    sin_b2 = jnp.concatenate([z, s, z, s], -1)
    cos_m = jnp.concatenate([c, c, jnp.full((p, 64), IDX_HEADS ** -0.5, F32)], -1)
    sin_m1 = jnp.concatenate([-s, z, z64], -1)
    sin_m2 = jnp.concatenate([z, s, z64], -1)
    return jnp.stack([cos_a, sin_a, cos_b, sin_b1, sin_b2, cos_m, sin_m1, sin_m2])


def _ret_body(rq_ref, rk_ref, rv_ref, rg_ref, mk_ref, mv_ref, dmat_ref, xi_ref, zeta_ref,
              mzeta_ref, gsc_ref, out_ref, r_ref):
    @pl.when(pl.program_id(1) == 0)
    def _():
        for h in range(RET_HEADS):
            hs = slice(h * 128, (h + 1) * 128)
            kz = (mk_ref[:, hs].astype(F32) * mzeta_ref[h]).astype(BF16)
            r_ref[h] = lax.dot_general(kz, mv_ref[:, hs], TN, preferred_element_type=F32)

    for h in range(RET_HEADS):
        hs = slice(h * 128, (h + 1) * 128)
        q = rq_ref[:, hs]
        k = rk_ref[:, hs]
        v = rv_ref[:, hs]
        s = lax.dot_general(q, k, NT, preferred_element_type=F32) * dmat_ref[h]
        o = jnp.dot(s.astype(BF16), v, preferred_element_type=F32)
        r = r_ref[h]
        qx = (q.astype(F32) * xi_ref[h]).astype(BF16)
        o = o + jnp.dot(qx, r.astype(BF16), preferred_element_type=F32)
        kz = (k.astype(F32) * zeta_ref[h]).astype(BF16)
        u = lax.dot_general(kz, v, TN, preferred_element_type=F32)
        r_ref[h] = r * gsc_ref[h] + u
        mu = jnp.mean(o, axis=-1, keepdims=True)
        d = o - mu
        var = jnp.mean(d * d, axis=-1, keepdims=True)
        on = d * lax.rsqrt(var + EPS)
        g = rg_ref[:, hs]
        out_ref[:, hs] = (g * jax.nn.sigmoid(g) * on).astype(BF16)


def _retention(rq, rk, rv, rg, mk, mv, batch, seq):
    n = rq.shape[0]
    nb = seq // RET_BLOCK
    lg = jnp.log(1.0 - 2.0 ** (-5.0 - jnp.arange(RET_HEADS, dtype=F32)))
    i = jnp.arange(RET_BLOCK, dtype=F32)
    ci = jnp.arange(RET_BLOCK) // CHUNK
    vis = (ci[None, :] <= ci[:, None])
    dmat = jnp.where(vis[None], jnp.exp(lg[:, None, None] * jnp.abs(i[:, None] - i[None, :])), 0.0)
    xi = jnp.broadcast_to(jnp.exp(lg[:, None] * (i + 1.0)[None, :])[:, :, None], (RET_HEADS, RET_BLOCK, 128))
    zeta = jnp.broadcast_to(jnp.exp(lg[:, None] * (RET_BLOCK - 1.0 - i)[None, :])[:, :, None],
                            (RET_HEADS, RET_BLOCK, 128))
    im = jnp.arange(N_META, dtype=F32)
    mzeta = jnp.broadcast_to(jnp.exp(lg[:, None] * (N_META - 1.0 - im)[None, :])[:, :, None],
                             (RET_HEADS, N_META, 128))
    gsc = jnp.broadcast_to(jnp.exp(lg * RET_BLOCK)[:, None, None], (RET_HEADS, 1, 128))
    row = lambda b, s: (b * nb + s, 0)
    c2 = lambda b, s: (0, 0)
    c3 = lambda b, s: (0, 0, 0)
    return pl.pallas_call(
        _ret_body,
        grid=(batch, nb),
        in_specs=[pl.BlockSpec((RET_BLOCK, 512), row), pl.BlockSpec((RET_BLOCK, 512), row),
                  pl.BlockSpec((RET_BLOCK, 512), row), pl.BlockSpec((RET_BLOCK, 512), row),
                  pl.BlockSpec((N_META, 512), c2), pl.BlockSpec((N_META, 512), c2),
                  pl.BlockSpec((RET_HEADS, RET_BLOCK, RET_BLOCK), c3),
                  pl.BlockSpec((RET_HEADS, RET_BLOCK, 128), c3),
                  pl.BlockSpec((RET_HEADS, RET_BLOCK, 128), c3),
                  pl.BlockSpec((RET_HEADS, N_META, 128), c3),
                  pl.BlockSpec((RET_HEADS, 1, 128), c3)],
        out_specs=pl.BlockSpec((RET_BLOCK, 512), row),
        out_shape=jax.ShapeDtypeStruct((n, 512), BF16),
        scratch_shapes=[pltpu.VMEM((RET_HEADS, 128, 128), F32)],
        compiler_params=_params(("arbitrary", "arbitrary")),
        name="retention",
    )(rq, rk, rv, rg, mk, mv, dmat, xi, zeta, mzeta, gsc)


def _aligned(off):
    return off if isinstance(off, int) else pl.multiple_of(off, LANES)


def _sort_key(x):
    b = lax.bitcast_convert_type(x, I32)
    return jnp.where(b < 0, b ^ jnp.int32(0x7FFFFFFF), b)


def _dsa_body(qa_ref, qi_ref, misc_ref, k_ref, v_ref, ki_ref, wuv_ref, out_ref,
              skey_ref, wb_ref, m_ref, l_ref, acc_ref, *, topk):
    qb = pl.program_id(1)
    tq = DSA_BLOCK
    nh = DSA_HEADS

    misc = misc_ref[...]
    for h in range(IDX_HEADS):
        wb_ref[h] = jnp.broadcast_to(misc[:, 64 + h:65 + h], (tq, LANES))
    qi = qi_ref[...].reshape(IDX_HEADS * tq, IDX_DIM)

    def isc_tile(koff, width, allowed):
        kt = ki_ref[pl.ds(koff, width), :]
        z = lax.dot_general(qi, kt, NT, preferred_element_type=F32)
        z = jnp.maximum(z, 0.0).reshape(IDX_HEADS, tq, width)
        for c in range(width // LANES):
            cs = slice(c * LANES, (c + 1) * LANES)
            isc = z[0][:, cs] * wb_ref[0]
            for h in range(1, IDX_HEADS):
                isc = isc + z[h][:, cs] * wb_ref[h]
            key = _sort_key(isc)
            if allowed is not None:
                key = jnp.where(allowed[:, cs], key, jnp.int32(INT_MIN))
            skey_ref[:, pl.ds(_aligned(koff + c * LANES), LANES)] = key

    lane = lax.broadcasted_iota(I32, (tq, LANES), 1)
    isc_tile(0, LANES, lane >= LANES - N_META)

    def body_a(t, c):
        isc_tile(pl.multiple_of(LANES + t * tq, LANES), tq, None)
        return c

    lax.fori_loop(0, qb, body_a, 0)
    rowc = lax.broadcasted_iota(I32, (tq, tq), 0) // CHUNK
    colc = lax.broadcasted_iota(I32, (tq, tq), 1) // CHUNK
    diag_ok = colc <= rowc
    diag_off = pl.multiple_of(LANES + qb * tq, LANES)
    isc_tile(diag_off, tq, diag_ok)

    nslab = 1 + 2 * (qb + 1)
    kf = float(topk)

    def count_ge(cand):
        def body(j, a):
            sk = skey_ref[:, pl.ds(pl.multiple_of(j * LANES, LANES), LANES)]
            return a + jnp.where(sk >= cand, 1.0, 0.0)
        a = lax.fori_loop(0, nslab, body, jnp.zeros((tq, LANES), F32))
        return jnp.sum(a, axis=1, keepdims=True)

    zero = jnp.zeros((tq, LANES), I32)
    base = jnp.where(count_ge(zero) >= kf, zero, jnp.int32(INT_MIN))

    def bit_body(it, base):
        cand = base | lax.shift_left(jnp.int32(1), 30 - it)
        return jnp.where(count_ge(cand) >= kf, cand, base)

    thr = lax.fori_loop(0, 31, bit_body, base)
    thr = jnp.maximum(thr, jnp.int32(INT_MIN + 1))

    m_ref[...] = jnp.full(m_ref.shape, NEG, F32)
    l_ref[...] = jnp.zeros(l_ref.shape, F32)
    acc_ref[...] = jnp.zeros(acc_ref.shape, F32)
    qa = qa_ref[...].reshape(nh * tq, DSA_DIM)

    def att_tile(koff, width):
        kt = k_ref[pl.ds(koff, width), :]
        vt = v_ref[pl.ds(koff, width), :]
        s = lax.dot_general(qa, kt, NT, preferred_element_type=F32).reshape(nh, tq, width)
        reps = width // LANES
        thr_w = thr if reps == 1 else jnp.concatenate([thr] * reps, axis=1)
        sel = skey_ref[:, pl.ds(koff, width)] >= thr_w
        s = jnp.where(sel[None], s, NEG).reshape(nh * tq, width)
        m_prev = m_ref[...]
        m_new = jnp.maximum(m_prev, jnp.max(s, axis=1, keepdims=True))
        alpha = jnp.exp(m_prev - m_new)
        m_w = m_new if reps == 1 else jnp.concatenate([m_new] * reps, axis=1)
        p = jnp.exp(s - m_w)
        l_ref[...] = alpha * l_ref[...] + jnp.sum(p, axis=1, keepdims=True)
        acc_ref[...] = alpha * acc_ref[...] + jnp.dot(p.astype(BF16), vt, preferred_element_type=F32)
        m_ref[...] = m_new

    att_tile(0, LANES)

    def body_c(t, c):
        att_tile(pl.multiple_of(LANES + t * tq, LANES), tq)
        return c

    lax.fori_loop(0, qb + 1, body_c, 0)

    o = (acc_ref[...] / l_ref[...]).astype(BF16).reshape(nh, tq, DSA_DIM)
    for h in range(nh):
        out_ref[:, h * DSA_OUT_DIM:(h + 1) * DSA_OUT_DIM] = jnp.dot(
            o[h], wuv_ref[h], preferred_element_type=F32).astype(BF16)


def _dsa(dq, iq, misc, kpad, vpad, kipad, wuv, batch, seq, topk):
    n = misc.shape[0]
    nq = seq // DSA_BLOCK
    tp = kpad.shape[1]
    head = lambda b, q: (0, b * nq + q, 0)
    row = lambda b, q: (b * nq + q, 0)
    per_b = lambda b, q: (b, 0, 0)
    return pl.pallas_call(
        functools.partial(_dsa_body, topk=topk),
        grid=(batch, nq),
        in_specs=[pl.BlockSpec((DSA_HEADS, DSA_BLOCK, DSA_DIM), head),
                  pl.BlockSpec((IDX_HEADS, DSA_BLOCK, IDX_DIM), head),
                  pl.BlockSpec((DSA_BLOCK, 128), row),
                  pl.BlockSpec((None, tp, DSA_DIM), per_b),
                  pl.BlockSpec((None, tp, DSA_DIM), per_b),
                  pl.BlockSpec((None, tp, IDX_DIM), per_b),
                  pl.BlockSpec((DSA_HEADS, DSA_DIM, DSA_OUT_DIM), lambda b, q: (0, 0, 0))],
        out_specs=pl.BlockSpec((DSA_BLOCK, 512), row),
        out_shape=jax.ShapeDtypeStruct((n, 512), BF16),
        scratch_shapes=[pltpu.VMEM((DSA_BLOCK, tp), I32),
                        pltpu.VMEM((IDX_HEADS, DSA_BLOCK, LANES), F32),
                        pltpu.VMEM((DSA_HEADS * DSA_BLOCK, LANES), F32),
                        pltpu.VMEM((DSA_HEADS * DSA_BLOCK, LANES), F32),
                        pltpu.VMEM((DSA_HEADS * DSA_BLOCK, DSA_DIM), F32)],
        compiler_params=_params(("arbitrary", "arbitrary")),
        name="dsa",
    )(dq, iq, misc, kpad, vpad, kipad, wuv)


def _post_body(ret_ref, att_ref, x_ref, w_ref, g_ref, h1_ref, hnt_ref):
    y = jnp.dot(ret_ref[...], w_ref[0:512, :], preferred_element_type=F32)
    y = y + jnp.dot(att_ref[...], w_ref[512:1024, :], preferred_element_type=F32)
    h1 = x_ref[...] + y
    h1_ref[...] = h1
    hnt_ref[...] = _rms(h1, g_ref[...]).T.astype(BF16)


def _post(ret, att, x2d, w_out, g, tm):
    n = x2d.shape[0]
    row = lambda i: (i, 0)
    return pl.pallas_call(
        _post_body,
        grid=(n // tm,),
        in_specs=[pl.BlockSpec((tm, 512), row), pl.BlockSpec((tm, 512), row),
                  pl.BlockSpec((tm, D_MODEL), row),
                  pl.BlockSpec((D_MODEL, D_MODEL), lambda i: (0, 0)),
                  pl.BlockSpec((1, D_MODEL), lambda i: (0, 0))],
        out_specs=(pl.BlockSpec((tm, D_MODEL), row), pl.BlockSpec((D_MODEL, tm), lambda i: (0, i))),
        out_shape=(jax.ShapeDtypeStruct((n, D_MODEL), F32), jax.ShapeDtypeStruct((D_MODEL, n), BF16)),
        compiler_params=_params(("arbitrary",)),
        name="post",
    )(ret, att, x2d, w_out, g)


def _top_desc(s, k, dst_ref, with_rank=False):
    cur = s
    rank = jnp.full(s.shape, float(k), F32) if with_rank else None
    for r in range(k):
        m = jnp.max(cur, axis=0, keepdims=True)
        dst_ref[r:r + 1, :] = m
        if with_rank or r + 1 < k:
            hit = cur == m
            if with_rank:
                rank = jnp.where(hit, float(r), rank)
            cur = jnp.where(hit, -jnp.inf, cur)
    return rank


def _psel_body(hnt_ref, wq_ref, sk_ref, cnt_ref, e1_ref, r2_ref, e2_ref, v1_ref, v2_ref, vc_ref):
    hnt = hnt_ref[...]
    for h in range(PEER_HEADS):
        q = jnp.dot(wq_ref[h * 256:(h + 1) * 256, :], hnt, preferred_element_type=F32)
        s1 = jnp.dot(sk_ref[h, 0], q[:128].astype(BF16), preferred_element_type=F32)
        s2 = jnp.dot(sk_ref[h, 1], q[128:].astype(BF16), preferred_element_type=F32)
        _top_desc(s1, PEER_TOPK, v1_ref)
        rank2 = _top_desc(s2, PEER_TOPK, v2_ref, with_rank=True)
        v1 = v1_ref[...]
        v2 = v2_ref[...]
        cand = jnp.concatenate([v2 + v1[0:1]] + [v2[0:8] + v1[a:a + 1] for a in range(1, 8)]
                               + [v1[8:16] + v2[0:1]], axis=0)
        _top_desc(cand, PEER_TOPK, vc_ref)
        tau = vc_ref[PEER_TOPK - 1:PEER_TOPK, :]
        top = vc_ref[0:1, :]
        zsum = jnp.sum(jnp.where(cand >= tau, jnp.exp(cand - top), 0.0), axis=0, keepdims=True)
        cnt = jnp.zeros(s1.shape, F32)
        for b in range(PEER_TOPK):
            cnt = cnt + jnp.where(s1 + v2[b:b + 1] >= tau, 1.0, 0.0)
        cnt_ref[h] = cnt
        e1_ref[h] = jnp.exp(s1 - v1[0:1]) / zsum
        r2_ref[h] = pltpu.bitcast(rank2.astype(BF16), I32)
        e2_ref[h] = pltpu.bitcast(jnp.exp(s2 - v2[0:1]).astype(BF16), I32)


def _psel(hnt, wq_t, sk, tn):
    n = hnt.shape[1]
    spec = pl.BlockSpec((PEER_HEADS, PEER_NKEYS, tn), lambda i: (0, 0, i))
    shp = jax.ShapeDtypeStruct((PEER_HEADS, PEER_NKEYS, n), F32)
    shp16 = jax.ShapeDtypeStruct((PEER_HEADS, PEER_NKEYS // 2, n), I32)
    spec16 = pl.BlockSpec((PEER_HEADS, PEER_NKEYS // 2, tn), lambda i: (0, 0, i))
    return pl.pallas_call(
        _psel_body,
        grid=(n // tn,),
        in_specs=[pl.BlockSpec((D_MODEL, tn), lambda i: (0, i)),
                  pl.BlockSpec((PEER_HEADS * 256, D_MODEL), lambda i: (0, 0)),
                  pl.BlockSpec((PEER_HEADS, 2, PEER_NKEYS, 128), lambda i: (0, 0, 0, 0))],
        out_specs=(spec, spec, spec16, spec16),
        out_shape=(shp, shp, shp16, shp16),
        scratch_shapes=[pltpu.VMEM((PEER_TOPK, tn), F32)] * 3,
        compiler_params=_params(("arbitrary",)),
        name="psel",
    )(hnt, wq_t, sk)


def _pdense_body(hnt_ref, u_ref, vt_ref, cnt_ref, e1_ref, r2_ref, e2_ref, h1_ref, g_ref,
                 out_ref, acc_ref, coef_ref, *, eb):
    j = pl.program_id(1)

    @pl.when(j == 0)
    def _():
        acc_ref[...] = jnp.zeros(acc_ref.shape, F32)

    hnt = hnt_ref[...]
    t = hnt.shape[1]
    nsub = eb // PEER_NKEYS
    for half in range(eb // PDENSE_ROWS):
        rows = slice(half * PDENSE_ROWS, (half + 1) * PDENSE_ROWS)
        a = jnp.dot(u_ref[rows, :], hnt, preferred_element_type=F32)
        for cp in range(0, PDENSE_ROWS // PEER_NKEYS, PDENSE_GROUP):
            cs = [half * (PDENSE_ROWS // PEER_NKEYS) + cp + k for k in range(PDENSE_GROUP)]
            cnt_rows = [[cnt_ref[h, pl.ds(j * nsub + c, 1), :] for h in range(PEER_HEADS)] for c in cs]
            e1_rows = [[e1_ref[h, pl.ds(j * nsub + c, 1), :] for h in range(PEER_HEADS)] for c in cs]
            for tc in range(t // LANES):
                ts = slice(tc * LANES, (tc + 1) * LANES)
                gates = [None] * PDENSE_GROUP
                for h in range(PEER_HEADS):
                    r2 = pltpu.bitcast(r2_ref[h, :, ts], BF16)
                    e2 = pltpu.bitcast(e2_ref[h, :, ts], BF16)
                    for k in range(PDENSE_GROUP):
                        cnt = jnp.broadcast_to(cnt_rows[k][h][:, ts], (PEER_NKEYS, LANES)).astype(BF16)
                        e1 = jnp.broadcast_to(e1_rows[k][h][:, ts], (PEER_NKEYS, LANES)).astype(BF16)
                        term = jnp.where(r2 < cnt, e2 * e1, jnp.zeros((), BF16))
                        gates[k] = term if gates[k] is None else gates[k] + term
                for k in range(PDENSE_GROUP):
                    r0 = (cp + k) * PEER_NKEYS
                    at = a[r0:r0 + PEER_NKEYS, ts]
                    gelu = (0.5 * at) * (1.0 + lax.erf(at * (0.5 ** 0.5)))
                    coef_ref[cs[k] * PEER_NKEYS:(cs[k] + 1) * PEER_NKEYS, ts] = gelu.astype(BF16) * gates[k]
    acc_ref[...] += jnp.dot(vt_ref[...], coef_ref[...], preferred_element_type=F32)

    @pl.when(j == pl.num_programs(1) - 1)
    def _():
        out_ref[...] = _rms(h1_ref[...] + acc_ref[...].T, g_ref[...])


def _pdense(hnt, u, vt, cnt, e1, r2, e2, h1, g, tn, eb):
    n = hnt.shape[1]
    ne = u.shape[0]
    sel = pl.BlockSpec((PEER_HEADS, PEER_NKEYS, tn), lambda i, j: (0, 0, i))
    sel16 = pl.BlockSpec((PEER_HEADS, PEER_NKEYS // 2, tn), lambda i, j: (0, 0, i))
    return pl.pallas_call(
        functools.partial(_pdense_body, eb=eb),
        grid=(n // tn, ne // eb),
        in_specs=[pl.BlockSpec((D_MODEL, tn), lambda i, j: (0, i)),
                  pl.BlockSpec((eb, D_MODEL), lambda i, j: (j, 0)),
                  pl.BlockSpec((D_MODEL, eb), lambda i, j: (0, j)),
                  sel, sel, sel16, sel16,
                  pl.BlockSpec((tn, D_MODEL), lambda i, j: (i, 0)),
                  pl.BlockSpec((1, D_MODEL), lambda i, j: (0, 0))],
        out_specs=pl.BlockSpec((tn, D_MODEL), lambda i, j: (i, 0)),
        out_shape=jax.ShapeDtypeStruct((n, D_MODEL), F32),
        scratch_shapes=[pltpu.VMEM((D_MODEL, tn), F32), pltpu.VMEM((eb, tn), BF16)],
        compiler_params=_params(("arbitrary", "arbitrary")),
        name="pdense",
    )(hnt, u, vt, cnt, e1, r2, e2, h1, g)


def kernel(x, meta_tokens, norm_mix, w_in, w_uv, w_out, norm_ffn, peer_wq, peer_subkeys,
           peer_u, peer_v, norm_final):
    batch, seq, d = x.shape
    assert d == D_MODEL and norm_mix.shape[0] == 1 and seq % DSA_BLOCK == 0
    n = batch * seq
    topk = min(DSA_TOPK_MAX, seq // 4)
    tm = 512 if seq % 512 == 0 else 256

    w = jnp.pad(w_in[0], ((0, 0), (0, IN_WIDTH_PAD - IN_WIDTH))).astype(BF16)
    g_mix = norm_mix[0][None, :]
    x2d = x.reshape(n, d)
    pos_f = jnp.arange(seq, dtype=F32) + float(N_META)
    pos_m = jnp.arange(N_META, dtype=F32)

    rq, rk, rv, rg, dq, dk, dv, iq, misc = _proj(x2d, g_mix, w, _rope_tables(pos_f), tm, seq // tm)
    _, mk, mv, _, _, mdk, mdv, _, mmisc = _proj(meta_tokens, g_mix, w, _rope_tables(pos_m), N_META, 1)

    ret = _retention(rq, rk, rv, rg, mk, mv, batch, seq)

    def keys(frames, meta):
        width = frames.shape[-1]
        lead = jnp.concatenate([jnp.zeros((LANES - N_META, width), frames.dtype), meta], axis=0)
        lead = jnp.broadcast_to(lead[None], (batch, LANES, width))
        return jnp.concatenate([lead, frames.reshape(batch, seq, width)], axis=1)

    kpad = keys(dk, mdk)
    vpad = keys(dv, mdv)
    kipad = keys(misc[:, :IDX_DIM].astype(BF16), mmisc[:, :IDX_DIM].astype(BF16))
    att = _dsa(dq, iq, misc, kpad, vpad, kipad, w_uv[0].astype(BF16), batch, seq, topk)

    h1, hnt = _post(ret, att, x2d, w_out[0].astype(BF16), norm_ffn[0][None, :], tm)

    wq_t = peer_wq[0].reshape(d, PEER_HEADS * 256).T.astype(BF16)
    cnt, e1, r2, e2 = _psel(hnt, wq_t, peer_subkeys[0].astype(BF16), tm)
    out = _pdense(hnt, peer_u[0].astype(BF16), peer_v[0].T.astype(BF16), cnt, e1, r2, e2, h1,
                  norm_final[None, :], tm, PDENSE_EXPERTS)
    return out.reshape(batch, seq, d)
```

```python
import functools

import jax
import jax.numpy as jnp
from jax import lax
from jax.experimental import pallas as pl
from jax.experimental.pallas import tpu as pltpu

F32 = jnp.float32
BF16 = jnp.bfloat16
I32 = jnp.int32

D_MODEL = 1024
N_META = 16
CHUNK = 64
RET_HEADS = 4
RET_DIM = 128
DSA_HEADS = 8
DSA_DIM = 128
DSA_OUT_DIM = 64
IDX_HEADS = 8
IDX_DIM = 64
DSA_TOPK_MAX = 256
ROPE_THETA = 10000.0
PEER_HEADS = 8
PEER_NKEYS = 128
PEER_TOPK = 16
EPS = 1e-6
NEG = -1e30
INT_MIN = -(2 ** 31)

IN_WIDTH = 3912
IN_WIDTH_PAD = 3968
LANES = 128
RET_BLOCK = 256
PDENSE_EXPERTS = 1024
PDENSE_ROWS = 512
PDENSE_GROUP = 2
DSA_BLOCK = 256
DSA_BITS_PER_CHECK = 4
VMEM_LIMIT = 56 * 1024 * 1024

NT = (((1,), (1,)), ((), ()))
TN = (((0,), (0,)), ((), ()))


def _rms(x, g):
    return x * lax.rsqrt(jnp.mean(x * x, axis=-1, keepdims=True) + EPS) * g


def _params(sem):
    return pltpu.CompilerParams(dimension_semantics=sem, vmem_limit_bytes=VMEM_LIMIT)


def _proj_body(x_ref, g_ref, w_ref, tab_ref, rq_ref, rk_ref, rv_ref, rg_ref, dq_ref, dk_ref,
               dv_ref, iq_ref, misc_ref):
    xn = _rms(x_ref[...], g_ref[...]).astype(BF16)
    cos_a, sin_a, cos_b, sin_b1, sin_b2, cos_m, sin_m1, sin_m2 = (tab_ref[i] for i in range(8))

    def mm(lo, n):
        return jnp.dot(xn, w_ref[:, lo:lo + n], preferred_element_type=F32)

    def rope_a(y):
        return y * cos_a + pltpu.roll(y, 64, 1) * sin_a

    def rope_b(y, c, s1, s2):
        return y * c + pltpu.roll(y, 96, 1) * s1 + pltpu.roll(y, 32, 1) * s2

    y = mm(0, 512)
    for h in range(RET_HEADS):
        rq_ref[:, h * 128:(h + 1) * 128] = rope_a(y[:, h * 128:(h + 1) * 128]).astype(BF16)
    y = mm(512, 512)
    for h in range(RET_HEADS):
        rk_ref[:, h * 128:(h + 1) * 128] = (
            rope_a(y[:, h * 128:(h + 1) * 128]) * (RET_DIM ** -0.5)).astype(BF16)
    rv_ref[...] = mm(1024, 512).astype(BF16)
    rg_ref[...] = mm(1536, 512)
    y = mm(2048, 1024)
    for h in range(DSA_HEADS):
        dq_ref[h] = (rope_a(y[:, h * 128:(h + 1) * 128]) * (DSA_DIM ** -0.5)).astype(BF16)
    dk_ref[...] = rope_a(mm(3072, 128)).astype(BF16)
    dv_ref[...] = mm(3200, 128).astype(BF16)
    y = mm(3328, 512)
    for p in range(IDX_HEADS // 2):
        o = (rope_b(y[:, p * 128:(p + 1) * 128], cos_b, sin_b1, sin_b2) * (IDX_DIM ** -0.5)).astype(BF16)
        iq_ref[2 * p] = o[:, :64]
        iq_ref[2 * p + 1] = o[:, 64:]
    misc_ref[...] = rope_b(mm(3840, 128), cos_m, sin_m1, sin_m2)


def _proj(x2d, g, w, tab, tm, tab_blocks):
    n = x2d.shape[0]
    row = lambda i: (i, 0)
    head = lambda i: (0, i, 0)
    out_shape = (
        jax.ShapeDtypeStruct((n, 512), BF16), jax.ShapeDtypeStruct((n, 512), BF16),
        jax.ShapeDtypeStruct((n, 512), BF16), jax.ShapeDtypeStruct((n, 512), F32),
        jax.ShapeDtypeStruct((DSA_HEADS, n, 128), BF16), jax.ShapeDtypeStruct((n, 128), BF16),
        jax.ShapeDtypeStruct((n, 128), BF16), jax.ShapeDtypeStruct((IDX_HEADS, n, 64), BF16),
        jax.ShapeDtypeStruct((n, 128), F32))
    out_specs = (
        pl.BlockSpec((tm, 512), row), pl.BlockSpec((tm, 512), row), pl.BlockSpec((tm, 512), row),
        pl.BlockSpec((tm, 512), row), pl.BlockSpec((DSA_HEADS, tm, 128), head),
        pl.BlockSpec((tm, 128), row), pl.BlockSpec((tm, 128), row),
        pl.BlockSpec((IDX_HEADS, tm, 64), head), pl.BlockSpec((tm, 128), row))
    return pl.pallas_call(
        _proj_body,
        grid=(n // tm,),
        in_specs=[pl.BlockSpec((tm, D_MODEL), row),
                  pl.BlockSpec((1, D_MODEL), lambda i: (0, 0)),
                  pl.BlockSpec((D_MODEL, IN_WIDTH_PAD), lambda i: (0, 0)),
                  pl.BlockSpec((8, tm, 128), lambda i: (0, i % tab_blocks, 0))],
        out_specs=out_specs,
        out_shape=out_shape,
        compiler_params=_params(("arbitrary",)),
        name="proj",
    )(x2d, g, w, tab)


def _rope_tables(pos):
    p = pos.shape[0]
    inv = ROPE_THETA ** (-jnp.arange(0, 64, dtype=F32) * 2.0 / 128)
    ang = pos[:, None] * inv[None, :]
    c, s = jnp.cos(ang), jnp.sin(ang)
    cos_a = jnp.concatenate([c, c], -1)
    sin_a = jnp.concatenate([-s, s], -1)
    inv = ROPE_THETA ** (-jnp.arange(0, 32, dtype=F32) * 2.0 / 64)
    ang = pos[:, None] * inv[None, :]
    c, s = jnp.cos(ang), jnp.sin(ang)
    z = jnp.zeros_like(s)
    z64 = jnp.zeros((p, 64), F32)
    cos_b = jnp.concatenate([c, c, c, c], -1)
    sin_b1 = jnp.concatenate([-s, z, -s, z], -1)
    sin_b2 = jnp.concatenate([z, s, z, s], -1)
    cos_m = jnp.concatenate([c, c, jnp.full((p, 64), IDX_HEADS ** -0.5, F32)], -1)
    sin_m1 = jnp.concatenate([-s, z, z64], -1)
    sin_m2 = jnp.concatenate([z, s, z64], -1)
    return jnp.stack([cos_a, sin_a, cos_b, sin_b1, sin_b2, cos_m, sin_m1, sin_m2])


def _ret_body(rq_ref, rk_ref, rv_ref, rg_ref, mk_ref, mv_ref, dmat_ref, xi_ref, zeta_ref,
              mzeta_ref, gsc_ref, out_ref, r_ref):
    @pl.when(pl.program_id(1) == 0)
    def _():
        for h in range(RET_HEADS):
            hs = slice(h * 128, (h + 1) * 128)
            kz = (mk_ref[:, hs].astype(F32) * mzeta_ref[h]).astype(BF16)
            r_ref[h] = lax.dot_general(kz, mv_ref[:, hs], TN, preferred_element_type=F32)

    for h in range(RET_HEADS):
        hs = slice(h * 128, (h + 1) * 128)
        q = rq_ref[:, hs]
        k = rk_ref[:, hs]
        v = rv_ref[:, hs]
        s = lax.dot_general(q, k, NT, preferred_element_type=F32) * dmat_ref[h]
        o = jnp.dot(s.astype(BF16), v, preferred_element_type=F32)
        r = r_ref[h]
        qx = (q.astype(F32) * xi_ref[h]).astype(BF16)
        o = o + jnp.dot(qx, r.astype(BF16), preferred_element_type=F32)
        kz = (k.astype(F32) * zeta_ref[h]).astype(BF16)
        u = lax.dot_general(kz, v, TN, preferred_element_type=F32)
        r_ref[h] = r * gsc_ref[h] + u
        mu = jnp.mean(o, axis=-1, keepdims=True)
        d = o - mu
        var = jnp.mean(d * d, axis=-1, keepdims=True)
        on = d * lax.rsqrt(var + EPS)
        g = rg_ref[:, hs]
        out_ref[:, hs] = (g * jax.nn.sigmoid(g) * on).astype(BF16)


def _retention(rq, rk, rv, rg, mk, mv, batch, seq):
    n = rq.shape[0]
    nb = seq // RET_BLOCK
    lg = jnp.log(1.0 - 2.0 ** (-5.0 - jnp.arange(RET_HEADS, dtype=F32)))
    i = jnp.arange(RET_BLOCK, dtype=F32)
    ci = jnp.arange(RET_BLOCK) // CHUNK
    vis = (ci[None, :] <= ci[:, None])
    dmat = jnp.where(vis[None], jnp.exp(lg[:, None, None] * jnp.abs(i[:, None] - i[None, :])), 0.0)
    xi = jnp.broadcast_to(jnp.exp(lg[:, None] * (i + 1.0)[None, :])[:, :, None], (RET_HEADS, RET_BLOCK, 128))
    zeta = jnp.broadcast_to(jnp.exp(lg[:, None] * (RET_BLOCK - 1.0 - i)[None, :])[:, :, None],
                            (RET_HEADS, RET_BLOCK, 128))
    im = jnp.arange(N_META, dtype=F32)
    mzeta = jnp.broadcast_to(jnp.exp(lg[:, None] * (N_META - 1.0 - im)[None, :])[:, :, None],
                             (RET_HEADS, N_META, 128))
    gsc = jnp.broadcast_to(jnp.exp(lg * RET_BLOCK)[:, None, None], (RET_HEADS, 1, 128))
    row = lambda b, s: (b * nb + s, 0)
    c2 = lambda b, s: (0, 0)
    c3 = lambda b, s: (0, 0, 0)
    return pl.pallas_call(
        _ret_body,
        grid=(batch, nb),
        in_specs=[pl.BlockSpec((RET_BLOCK, 512), row), pl.BlockSpec((RET_BLOCK, 512), row),
                  pl.BlockSpec((RET_BLOCK, 512), row), pl.BlockSpec((RET_BLOCK, 512), row),
                  pl.BlockSpec((N_META, 512), c2), pl.BlockSpec((N_META, 512), c2),
                  pl.BlockSpec((RET_HEADS, RET_BLOCK, RET_BLOCK), c3),
                  pl.BlockSpec((RET_HEADS, RET_BLOCK, 128), c3),
                  pl.BlockSpec((RET_HEADS, RET_BLOCK, 128), c3),
                  pl.BlockSpec((RET_HEADS, N_META, 128), c3),
                  pl.BlockSpec((RET_HEADS, 1, 128), c3)],
        out_specs=pl.BlockSpec((RET_BLOCK, 512), row),
        out_shape=jax.ShapeDtypeStruct((n, 512), BF16),
        scratch_shapes=[pltpu.VMEM((RET_HEADS, 128, 128), F32)],
        compiler_params=_params(("arbitrary", "arbitrary")),
        name="retention",
    )(rq, rk, rv, rg, mk, mv, dmat, xi, zeta, mzeta, gsc)


def _aligned(off):
    return off if isinstance(off, int) else pl.multiple_of(off, LANES)


def _dsa_body(qa_ref, qi_ref, misc_ref, k_ref, v_ref, ki_ref, wuv_ref, out_ref,
              isc_ref, bias_ref, m_ref, l_ref, acc_ref, *, topk):
    qb = pl.program_id(1)
    tq = DSA_BLOCK
    nh = DSA_HEADS

    wt = misc_ref[...].T
    qi = qi_ref[...].reshape(IDX_HEADS * tq, IDX_DIM)

    def isc_tile(koff, width, allowed):
        kt = ki_ref[pl.ds(koff, width), :]
        z = lax.dot_general(kt, qi, NT, preferred_element_type=F32)
        isc = None
        for h in range(IDX_HEADS):
            term = jnp.maximum(z[:, h * tq:(h + 1) * tq], 0.0) * wt[64 + h:65 + h, :]
            isc = term if isc is None else isc + term
        if allowed is not None:
            isc = jnp.where(allowed, isc, -jnp.inf)
        isc_ref[pl.ds(koff, width), :] = isc

    isc_tile(0, LANES, lax.broadcasted_iota(I32, (LANES, tq), 0) >= LANES - N_META)

    def body_a(t, c):
        isc_tile(pl.multiple_of(LANES + t * tq, LANES), tq, None)
        return c

    lax.fori_loop(0, qb, body_a, 0)
    keyc = lax.broadcasted_iota(I32, (tq, tq), 0) // CHUNK
    qryc = lax.broadcasted_iota(I32, (tq, tq), 1) // CHUNK
    diag_off = pl.multiple_of(LANES + qb * tq, LANES)
    isc_tile(diag_off, tq, keyc <= qryc)

    kf = float(topk)
    sub = 8

    def key_to_f32(key):
        return lax.bitcast_convert_type(jnp.where(key < 0, key ^ jnp.int32(0x7FFFFFFF), key), F32)

    def count(pred):
        def slab(off, a):
            hit = jnp.where(pred(isc_ref[pl.ds(off, LANES), :], off), 1.0, 0.0)
            return a + jnp.sum(hit.reshape(LANES // sub, sub, tq), axis=0)

        def body(t, a):
            off = pl.multiple_of(LANES + t * (2 * LANES), LANES)
            return slab(pl.multiple_of(off + LANES, LANES), slab(off, a))

        a = lax.fori_loop(0, qb + 1, body, slab(0, jnp.zeros((sub, tq), F32)))
        return jnp.sum(a, axis=0, keepdims=True)

    def count_ge(cand):
        return count(lambda x, off: x >= cand)

    lowest = jnp.full((1, tq), -3.0e38, F32)
    n_real = count_ge(lowest)
    c_pos = count_ge(jnp.zeros((1, tq), F32))
    pos = c_pos >= kf
    base0 = jnp.where(pos, jnp.int32(0), jnp.int32(INT_MIN))
    cnt0 = jnp.where(n_real <= kf, kf, jnp.where(pos, c_pos, n_real))

    def unfinished(cnt):
        return jnp.max(jnp.where(cnt != kf, 1.0, 0.0))

    def try_bit(bit, base, cnt):
        cand = base | lax.shift_left(jnp.int32(1), bit)
        c = count_ge(key_to_f32(cand))
        ok = c >= kf
        return jnp.where(ok, cand, base), jnp.where(ok, c, cnt)

    base, cnt = base0, cnt0
    for bit in range(30, DSA_BITS_PER_CHECK * 7 - 1, -1):
        base, cnt = try_bit(bit, base, cnt)

    def group_cond(state):
        g, _, cnt = state
        return jnp.logical_and(g >= 0, unfinished(cnt) > 0.0)

    def group_body(state):
        g, base, cnt = state
        for k in range(DSA_BITS_PER_CHECK - 1, -1, -1):
            base, cnt = try_bit(g * DSA_BITS_PER_CHECK + k, base, cnt)
        return g - 1, base, cnt

    _, base, cnt = lax.while_loop(group_cond, group_body, (jnp.int32(6), base, cnt))

    def refine_cond(state):
        it, _, _, _, go = state
        return jnp.logical_and(it < 40, go > 0.0)

    def refine_body(state):
        it, lo, hi, cnt, _ = state
        mid = lo + 0.5 * (hi - lo)
        c = count_ge(mid)
        ok = c >= kf
        cnt = jnp.where(ok, c, cnt)
        moving = jnp.logical_and(jnp.logical_and(mid > lo, mid < hi), cnt != kf)
        return it + 1, jnp.where(ok, mid, lo), jnp.where(ok, hi, mid), cnt, jnp.max(jnp.where(moving, 1.0, 0.0))

    _, thr, _, cnt, _ = lax.while_loop(
        refine_cond, refine_body,
        (jnp.int32(0), key_to_f32(base), key_to_f32(base + 1), cnt, unfinished(cnt)))
    thr = jnp.where(n_real <= kf, lowest, thr)

    key_id = lax.broadcasted_iota(I32, (LANES, tq), 0)
    index_bits = max(1, (isc_ref.shape[0] - 1).bit_length())

    def tie_bound():
        need = kf - count(lambda x, off: x > thr)

        def ties_below(bound):
            return count(lambda x, off: jnp.logical_and(x == thr, key_id < bound - off))

        def bit_step(i, last):
            cand = last | lax.shift_left(jnp.int32(1), index_bits - 1 - i)
            return jnp.where(ties_below(cand) < need, cand, last)

        return lax.fori_loop(0, index_bits, bit_step, jnp.zeros((1, tq), I32)) + 1

    everything = jnp.full((1, tq), isc_ref.shape[0], I32)
    tie_end = lax.cond(unfinished(cnt) > 0.0, tie_bound, lambda: everything)
    tie_end = jnp.where(cnt == kf, everything, tie_end)

    def mask_tile(off, width):
        x = isc_ref[pl.ds(off, width), :]
        ids = lax.broadcasted_iota(I32, (width, tq), 0)
        keep = jnp.logical_or(x > thr, jnp.logical_and(x == thr, ids < tie_end - off))
        bias_ref[:, pl.ds(off, width)] = jnp.where(keep, 0.0, NEG).T

    def body_m(t, c):
        mask_tile(pl.multiple_of(LANES + t * tq, LANES), tq)
        return c

    mask_tile(0, LANES)
    lax.fori_loop(0, qb + 1, body_m, 0)

    m_ref[...] = jnp.full(m_ref.shape, NEG, F32)
    l_ref[...] = jnp.zeros(l_ref.shape, F32)
    acc_ref[...] = jnp.zeros(acc_ref.shape, F32)
    qa = qa_ref[...].reshape(nh * tq, DSA_DIM)

    def att_tile(koff, width):
        kt = k_ref[pl.ds(koff, width), :]
        vt = v_ref[pl.ds(koff, width), :]
        s = lax.dot_general(qa, kt, NT, preferred_element_type=F32).reshape(nh, tq, width)
        reps = width // LANES
        s = (s + bias_ref[:, pl.ds(koff, width)][None]).reshape(nh * tq, width)
        m_prev = m_ref[...]
        m_new = jnp.maximum(m_prev, jnp.max(s, axis=1, keepdims=True))
        alpha = jnp.exp(m_prev - m_new)
        m_w = m_new if reps == 1 else jnp.concatenate([m_new] * reps, axis=1)
        p = jnp.exp(s - m_w)
        l_ref[...] = alpha * l_ref[...] + jnp.sum(p, axis=1, keepdims=True)
        acc_ref[...] = alpha * acc_ref[...] + jnp.dot(p.astype(BF16), vt, preferred_element_type=F32)
        m_ref[...] = m_new

    att_tile(0, LANES)

    def body_c(t, c):
        att_tile(pl.multiple_of(LANES + t * tq, LANES), tq)
        return c

    lax.fori_loop(0, qb + 1, body_c, 0)

    o = (acc_ref[...] / l_ref[...]).astype(BF16).reshape(nh, tq, DSA_DIM)
    for h in range(nh):
        out_ref[:, h * DSA_OUT_DIM:(h + 1) * DSA_OUT_DIM] = jnp.dot(
            o[h], wuv_ref[h], preferred_element_type=F32).astype(BF16)


def _dsa(dq, iq, misc, kpad, vpad, kipad, wuv, batch, seq, topk):
    n = misc.shape[0]
    nq = seq // DSA_BLOCK
    tp = kpad.shape[1]
    head = lambda b, q: (0, b * nq + q, 0)
    row = lambda b, q: (b * nq + q, 0)
    per_b = lambda b, q: (b, 0, 0)
    return pl.pallas_call(
        functools.partial(_dsa_body, topk=topk),
        grid=(batch, nq),
        in_specs=[pl.BlockSpec((DSA_HEADS, DSA_BLOCK, DSA_DIM), head),
                  pl.BlockSpec((IDX_HEADS, DSA_BLOCK, IDX_DIM), head),
                  pl.BlockSpec((DSA_BLOCK, 128), row),
                  pl.BlockSpec((None, tp, DSA_DIM), per_b),
                  pl.BlockSpec((None, tp, DSA_DIM), per_b),
                  pl.BlockSpec((None, tp, IDX_DIM), per_b),
                  pl.BlockSpec((DSA_HEADS, DSA_DIM, DSA_OUT_DIM), lambda b, q: (0, 0, 0))],
        out_specs=pl.BlockSpec((DSA_BLOCK, 512), row),
        out_shape=jax.ShapeDtypeStruct((n, 512), BF16),
        scratch_shapes=[pltpu.VMEM((tp, DSA_BLOCK), F32),
                        pltpu.VMEM((DSA_BLOCK, tp), F32),
                        pltpu.VMEM((DSA_HEADS * DSA_BLOCK, LANES), F32),
                        pltpu.VMEM((DSA_HEADS * DSA_BLOCK, LANES), F32),
                        pltpu.VMEM((DSA_HEADS * DSA_BLOCK, DSA_DIM), F32)],
        compiler_params=_params(("arbitrary", "arbitrary")),
        name="dsa",
    )(dq, iq, misc, kpad, vpad, kipad, wuv)


def _post_body(ret_ref, att_ref, x_ref, w_ref, g_ref, h1_ref, hnt_ref):
    y = jnp.dot(ret_ref[...], w_ref[0:512, :], preferred_element_type=F32)
    y = y + jnp.dot(att_ref[...], w_ref[512:1024, :], preferred_element_type=F32)
    h1 = x_ref[...] + y
    h1_ref[...] = h1
    hnt_ref[...] = _rms(h1, g_ref[...]).T.astype(BF16)


def _post(ret, att, x2d, w_out, g, tm):
    n = x2d.shape[0]
    row = lambda i: (i, 0)
    return pl.pallas_call(
        _post_body,
        grid=(n // tm,),
        in_specs=[pl.BlockSpec((tm, 512), row), pl.BlockSpec((tm, 512), row),
                  pl.BlockSpec((tm, D_MODEL), row),
                  pl.BlockSpec((D_MODEL, D_MODEL), lambda i: (0, 0)),
                  pl.BlockSpec((1, D_MODEL), lambda i: (0, 0))],
        out_specs=(pl.BlockSpec((tm, D_MODEL), row), pl.BlockSpec((D_MODEL, tm), lambda i: (0, i))),
        out_shape=(jax.ShapeDtypeStruct((n, D_MODEL), F32), jax.ShapeDtypeStruct((D_MODEL, n), BF16)),
        compiler_params=_params(("arbitrary",)),
        name="post",
    )(ret, att, x2d, w_out, g)


def _top_desc(s, k, dst_ref, with_rank=False):
    cur = s
    rank = jnp.full(s.shape, float(k), F32) if with_rank else None
    for r in range(k):
        m = jnp.max(cur, axis=0, keepdims=True)
        dst_ref[r:r + 1, :] = m
        if with_rank or r + 1 < k:
            hit = cur == m
            if with_rank:
                rank = jnp.where(hit, float(r), rank)
            cur = jnp.where(hit, -jnp.inf, cur)
    return rank


def _psel_body(hnt_ref, wq_ref, sk_ref, cnt_ref, e1_ref, r2_ref, e2_ref, v1_ref, v2_ref, vc_ref):
    hnt = hnt_ref[...]
    for h in range(PEER_HEADS):
        q = jnp.dot(wq_ref[h * 256:(h + 1) * 256, :], hnt, preferred_element_type=F32)
        s1 = jnp.dot(sk_ref[h, 0], q[:128].astype(BF16), preferred_element_type=F32)
        s2 = jnp.dot(sk_ref[h, 1], q[128:].astype(BF16), preferred_element_type=F32)
        _top_desc(s1, PEER_TOPK, v1_ref)
        rank2 = _top_desc(s2, PEER_TOPK, v2_ref, with_rank=True)
        v1 = v1_ref[...]
        v2 = v2_ref[...]
        cand = jnp.concatenate([v2 + v1[0:1]] + [v2[0:8] + v1[a:a + 1] for a in range(1, 8)]
                               + [v1[8:16] + v2[0:1]], axis=0)
        _top_desc(cand, PEER_TOPK, vc_ref)
        tau = vc_ref[PEER_TOPK - 1:PEER_TOPK, :]
        top = vc_ref[0:1, :]
        zsum = jnp.sum(jnp.where(cand >= tau, jnp.exp(cand - top), 0.0), axis=0, keepdims=True)
        cnt = jnp.zeros(s1.shape, F32)
        for b in range(PEER_TOPK):
            cnt = cnt + jnp.where(s1 + v2[b:b + 1] >= tau, 1.0, 0.0)
        cnt_ref[h] = cnt
        e1_ref[h] = jnp.exp(s1 - v1[0:1]) / zsum
        r2_ref[h] = pltpu.bitcast(rank2.astype(BF16), I32)
        e2_ref[h] = pltpu.bitcast(jnp.exp(s2 - v2[0:1]).astype(BF16), I32)


def _psel(hnt, wq_t, sk, tn):
    n = hnt.shape[1]
    spec = pl.BlockSpec((PEER_HEADS, PEER_NKEYS, tn), lambda i: (0, 0, i))
    shp32 = jax.ShapeDtypeStruct((PEER_HEADS, PEER_NKEYS, n), F32)
    shp16 = jax.ShapeDtypeStruct((PEER_HEADS, PEER_NKEYS // 2, n), I32)
    spec16 = pl.BlockSpec((PEER_HEADS, PEER_NKEYS // 2, tn), lambda i: (0, 0, i))
    return pl.pallas_call(
        _psel_body,
        grid=(n // tn,),
        in_specs=[pl.BlockSpec((D_MODEL, tn), lambda i: (0, i)),
                  pl.BlockSpec((PEER_HEADS * 256, D_MODEL), lambda i: (0, 0)),
                  pl.BlockSpec((PEER_HEADS, 2, PEER_NKEYS, 128), lambda i: (0, 0, 0, 0))],
        out_specs=(spec, spec, spec16, spec16),
        out_shape=(shp32, shp32, shp16, shp16),
        scratch_shapes=[pltpu.VMEM((PEER_TOPK, tn), F32)] * 3,
        compiler_params=_params(("arbitrary",)),
        name="psel",
    )(hnt, wq_t, sk)


def _pdense_body(*refs, eb, nchunk):
    hnt_ref, u_ref, vt_ref = refs[:3]
    cnt_refs = refs[3:3 + nchunk]
    e1_refs = refs[3 + nchunk:3 + 2 * nchunk]
    r2_ref, e2_ref, h1_ref, g_ref, out_ref, acc_ref, coef_ref = refs[3 + 2 * nchunk:]
    j = pl.program_id(1)

    @pl.when(j == 0)
    def _():
        acc_ref[...] = jnp.zeros(acc_ref.shape, F32)

    nsub = eb // PEER_NKEYS

    def row_bf16(ref, h, i1):
        return jnp.broadcast_to(ref[h, pl.ds(i1, 1), :], (PEER_NKEYS, LANES)).astype(BF16)

    for half in range(eb // PDENSE_ROWS):
        rows = slice(half * PDENSE_ROWS, (half + 1) * PDENSE_ROWS)
        a = jnp.dot(u_ref[rows, :], hnt_ref[...], preferred_element_type=F32)
        for cp in range(0, PDENSE_ROWS // PEER_NKEYS, PDENSE_GROUP):
            cs = [half * (PDENSE_ROWS // PEER_NKEYS) + cp + k for k in range(PDENSE_GROUP)]
            for tc in range(nchunk):
                ts = slice(tc * LANES, (tc + 1) * LANES)
                gates = [None] * PDENSE_GROUP
                for h in range(PEER_HEADS):
                    r2 = pltpu.bitcast(r2_ref[h, :, ts], BF16)
                    e2 = pltpu.bitcast(e2_ref[h, :, ts], BF16)
                    for k in range(PDENSE_GROUP):
                        cnt = row_bf16(cnt_refs[tc], h, j * nsub + cs[k])
                        e1 = row_bf16(e1_refs[tc], h, j * nsub + cs[k])
                        term = jnp.where(r2 < cnt, e2 * e1, jnp.zeros((), BF16))
                        gates[k] = term if gates[k] is None else gates[k] + term
                for k in range(PDENSE_GROUP):
                    r0 = (cp + k) * PEER_NKEYS
                    at = a[r0:r0 + PEER_NKEYS, ts]
                    gelu = (0.5 * at) * (1.0 + lax.erf(at * (0.5 ** 0.5)))
                    coef_ref[cs[k] * PEER_NKEYS:(cs[k] + 1) * PEER_NKEYS, ts] = gelu.astype(BF16) * gates[k]
    acc_ref[...] += jnp.dot(vt_ref[...], coef_ref[...], preferred_element_type=F32)

    @pl.when(j == pl.num_programs(1) - 1)
    def _():
        out_ref[...] = _rms(h1_ref[...] + acc_ref[...].T, g_ref[...])


def _pdense(hnt, u, vt, cnt, e1, r2, e2, h1, g, tn, eb):
    n = hnt.shape[1]
    ne = u.shape[0]
    nchunk = tn // LANES
    chunk = [pl.BlockSpec((PEER_HEADS, PEER_NKEYS, LANES), lambda i, j, c=c: (0, 0, i * nchunk + c))
             for c in range(nchunk)]
    sel16 = pl.BlockSpec((PEER_HEADS, PEER_NKEYS // 2, tn), lambda i, j: (0, 0, i))
    return pl.pallas_call(
        functools.partial(_pdense_body, eb=eb, nchunk=nchunk),
        grid=(n // tn, ne // eb),
        in_specs=[pl.BlockSpec((D_MODEL, tn), lambda i, j: (0, i)),
                  pl.BlockSpec((eb, D_MODEL), lambda i, j: (j, 0)),
                  pl.BlockSpec((D_MODEL, eb), lambda i, j: (0, j))]
                 + chunk + chunk
                 + [sel16, sel16,
                    pl.BlockSpec((tn, D_MODEL), lambda i, j: (i, 0)),
                    pl.BlockSpec((1, D_MODEL), lambda i, j: (0, 0))],
        out_specs=pl.BlockSpec((tn, D_MODEL), lambda i, j: (i, 0)),
        out_shape=jax.ShapeDtypeStruct((n, D_MODEL), F32),
        scratch_shapes=[pltpu.VMEM((D_MODEL, tn), F32), pltpu.VMEM((eb, tn), BF16)],
        compiler_params=_params(("arbitrary", "arbitrary")),
        name="pdense",
    )(hnt, u, vt, *([cnt] * nchunk), *([e1] * nchunk), r2, e2, h1, g)


def kernel(x, meta_tokens, norm_mix, w_in, w_uv, w_out, norm_ffn, peer_wq, peer_subkeys,
           peer_u, peer_v, norm_final):
    batch, seq, d = x.shape
    assert d == D_MODEL and norm_mix.shape[0] == 1 and seq % DSA_BLOCK == 0
    n = batch * seq
    topk = min(DSA_TOPK_MAX, seq // 4)
    tm = 512 if seq % 512 == 0 else 256

    w = jnp.pad(w_in[0], ((0, 0), (0, IN_WIDTH_PAD - IN_WIDTH))).astype(BF16)
    g_mix = norm_mix[0][None, :]
    x2d = x.reshape(n, d)
    pos_f = jnp.arange(seq, dtype=F32) + float(N_META)
    pos_m = jnp.arange(N_META, dtype=F32)

    rq, rk, rv, rg, dq, dk, dv, iq, misc = _proj(x2d, g_mix, w, _rope_tables(pos_f), tm, seq // tm)
    _, mk, mv, _, _, mdk, mdv, _, mmisc = _proj(meta_tokens, g_mix, w, _rope_tables(pos_m), N_META, 1)

    ret = _retention(rq, rk, rv, rg, mk, mv, batch, seq)

    def keys(frames, meta):
        width = frames.shape[-1]
        lead = jnp.concatenate([jnp.zeros((LANES - N_META, width), frames.dtype), meta], axis=0)
        lead = jnp.broadcast_to(lead[None], (batch, LANES, width))
        return jnp.concatenate([lead, frames.reshape(batch, seq, width)], axis=1)

    kpad = keys(dk, mdk)
    vpad = keys(dv, mdv)
    kipad = keys(misc[:, :IDX_DIM].astype(BF16), mmisc[:, :IDX_DIM].astype(BF16))
    att = _dsa(dq, iq, misc, kpad, vpad, kipad, w_uv[0].astype(BF16), batch, seq, topk)

    h1, hnt = _post(ret, att, x2d, w_out[0].astype(BF16), norm_ffn[0][None, :], tm)

    wq_t = peer_wq[0].reshape(d, PEER_HEADS * 256).T.astype(BF16)
    cnt, e1, r2, e2 = _psel(hnt, wq_t, peer_subkeys[0].astype(BF16), tm)
    out = _pdense(hnt, peer_u[0].astype(BF16), peer_v[0].T.astype(BF16), cnt, e1, r2, e2, h1,
                  norm_final[None, :], tm, PDENSE_EXPERTS)
    return out.reshape(batch, seq, d)
```

```python
import functools

import jax
import jax.numpy as jnp
from jax import lax
from jax.experimental import pallas as pl
from jax.experimental.pallas import tpu as pltpu

F32 = jnp.float32
BF16 = jnp.bfloat16
I32 = jnp.int32

D_MODEL = 1024
N_META = 16
CHUNK = 64
RET_HEADS = 4
RET_DIM = 128
DSA_HEADS = 8
DSA_DIM = 128
DSA_OUT_DIM = 64
IDX_HEADS = 8
IDX_DIM = 64
DSA_TOPK_MAX = 256
ROPE_THETA = 10000.0
PEER_HEADS = 8
PEER_NKEYS = 128
PEER_TOPK = 16
EPS = 1e-6
NEG = -1e30
INT_MIN = -(2 ** 31)

IN_WIDTH = 3912
IN_WIDTH_PAD = 3968
LANES = 128
RET_BLOCK = 256
PDENSE_EXPERTS = 1024
PDENSE_ROWS = 512
PDENSE_GROUP = 2
DSA_BLOCK = 256
DSA_BITS_PER_CHECK = 4
VMEM_LIMIT = 56 * 1024 * 1024

NT = (((1,), (1,)), ((), ()))
TN = (((0,), (0,)), ((), ()))


def _rms(x, g):
    return x * lax.rsqrt(jnp.mean(x * x, axis=-1, keepdims=True) + EPS) * g


def _params(sem):
    return pltpu.CompilerParams(dimension_semantics=sem, vmem_limit_bytes=VMEM_LIMIT)


def _proj_body(x_ref, g_ref, w_ref, tab_ref, rq_ref, rk_ref, rv_ref, rg_ref, dq_ref, dk_ref,
               dv_ref, iq_ref, misc_ref):
    xn = _rms(x_ref[...], g_ref[...]).astype(BF16)
    cos_a, sin_a, cos_b, sin_b1, sin_b2, cos_m, sin_m1, sin_m2 = (tab_ref[i] for i in range(8))

    def mm(lo, n):
        return jnp.dot(xn, w_ref[:, lo:lo + n], preferred_element_type=F32)

    def rope_a(y):
        return y * cos_a + pltpu.roll(y, 64, 1) * sin_a

    def rope_b(y, c, s1, s2):
        return y * c + pltpu.roll(y, 96, 1) * s1 + pltpu.roll(y, 32, 1) * s2

    y = mm(0, 512)
    for h in range(RET_HEADS):
        rq_ref[:, h * 128:(h + 1) * 128] = rope_a(y[:, h * 128:(h + 1) * 128]).astype(BF16)
    y = mm(512, 512)
    for h in range(RET_HEADS):
        rk_ref[:, h * 128:(h + 1) * 128] = (
            rope_a(y[:, h * 128:(h + 1) * 128]) * (RET_DIM ** -0.5)).astype(BF16)
    rv_ref[...] = mm(1024, 512).astype(BF16)
    rg_ref[...] = mm(1536, 512)
    y = mm(2048, 1024)
    for h in range(DSA_HEADS):
        dq_ref[h] = (rope_a(y[:, h * 128:(h + 1) * 128]) * (DSA_DIM ** -0.5)).astype(BF16)
    dk_ref[...] = rope_a(mm(3072, 128)).astype(BF16)
    dv_ref[...] = mm(3200, 128).astype(BF16)
    y = mm(3328, 512)
    for p in range(IDX_HEADS // 2):
        o = (rope_b(y[:, p * 128:(p + 1) * 128], cos_b, sin_b1, sin_b2) * (IDX_DIM ** -0.5)).astype(BF16)
        iq_ref[2 * p] = o[:, :64]
        iq_ref[2 * p + 1] = o[:, 64:]
    misc_ref[...] = rope_b(mm(3840, 128), cos_m, sin_m1, sin_m2)


def _proj(x2d, g, w, tab, tm, tab_blocks):
    n = x2d.shape[0]
    row = lambda i: (i, 0)
    head = lambda i: (0, i, 0)
    out_shape = (
        jax.ShapeDtypeStruct((n, 512), BF16), jax.ShapeDtypeStruct((n, 512), BF16),
        jax.ShapeDtypeStruct((n, 512), BF16), jax.ShapeDtypeStruct((n, 512), F32),
        jax.ShapeDtypeStruct((DSA_HEADS, n, 128), BF16), jax.ShapeDtypeStruct((n, 128), BF16),
        jax.ShapeDtypeStruct((n, 128), BF16), jax.ShapeDtypeStruct((IDX_HEADS, n, 64), BF16),
        jax.ShapeDtypeStruct((n, 128), F32))
    out_specs = (
        pl.BlockSpec((tm, 512), row), pl.BlockSpec((tm, 512), row), pl.BlockSpec((tm, 512), row),
        pl.BlockSpec((tm, 512), row), pl.BlockSpec((DSA_HEADS, tm, 128), head),
        pl.BlockSpec((tm, 128), row), pl.BlockSpec((tm, 128), row),
        pl.BlockSpec((IDX_HEADS, tm, 64), head), pl.BlockSpec((tm, 128), row))
    return pl.pallas_call(
        _proj_body,
        grid=(n // tm,),
        in_specs=[pl.BlockSpec((tm, D_MODEL), row),
                  pl.BlockSpec((1, D_MODEL), lambda i: (0, 0)),
                  pl.BlockSpec((D_MODEL, IN_WIDTH_PAD), lambda i: (0, 0)),
                  pl.BlockSpec((8, tm, 128), lambda i: (0, i % tab_blocks, 0))],
        out_specs=out_specs,
        out_shape=out_shape,
        compiler_params=_params(("arbitrary",)),
        name="proj",
    )(x2d, g, w, tab)


def _rope_tables(pos):
    p = pos.shape[0]
    inv = ROPE_THETA ** (-jnp.arange(0, 64, dtype=F32) * 2.0 / 128)
    ang = pos[:, None] * inv[None, :]
    c, s = jnp.cos(ang), jnp.sin(ang)
    cos_a = jnp.concatenate([c, c], -1)
    sin_a = jnp.concatenate([-s, s], -1)
    inv = ROPE_THETA ** (-jnp.arange(0, 32, dtype=F32) * 2.0 / 64)
    ang = pos[:, None] * inv[None, :]
    c, s = jnp.cos(ang), jnp.sin(ang)
    z = jnp.zeros_like(s)
    z64 = jnp.zeros((p, 64), F32)
    cos_b = jnp.concatenate([c, c, c, c], -1)
    sin_b1 = jnp.concatenate([-s, z, -s, z], -1)
    sin_b2 = jnp.concatenate([z, s, z, s], -1)
    cos_m = jnp.concatenate([c, c, jnp.full((p, 64), IDX_HEADS ** -0.5, F32)], -1)
    sin_m1 = jnp.concatenate([-s, z, z64], -1)
    sin_m2 = jnp.concatenate([z, s, z64], -1)
    return jnp.stack([cos_a, sin_a, cos_b, sin_b1, sin_b2, cos_m, sin_m1, sin_m2])


def _ret_body(rq_ref, rk_ref, rv_ref, rg_ref, mk_ref, mv_ref, dmat_ref, xi_ref, zeta_ref,
              mzeta_ref, gsc_ref, out_ref, r_ref):
    @pl.when(pl.program_id(1) == 0)
    def _():
        for h in range(RET_HEADS):
            hs = slice(h * 128, (h + 1) * 128)
            kz = (mk_ref[:, hs].astype(F32) * mzeta_ref[h]).astype(BF16)
            r_ref[h] = lax.dot_general(kz, mv_ref[:, hs], TN, preferred_element_type=F32)

    for h in range(RET_HEADS):
        hs = slice(h * 128, (h + 1) * 128)
        q = rq_ref[:, hs]
        k = rk_ref[:, hs]
        v = rv_ref[:, hs]
        s = lax.dot_general(q, k, NT, preferred_element_type=F32) * dmat_ref[h]
        o = jnp.dot(s.astype(BF16), v, preferred_element_type=F32)
        r = r_ref[h]
        qx = (q.astype(F32) * xi_ref[h]).astype(BF16)
        o = o + jnp.dot(qx, r.astype(BF16), preferred_element_type=F32)
        kz = (k.astype(F32) * zeta_ref[h]).astype(BF16)
        u = lax.dot_general(kz, v, TN, preferred_element_type=F32)
        r_ref[h] = r * gsc_ref[h] + u
        mu = jnp.mean(o, axis=-1, keepdims=True)
        d = o - mu
        var = jnp.mean(d * d, axis=-1, keepdims=True)
        on = d * lax.rsqrt(var + EPS)
        g = rg_ref[:, hs]
        out_ref[:, hs] = (g * jax.nn.sigmoid(g) * on).astype(BF16)


def _retention(rq, rk, rv, rg, mk, mv, batch, seq):
    n = rq.shape[0]
    nb = seq // RET_BLOCK
    lg = jnp.log(1.0 - 2.0 ** (-5.0 - jnp.arange(RET_HEADS, dtype=F32)))
    i = jnp.arange(RET_BLOCK, dtype=F32)
    ci = jnp.arange(RET_BLOCK) // CHUNK
    vis = (ci[None, :] <= ci[:, None])
    dmat = jnp.where(vis[None], jnp.exp(lg[:, None, None] * jnp.abs(i[:, None] - i[None, :])), 0.0)
    xi = jnp.broadcast_to(jnp.exp(lg[:, None] * (i + 1.0)[None, :])[:, :, None], (RET_HEADS, RET_BLOCK, 128))
    zeta = jnp.broadcast_to(jnp.exp(lg[:, None] * (RET_BLOCK - 1.0 - i)[None, :])[:, :, None],
                            (RET_HEADS, RET_BLOCK, 128))
    im = jnp.arange(N_META, dtype=F32)
    mzeta = jnp.broadcast_to(jnp.exp(lg[:, None] * (N_META - 1.0 - im)[None, :])[:, :, None],
                             (RET_HEADS, N_META, 128))
    gsc = jnp.broadcast_to(jnp.exp(lg * RET_BLOCK)[:, None, None], (RET_HEADS, 1, 128))
    row = lambda b, s: (b * nb + s, 0)
    c2 = lambda b, s: (0, 0)
    c3 = lambda b, s: (0, 0, 0)
    return pl.pallas_call(
        _ret_body,
        grid=(batch, nb),
        in_specs=[pl.BlockSpec((RET_BLOCK, 512), row), pl.BlockSpec((RET_BLOCK, 512), row),
                  pl.BlockSpec((RET_BLOCK, 512), row), pl.BlockSpec((RET_BLOCK, 512), row),
                  pl.BlockSpec((N_META, 512), c2), pl.BlockSpec((N_META, 512), c2),
                  pl.BlockSpec((RET_HEADS, RET_BLOCK, RET_BLOCK), c3),
                  pl.BlockSpec((RET_HEADS, RET_BLOCK, 128), c3),
                  pl.BlockSpec((RET_HEADS, RET_BLOCK, 128), c3),
                  pl.BlockSpec((RET_HEADS, N_META, 128), c3),
                  pl.BlockSpec((RET_HEADS, 1, 128), c3)],
        out_specs=pl.BlockSpec((RET_BLOCK, 512), row),
        out_shape=jax.ShapeDtypeStruct((n, 512), BF16),
        scratch_shapes=[pltpu.VMEM((RET_HEADS, 128, 128), F32)],
        compiler_params=_params(("arbitrary", "arbitrary")),
        name="retention",
    )(rq, rk, rv, rg, mk, mv, dmat, xi, zeta, mzeta, gsc)


def _aligned(off):
    return off if isinstance(off, int) else pl.multiple_of(off, LANES)


def _dsa_body(qa_ref, qi_ref, misc_ref, k_ref, v_ref, ki_ref, wuv_ref, out_ref,
              isc_ref, bias_ref, m_ref, l_ref, acc_ref, *, topk):
    qb = pl.program_id(1)
    tq = DSA_BLOCK
    nh = DSA_HEADS

    wt = misc_ref[...].T
    qi = qi_ref[...].reshape(IDX_HEADS * tq, IDX_DIM)

    def isc_tile(koff, width, allowed):
        kt = ki_ref[pl.ds(koff, width), :]
        z = lax.dot_general(kt, qi, NT, preferred_element_type=F32)
        isc = None
        for h in range(IDX_HEADS):
            term = jnp.maximum(z[:, h * tq:(h + 1) * tq], 0.0) * wt[64 + h:65 + h, :]
            isc = term if isc is None else isc + term
        if allowed is not None:
            isc = jnp.where(allowed, isc, -jnp.inf)
        isc_ref[pl.ds(koff, width), :] = isc

    isc_tile(0, LANES, lax.broadcasted_iota(I32, (LANES, tq), 0) >= LANES - N_META)

    def body_a(t, c):
        isc_tile(pl.multiple_of(LANES + t * tq, LANES), tq, None)
        return c

    lax.fori_loop(0, qb, body_a, 0)
    keyc = lax.broadcasted_iota(I32, (tq, tq), 0) // CHUNK
    qryc = lax.broadcasted_iota(I32, (tq, tq), 1) // CHUNK
    diag_off = pl.multiple_of(LANES + qb * tq, LANES)
    isc_tile(diag_off, tq, keyc <= qryc)

    kf = float(topk)
    sub = 8

    def key_to_f32(key):
        return lax.bitcast_convert_type(jnp.where(key < 0, key ^ jnp.int32(0x7FFFFFFF), key), F32)

    def count(pred):
        def slab(off, a):
            hit = jnp.where(pred(isc_ref[pl.ds(off, LANES), :], off), 1.0, 0.0)
            return a + jnp.sum(hit.reshape(LANES // sub, sub, tq), axis=0)

        def body(t, a):
            off = pl.multiple_of(LANES + t * (2 * LANES), LANES)
            return slab(pl.multiple_of(off + LANES, LANES), slab(off, a))

        a = lax.fori_loop(0, qb + 1, body, slab(0, jnp.zeros((sub, tq), F32)))
        return jnp.sum(a, axis=0, keepdims=True)

    def count_ge(cand):
        return count(lambda x, off: x >= cand)

    lowest = jnp.full((1, tq), -3.0e38, F32)
    n_real = count_ge(lowest)
    c_pos = count_ge(jnp.zeros((1, tq), F32))
    pos = c_pos >= kf
    base0 = jnp.where(pos, jnp.int32(0), jnp.int32(INT_MIN))
    cnt0 = jnp.where(n_real <= kf, kf, jnp.where(pos, c_pos, n_real))

    def unfinished(cnt):
        return jnp.max(jnp.where(cnt != kf, 1.0, 0.0))

    def try_bit(bit, base, cnt):
        cand = base | lax.shift_left(jnp.int32(1), bit)
        c = count_ge(key_to_f32(cand))
        ok = c >= kf
        return jnp.where(ok, cand, base), jnp.where(ok, c, cnt)

    base, cnt = base0, cnt0
    for bit in range(30, DSA_BITS_PER_CHECK * 7 - 1, -1):
        base, cnt = try_bit(bit, base, cnt)

    def group_cond(state):
        g, _, cnt = state
        return jnp.logical_and(g >= 0, unfinished(cnt) > 0.0)

    def group_body(state):
        g, base, cnt = state
        for k in range(DSA_BITS_PER_CHECK - 1, -1, -1):
            base, cnt = try_bit(g * DSA_BITS_PER_CHECK + k, base, cnt)
        return g - 1, base, cnt

    _, base, cnt = lax.while_loop(group_cond, group_body, (jnp.int32(6), base, cnt))

    def refine_cond(state):
        it, _, _, _, go = state
        return jnp.logical_and(it < 40, go > 0.0)

    def refine_body(state):
        it, lo, hi, cnt, _ = state
        mid = lo + 0.5 * (hi - lo)
        c = count_ge(mid)
        ok = c >= kf
        cnt = jnp.where(ok, c, cnt)
        moving = jnp.logical_and(jnp.logical_and(mid > lo, mid < hi), cnt != kf)
        return it + 1, jnp.where(ok, mid, lo), jnp.where(ok, hi, mid), cnt, jnp.max(jnp.where(moving, 1.0, 0.0))

    _, thr, _, cnt, _ = lax.while_loop(
        refine_cond, refine_body,
        (jnp.int32(0), key_to_f32(base), key_to_f32(base + 1), cnt, unfinished(cnt)))
    thr = jnp.where(n_real <= kf, lowest, thr)

    key_id = lax.broadcasted_iota(I32, (LANES, tq), 0)
    index_bits = max(1, (isc_ref.shape[0] - 1).bit_length())

    def tie_bound():
        need = kf - count(lambda x, off: x > thr)

        def ties_below(bound):
            return count(lambda x, off: jnp.logical_and(x == thr, key_id < bound - off))

        def bit_step(i, last):
            cand = last | lax.shift_left(jnp.int32(1), index_bits - 1 - i)
            return jnp.where(ties_below(cand) < need, cand, last)

        return lax.fori_loop(0, index_bits, bit_step, jnp.zeros((1, tq), I32)) + 1

    everything = jnp.full((1, tq), isc_ref.shape[0], I32)
    tie_end = lax.cond(unfinished(cnt) > 0.0, tie_bound, lambda: everything)
    tie_end = jnp.where(cnt == kf, everything, tie_end)

    def mask_tile(off, width):
        x = isc_ref[pl.ds(off, width), :]
        ids = lax.broadcasted_iota(I32, (width, tq), 0)
        keep = jnp.logical_or(x > thr, jnp.logical_and(x == thr, ids < tie_end - off))
        bias_ref[:, pl.ds(off, width)] = jnp.where(keep, 0.0, NEG).T

    def body_m(t, c):
        mask_tile(pl.multiple_of(LANES + t * tq, LANES), tq)
        return c

    mask_tile(0, LANES)
    lax.fori_loop(0, qb + 1, body_m, 0)

    m_ref[...] = jnp.full(m_ref.shape, NEG, F32)
    l_ref[...] = jnp.zeros(l_ref.shape, F32)
    acc_ref[...] = jnp.zeros(acc_ref.shape, F32)
    qa = qa_ref[...].reshape(nh * tq, DSA_DIM)

    def att_tile(koff, width):
        kt = k_ref[pl.ds(koff, width), :]
        vt = v_ref[pl.ds(koff, width), :]
        s = lax.dot_general(qa, kt, NT, preferred_element_type=F32).reshape(nh, tq, width)
        reps = width // LANES
        s = (s + bias_ref[:, pl.ds(koff, width)][None]).reshape(nh * tq, width)
        m_prev = m_ref[...]
        m_new = jnp.maximum(m_prev, jnp.max(s, axis=1, keepdims=True))
        alpha = jnp.exp(m_prev - m_new)
        m_w = m_new if reps == 1 else jnp.concatenate([m_new] * reps, axis=1)
        p = jnp.exp(s - m_w)
        l_ref[...] = alpha * l_ref[...] + jnp.sum(p, axis=1, keepdims=True)
        acc_ref[...] = alpha * acc_ref[...] + jnp.dot(p.astype(BF16), vt, preferred_element_type=F32)
        m_ref[...] = m_new

    att_tile(0, LANES)

    def body_c(t, c):
        att_tile(pl.multiple_of(LANES + t * tq, LANES), tq)
        return c

    lax.fori_loop(0, qb + 1, body_c, 0)

    o = (acc_ref[...] / l_ref[...]).astype(BF16).reshape(nh, tq, DSA_DIM)
    for h in range(nh):
        out_ref[:, h * DSA_OUT_DIM:(h + 1) * DSA_OUT_DIM] = jnp.dot(
            o[h], wuv_ref[h], preferred_element_type=F32).astype(BF16)


def _dsa(dq, iq, misc, kpad, vpad, kipad, wuv, batch, seq, topk):
    n = misc.shape[0]
    nq = seq // DSA_BLOCK
    tp = kpad.shape[1]
    head = lambda b, q: (0, b * nq + q, 0)
    row = lambda b, q: (b * nq + q, 0)
    per_b = lambda b, q: (b, 0, 0)
    return pl.pallas_call(
        functools.partial(_dsa_body, topk=topk),
        grid=(batch, nq),
        in_specs=[pl.BlockSpec((DSA_HEADS, DSA_BLOCK, DSA_DIM), head),
                  pl.BlockSpec((IDX_HEADS, DSA_BLOCK, IDX_DIM), head),
                  pl.BlockSpec((DSA_BLOCK, 128), row),
                  pl.BlockSpec((None, tp, DSA_DIM), per_b),
                  pl.BlockSpec((None, tp, DSA_DIM), per_b),
                  pl.BlockSpec((None, tp, IDX_DIM), per_b),
                  pl.BlockSpec((DSA_HEADS, DSA_DIM, DSA_OUT_DIM), lambda b, q: (0, 0, 0))],
        out_specs=pl.BlockSpec((DSA_BLOCK, 512), row),
        out_shape=jax.ShapeDtypeStruct((n, 512), BF16),
        scratch_shapes=[pltpu.VMEM((tp, DSA_BLOCK), F32),
                        pltpu.VMEM((DSA_BLOCK, tp), F32),
                        pltpu.VMEM((DSA_HEADS * DSA_BLOCK, LANES), F32),
                        pltpu.VMEM((DSA_HEADS * DSA_BLOCK, LANES), F32),
                        pltpu.VMEM((DSA_HEADS * DSA_BLOCK, DSA_DIM), F32)],
        compiler_params=_params(("arbitrary", "arbitrary")),
        name="dsa",
    )(dq, iq, misc, kpad, vpad, kipad, wuv)


def _post_body(ret_ref, att_ref, x_ref, w_ref, g_ref, h1_ref, hnt_ref):
    y = jnp.dot(ret_ref[...], w_ref[0:512, :], preferred_element_type=F32)
    y = y + jnp.dot(att_ref[...], w_ref[512:1024, :], preferred_element_type=F32)
    h1 = x_ref[...] + y
    h1_ref[...] = h1
    hnt_ref[...] = _rms(h1, g_ref[...]).T.astype(BF16)


def _post(ret, att, x2d, w_out, g, tm):
    n = x2d.shape[0]
    row = lambda i: (i, 0)
    return pl.pallas_call(
        _post_body,
        grid=(n // tm,),
        in_specs=[pl.BlockSpec((tm, 512), row), pl.BlockSpec((tm, 512), row),
                  pl.BlockSpec((tm, D_MODEL), row),
                  pl.BlockSpec((D_MODEL, D_MODEL), lambda i: (0, 0)),
                  pl.BlockSpec((1, D_MODEL), lambda i: (0, 0))],
        out_specs=(pl.BlockSpec((tm, D_MODEL), row), pl.BlockSpec((D_MODEL, tm), lambda i: (0, i))),
        out_shape=(jax.ShapeDtypeStruct((n, D_MODEL), F32), jax.ShapeDtypeStruct((D_MODEL, n), BF16)),
        compiler_params=_params(("arbitrary",)),
        name="post",
    )(ret, att, x2d, w_out, g)


def _top_desc(s, k, dst_ref, with_rank=False):
    cur = s
    rank = jnp.full(s.shape, float(k), F32) if with_rank else None
    for r in range(k):
        m = jnp.max(cur, axis=0, keepdims=True)
        dst_ref[r:r + 1, :] = m
        if with_rank or r + 1 < k:
            hit = cur == m
            if with_rank:
                rank = jnp.where(hit, float(r), rank)
            cur = jnp.where(hit, -jnp.inf, cur)
    return rank


def _psel_body(hnt_ref, wq_ref, sk_ref, cnt_ref, e1_ref, r2_ref, e2_ref, v1_ref, v2_ref, vc_ref):
    hnt = hnt_ref[...]
    for h in range(PEER_HEADS):
        q = jnp.dot(wq_ref[h * 256:(h + 1) * 256, :], hnt, preferred_element_type=F32)
        s1 = jnp.dot(sk_ref[h, 0], q[:128].astype(BF16), preferred_element_type=F32)
        s2 = jnp.dot(sk_ref[h, 1], q[128:].astype(BF16), preferred_element_type=F32)
        _top_desc(s1, PEER_TOPK, v1_ref)
        rank2 = _top_desc(s2, PEER_TOPK, v2_ref, with_rank=True)
        v1 = v1_ref[...]
        v2 = v2_ref[...]
        cand = jnp.concatenate([v2 + v1[0:1]] + [v2[0:8] + v1[a:a + 1] for a in range(1, 8)]
                               + [v1[8:16] + v2[0:1]], axis=0)
        _top_desc(cand, PEER_TOPK, vc_ref)
        tau = vc_ref[PEER_TOPK - 1:PEER_TOPK, :]
        top = vc_ref[0:1, :]
        zsum = jnp.sum(jnp.where(cand >= tau, jnp.exp(cand - top), 0.0), axis=0, keepdims=True)
        cnt = jnp.zeros(s1.shape, F32)
        for b in range(PEER_TOPK):
            cnt = cnt + jnp.where(s1 + v2[b:b + 1] >= tau, 1.0, 0.0)
        cnt_ref[h] = cnt
        e1_ref[h] = jnp.exp(s1 - v1[0:1]) / zsum
        r2_ref[h] = pltpu.bitcast(rank2.astype(BF16), I32)
        e2_ref[h] = pltpu.bitcast(jnp.exp(s2 - v2[0:1]).astype(BF16), I32)


def _psel(hnt, wq_t, sk, tn):
    n = hnt.shape[1]
    spec = pl.BlockSpec((PEER_HEADS, PEER_NKEYS, tn), lambda i: (0, 0, i))
    shp32 = jax.ShapeDtypeStruct((PEER_HEADS, PEER_NKEYS, n), F32)
    shp16 = jax.ShapeDtypeStruct((PEER_HEADS, PEER_NKEYS // 2, n), I32)
    spec16 = pl.BlockSpec((PEER_HEADS, PEER_NKEYS // 2, tn), lambda i: (0, 0, i))
    return pl.pallas_call(
        _psel_body,
        grid=(n // tn,),
        in_specs=[pl.BlockSpec((D_MODEL, tn), lambda i: (0, i)),
                  pl.BlockSpec((PEER_HEADS * 256, D_MODEL), lambda i: (0, 0)),
                  pl.BlockSpec((PEER_HEADS, 2, PEER_NKEYS, 128), lambda i: (0, 0, 0, 0))],
        out_specs=(spec, spec, spec16, spec16),
        out_shape=(shp32, shp32, shp16, shp16),
        scratch_shapes=[pltpu.VMEM((PEER_TOPK, tn), F32)] * 3,
        compiler_params=_params(("arbitrary",)),
        name="psel",
    )(hnt, wq_t, sk)


def _pdense_body(*refs, eb, nchunk, nblk):
    hnt_ref, u_ref, vt_ref = refs[:3]
    cnt_refs = refs[3:3 + nchunk]
    e1_refs = refs[3 + nchunk:3 + 2 * nchunk]
    r2_ref, e2_ref, h1_ref, g_ref, out_ref, acc_ref, a0_ref, a1_ref, c0_ref, c1_ref = refs[3 + 2 * nchunk:]
    a_refs = (a0_ref, a1_ref)
    c_refs = (c0_ref, c1_ref)
    s = pl.program_id(1)
    nsub = eb // PEER_NKEYS

    def row_bf16(ref, h, i1):
        return jnp.broadcast_to(ref[h, pl.ds(i1, 1), :], (PEER_NKEYS, LANES)).astype(BF16)

    def stage_a(a_ref):
        for half in range(eb // PDENSE_ROWS):
            rows = slice(half * PDENSE_ROWS, (half + 1) * PDENSE_ROWS)
            words = slice(half * PDENSE_ROWS // 2, (half + 1) * PDENSE_ROWS // 2)
            a_ref[rows, :] = jnp.dot(pltpu.bitcast(u_ref[words, :], BF16), hnt_ref[...],
                                     preferred_element_type=F32)

    def stage_b(blk, a_ref, coef_ref):
        for cp in range(0, nsub, PDENSE_GROUP):
            cs = [cp + k for k in range(PDENSE_GROUP)]
            for tc in range(nchunk):
                ts = slice(tc * LANES, (tc + 1) * LANES)
                gates = [None] * PDENSE_GROUP
                for h in range(PEER_HEADS):
                    r2 = pltpu.bitcast(r2_ref[h, :, ts], BF16)
                    e2 = pltpu.bitcast(e2_ref[h, :, ts], BF16)
                    for k in range(PDENSE_GROUP):
                        cnt = row_bf16(cnt_refs[tc], h, blk * nsub + cs[k])
                        e1 = row_bf16(e1_refs[tc], h, blk * nsub + cs[k])
                        term = jnp.where(r2 < cnt, e2 * e1, jnp.zeros((), BF16))
                        gates[k] = term if gates[k] is None else gates[k] + term
                for k in range(PDENSE_GROUP):
                    rows = slice(cs[k] * PEER_NKEYS, (cs[k] + 1) * PEER_NKEYS)
                    at = a_ref[rows, ts]
                    gelu = (0.5 * at) * (1.0 + lax.erf(at * (0.5 ** 0.5)))
                    coef_ref[rows, ts] = gelu.astype(BF16) * gates[k]

    def stage_c(coef_ref):
        acc_ref[...] += jnp.dot(vt_ref[...], coef_ref[...], preferred_element_type=F32)

    steady = jnp.logical_and(s >= 2, s < nblk)
    for par in range(2):
        @pl.when(jnp.logical_and(steady, s % 2 == par))
        def _(par=par):
            stage_a(a_refs[par])
            stage_b(s - 1, a_refs[1 - par], c_refs[1 - par])
            stage_c(c_refs[par])

    @pl.when(s == 0)
    def _():
        acc_ref[...] = jnp.zeros(acc_ref.shape, F32)
        stage_a(a_refs[0])

    @pl.when(s == 1)
    def _():
        stage_a(a_refs[1])
        stage_b(0, a_refs[0], c_refs[0])

    @pl.when(s == nblk)
    def _():
        stage_b(nblk - 1, a_refs[(nblk - 1) % 2], c_refs[(nblk - 1) % 2])
        stage_c(c_refs[nblk % 2])

    @pl.when(s == nblk + 1)
    def _():
        stage_c(c_refs[(nblk - 1) % 2])
        out_ref[...] = _rms(h1_ref[...] + acc_ref[...].T, g_ref[...])


def _pdense(hnt, u, vt, cnt, e1, r2, e2, h1, g, tn, eb):
    n = hnt.shape[1]
    nblk = 2 * u.shape[0] // eb
    assert nblk >= 2
    nchunk = tn // LANES
    chunk = [pl.BlockSpec((PEER_HEADS, PEER_NKEYS, LANES), lambda i, s, c=c: (0, 0, i * nchunk + c))
             for c in range(nchunk)]
    sel16 = pl.BlockSpec((PEER_HEADS, PEER_NKEYS // 2, tn), lambda i, s: (0, 0, i))
    return pl.pallas_call(
        functools.partial(_pdense_body, eb=eb, nchunk=nchunk, nblk=nblk),
        grid=(n // tn, nblk + 2),
        in_specs=[pl.BlockSpec((D_MODEL, tn), lambda i, s: (0, i)),
                  pl.BlockSpec((eb // 2, D_MODEL), lambda i, s: (jnp.minimum(s, nblk - 1), 0)),
                  pl.BlockSpec((D_MODEL, eb), lambda i, s: (0, jnp.clip(s - 2, 0, nblk - 1)))]
                 + chunk + chunk
                 + [sel16, sel16,
                    pl.BlockSpec((tn, D_MODEL), lambda i, s: (i, 0)),
                    pl.BlockSpec((1, D_MODEL), lambda i, s: (0, 0))],
        out_specs=pl.BlockSpec((tn, D_MODEL), lambda i, s: (i, 0)),
        out_shape=jax.ShapeDtypeStruct((n, D_MODEL), F32),
        scratch_shapes=[pltpu.VMEM((D_MODEL, tn), F32),
                        pltpu.VMEM((eb, tn), F32), pltpu.VMEM((eb, tn), F32),
                        pltpu.VMEM((eb, tn), BF16), pltpu.VMEM((eb, tn), BF16)],
        compiler_params=_params(("arbitrary", "arbitrary")),
        name="pdense",
    )(hnt, u, vt, *([cnt] * nchunk), *([e1] * nchunk), r2, e2, h1, g)


def _pack_row_pairs(w):
    r2, c = w.shape
    return lax.bitcast_convert_type(w.reshape(r2 // 2, 2, c).transpose(0, 2, 1), I32)


def kernel(x, meta_tokens, norm_mix, w_in, w_uv, w_out, norm_ffn, peer_wq, peer_subkeys,
           peer_u, peer_v, norm_final):
    batch, seq, d = x.shape
    assert d == D_MODEL and norm_mix.shape[0] == 1 and seq % DSA_BLOCK == 0
    n = batch * seq
    topk = min(DSA_TOPK_MAX, seq // 4)
    tm = 512 if seq % 512 == 0 else 256

    w = jnp.pad(w_in[0], ((0, 0), (0, IN_WIDTH_PAD - IN_WIDTH))).astype(BF16)
    g_mix = norm_mix[0][None, :]
    x2d = x.reshape(n, d)
    pos_f = jnp.arange(seq, dtype=F32) + float(N_META)
    pos_m = jnp.arange(N_META, dtype=F32)

    rq, rk, rv, rg, dq, dk, dv, iq, misc = _proj(x2d, g_mix, w, _rope_tables(pos_f), tm, seq // tm)
    _, mk, mv, _, _, mdk, mdv, _, mmisc = _proj(meta_tokens, g_mix, w, _rope_tables(pos_m), N_META, 1)

    ret = _retention(rq, rk, rv, rg, mk, mv, batch, seq)

    def keys(frames, meta):
        width = frames.shape[-1]
        lead = jnp.concatenate([jnp.zeros((LANES - N_META, width), frames.dtype), meta], axis=0)
        lead = jnp.broadcast_to(lead[None], (batch, LANES, width))
        return jnp.concatenate([lead, frames.reshape(batch, seq, width)], axis=1)

    kpad = keys(dk, mdk)
    vpad = keys(dv, mdv)
    kipad = keys(misc[:, :IDX_DIM].astype(BF16), mmisc[:, :IDX_DIM].astype(BF16))
    att = _dsa(dq, iq, misc, kpad, vpad, kipad, w_uv[0].astype(BF16), batch, seq, topk)

    h1, hnt = _post(ret, att, x2d, w_out[0].astype(BF16), norm_ffn[0][None, :], tm)

    wq_t = peer_wq[0].reshape(d, PEER_HEADS * 256).T.astype(BF16)
    cnt, e1, r2, e2 = _psel(hnt, wq_t, peer_subkeys[0].astype(BF16), tm)
    out = _pdense(hnt, _pack_row_pairs(peer_u[0].astype(BF16)), peer_v[0].T.astype(BF16), cnt, e1, r2, e2, h1,
                  norm_final[None, :], tm, PDENSE_EXPERTS)
    return out.reshape(batch, seq, d)
```

```python
import functools

import jax
import jax.numpy as jnp
from jax import lax
from jax.experimental import pallas as pl
from jax.experimental.pallas import tpu as pltpu

F32 = jnp.float32
BF16 = jnp.bfloat16
I32 = jnp.int32

D_MODEL = 1024
N_META = 16
CHUNK = 64
RET_HEADS = 4
RET_DIM = 128
DSA_HEADS = 8
DSA_DIM = 128
DSA_OUT_DIM = 64
IDX_HEADS = 8
IDX_DIM = 64
DSA_TOPK_MAX = 256
ROPE_THETA = 10000.0
PEER_HEADS = 8
PEER_NKEYS = 128
PEER_TOPK = 16
EPS = 1e-6
NEG = -1e30
INT_MIN = -(2 ** 31)

IN_WIDTH = 3912
IN_WIDTH_PAD = 3968
LANES = 128
RET_BLOCK = 256
PDENSE_EXPERTS = 1024
PDENSE_ROWS = 512
PDENSE_GROUP = 2
DSA_BLOCK = 256
DSA_BITS_PER_CHECK = 4
VMEM_LIMIT = 56 * 1024 * 1024

NT = (((1,), (1,)), ((), ()))
TN = (((0,), (0,)), ((), ()))


def _rms(x, g):
    return x * lax.rsqrt(jnp.mean(x * x, axis=-1, keepdims=True) + EPS) * g


def _params(sem):
    return pltpu.CompilerParams(dimension_semantics=sem, vmem_limit_bytes=VMEM_LIMIT)


def _proj_body(x_ref, g_ref, w_ref, tab_ref, rq_ref, rk_ref, rv_ref, rg_ref, dq_ref, dk_ref,
               dv_ref, iq_ref, misc_ref):
    xn = _rms(x_ref[...], g_ref[...]).astype(BF16)
    cos_a, sin_a, cos_b, sin_b1, sin_b2, cos_m, sin_m1, sin_m2 = (tab_ref[i] for i in range(8))

    def mm(lo, n):
        return jnp.dot(xn, w_ref[:, lo:lo + n], preferred_element_type=F32)

    def rope_a(y):
        return y * cos_a + pltpu.roll(y, 64, 1) * sin_a

    def rope_b(y, c, s1, s2):
        return y * c + pltpu.roll(y, 96, 1) * s1 + pltpu.roll(y, 32, 1) * s2

    y = mm(0, 512)
    for h in range(RET_HEADS):
        rq_ref[:, h * 128:(h + 1) * 128] = rope_a(y[:, h * 128:(h + 1) * 128]).astype(BF16)
    y = mm(512, 512)
    for h in range(RET_HEADS):
        rk_ref[:, h * 128:(h + 1) * 128] = (
            rope_a(y[:, h * 128:(h + 1) * 128]) * (RET_DIM ** -0.5)).astype(BF16)
    rv_ref[...] = mm(1024, 512).astype(BF16)
    rg_ref[...] = mm(1536, 512)
    y = mm(2048, 1024)
    for h in range(DSA_HEADS):
        dq_ref[h] = (rope_a(y[:, h * 128:(h + 1) * 128]) * (DSA_DIM ** -0.5)).astype(BF16)
    dk_ref[...] = rope_a(mm(3072, 128)).astype(BF16)
    dv_ref[...] = mm(3200, 128).astype(BF16)
    y = mm(3328, 512)
    for p in range(IDX_HEADS // 2):
        o = (rope_b(y[:, p * 128:(p + 1) * 128], cos_b, sin_b1, sin_b2) * (IDX_DIM ** -0.5)).astype(BF16)
        iq_ref[2 * p] = o[:, :64]
        iq_ref[2 * p + 1] = o[:, 64:]
    misc_ref[...] = rope_b(mm(3840, 128), cos_m, sin_m1, sin_m2)


def _proj(x2d, g, w, tab, tm, tab_blocks):
    n = x2d.shape[0]
    row = lambda i: (i, 0)
    head = lambda i: (0, i, 0)
    out_shape = (
        jax.ShapeDtypeStruct((n, 512), BF16), jax.ShapeDtypeStruct((n, 512), BF16),
        jax.ShapeDtypeStruct((n, 512), BF16), jax.ShapeDtypeStruct((n, 512), F32),
        jax.ShapeDtypeStruct((DSA_HEADS, n, 128), BF16), jax.ShapeDtypeStruct((n, 128), BF16),
        jax.ShapeDtypeStruct((n, 128), BF16), jax.ShapeDtypeStruct((IDX_HEADS, n, 64), BF16),
        jax.ShapeDtypeStruct((n, 128), F32))
    out_specs = (
        pl.BlockSpec((tm, 512), row), pl.BlockSpec((tm, 512), row), pl.BlockSpec((tm, 512), row),
        pl.BlockSpec((tm, 512), row), pl.BlockSpec((DSA_HEADS, tm, 128), head),
        pl.BlockSpec((tm, 128), row), pl.BlockSpec((tm, 128), row),
        pl.BlockSpec((IDX_HEADS, tm, 64), head), pl.BlockSpec((tm, 128), row))
    return pl.pallas_call(
        _proj_body,
        grid=(n // tm,),
        in_specs=[pl.BlockSpec((tm, D_MODEL), row),
                  pl.BlockSpec((1, D_MODEL), lambda i: (0, 0)),
                  pl.BlockSpec((D_MODEL, IN_WIDTH_PAD), lambda i: (0, 0)),
                  pl.BlockSpec((8, tm, 128), lambda i: (0, i % tab_blocks, 0))],
        out_specs=out_specs,
        out_shape=out_shape,
        compiler_params=_params(("arbitrary",)),
        name="proj",
    )(x2d, g, w, tab)


def _rope_tables(pos):
    p = pos.shape[0]
    inv = ROPE_THETA ** (-jnp.arange(0, 64, dtype=F32) * 2.0 / 128)
    ang = pos[:, None] * inv[None, :]
    c, s = jnp.cos(ang), jnp.sin(ang)
    cos_a = jnp.concatenate([c, c], -1)
    sin_a = jnp.concatenate([-s, s], -1)
    inv = ROPE_THETA ** (-jnp.arange(0, 32, dtype=F32) * 2.0 / 64)
    ang = pos[:, None] * inv[None, :]
    c, s = jnp.cos(ang), jnp.sin(ang)
    z = jnp.zeros_like(s)
    z64 = jnp.zeros((p, 64), F32)
    cos_b = jnp.concatenate([c, c, c, c], -1)
    sin_b1 = jnp.concatenate([-s, z, -s, z], -1)
    sin_b2 = jnp.concatenate([z, s, z, s], -1)
    cos_m = jnp.concatenate([c, c, jnp.full((p, 64), IDX_HEADS ** -0.5, F32)], -1)
    sin_m1 = jnp.concatenate([-s, z, z64], -1)
    sin_m2 = jnp.concatenate([z, s, z64], -1)
    return jnp.stack([cos_a, sin_a, cos_b, sin_b1, sin_b2, cos_m, sin_m1, sin_m2])


def _ret_body(rq_ref, rk_ref, rv_ref, rg_ref, mk_ref, mv_ref, dmat_ref, xi_ref, zeta_ref,
              mzeta_ref, gsc_ref, out_ref, r_ref):
    @pl.when(pl.program_id(1) == 0)
    def _():
        for h in range(RET_HEADS):
            hs = slice(h * 128, (h + 1) * 128)
            kz = (mk_ref[:, hs].astype(F32) * mzeta_ref[h]).astype(BF16)
            r_ref[h] = lax.dot_general(kz, mv_ref[:, hs], TN, preferred_element_type=F32)

    for h in range(RET_HEADS):
        hs = slice(h * 128, (h + 1) * 128)
        q = rq_ref[:, hs]
        k = rk_ref[:, hs]
        v = rv_ref[:, hs]
        s = lax.dot_general(q, k, NT, preferred_element_type=F32) * dmat_ref[h]
        o = jnp.dot(s.astype(BF16), v, preferred_element_type=F32)
        r = r_ref[h]
        qx = (q.astype(F32) * xi_ref[h]).astype(BF16)
        o = o + jnp.dot(qx, r.astype(BF16), preferred_element_type=F32)
        kz = (k.astype(F32) * zeta_ref[h]).astype(BF16)
        u = lax.dot_general(kz, v, TN, preferred_element_type=F32)
        r_ref[h] = r * gsc_ref[h] + u
        mu = jnp.mean(o, axis=-1, keepdims=True)
        d = o - mu
        var = jnp.mean(d * d, axis=-1, keepdims=True)
        on = d * lax.rsqrt(var + EPS)
        g = rg_ref[:, hs]
        out_ref[:, hs] = (g * jax.nn.sigmoid(g) * on).astype(BF16)


def _retention(rq, rk, rv, rg, mk, mv, batch, seq):
    n = rq.shape[0]
    nb = seq // RET_BLOCK
    lg = jnp.log(1.0 - 2.0 ** (-5.0 - jnp.arange(RET_HEADS, dtype=F32)))
    i = jnp.arange(RET_BLOCK, dtype=F32)
    ci = jnp.arange(RET_BLOCK) // CHUNK
    vis = (ci[None, :] <= ci[:, None])
    dmat = jnp.where(vis[None], jnp.exp(lg[:, None, None] * jnp.abs(i[:, None] - i[None, :])), 0.0)
    xi = jnp.broadcast_to(jnp.exp(lg[:, None] * (i + 1.0)[None, :])[:, :, None], (RET_HEADS, RET_BLOCK, 128))
    zeta = jnp.broadcast_to(jnp.exp(lg[:, None] * (RET_BLOCK - 1.0 - i)[None, :])[:, :, None],
                            (RET_HEADS, RET_BLOCK, 128))
    im = jnp.arange(N_META, dtype=F32)
    mzeta = jnp.broadcast_to(jnp.exp(lg[:, None] * (N_META - 1.0 - im)[None, :])[:, :, None],
                             (RET_HEADS, N_META, 128))
    gsc = jnp.broadcast_to(jnp.exp(lg * RET_BLOCK)[:, None, None], (RET_HEADS, 1, 128))
    row = lambda b, s: (b * nb + s, 0)
    c2 = lambda b, s: (0, 0)
    c3 = lambda b, s: (0, 0, 0)
    return pl.pallas_call(
        _ret_body,
        grid=(batch, nb),
        in_specs=[pl.BlockSpec((RET_BLOCK, 512), row), pl.BlockSpec((RET_BLOCK, 512), row),
                  pl.BlockSpec((RET_BLOCK, 512), row), pl.BlockSpec((RET_BLOCK, 512), row),
                  pl.BlockSpec((N_META, 512), c2), pl.BlockSpec((N_META, 512), c2),
                  pl.BlockSpec((RET_HEADS, RET_BLOCK, RET_BLOCK), c3),
                  pl.BlockSpec((RET_HEADS, RET_BLOCK, 128), c3),
                  pl.BlockSpec((RET_HEADS, RET_BLOCK, 128), c3),
                  pl.BlockSpec((RET_HEADS, N_META, 128), c3),
                  pl.BlockSpec((RET_HEADS, 1, 128), c3)],
        out_specs=pl.BlockSpec((RET_BLOCK, 512), row),
        out_shape=jax.ShapeDtypeStruct((n, 512), BF16),
        scratch_shapes=[pltpu.VMEM((RET_HEADS, 128, 128), F32)],
        compiler_params=_params(("arbitrary", "arbitrary")),
        name="retention",
    )(rq, rk, rv, rg, mk, mv, dmat, xi, zeta, mzeta, gsc)


def _aligned(off):
    return off if isinstance(off, int) else pl.multiple_of(off, LANES)


def _dsa_body(qa_ref, qi_ref, misc_ref, k_ref, v_ref, ki_ref, wuv_ref, out_ref,
              isc_ref, bias_ref, m_ref, l_ref, acc_ref, *, topk):
    qb = pl.program_id(1)
    tq = DSA_BLOCK
    nh = DSA_HEADS

    wt = misc_ref[...].T
    qi = qi_ref[...].reshape(IDX_HEADS * tq, IDX_DIM)

    def isc_tile(koff, width, allowed):
        kt = ki_ref[pl.ds(koff, width), :]
        z = lax.dot_general(kt, qi, NT, preferred_element_type=F32)
        isc = None
        for h in range(IDX_HEADS):
            term = jnp.maximum(z[:, h * tq:(h + 1) * tq], 0.0) * wt[64 + h:65 + h, :]
            isc = term if isc is None else isc + term
        if allowed is not None:
            isc = jnp.where(allowed, isc, -jnp.inf)
        isc_ref[pl.ds(koff, width), :] = isc

    isc_tile(0, LANES, lax.broadcasted_iota(I32, (LANES, tq), 0) >= LANES - N_META)

    def body_a(t, c):
        isc_tile(pl.multiple_of(LANES + t * tq, LANES), tq, None)
        return c

    lax.fori_loop(0, qb, body_a, 0)
    keyc = lax.broadcasted_iota(I32, (tq, tq), 0) // CHUNK
    qryc = lax.broadcasted_iota(I32, (tq, tq), 1) // CHUNK
    diag_off = pl.multiple_of(LANES + qb * tq, LANES)
    isc_tile(diag_off, tq, keyc <= qryc)

    kf = float(topk)
    sub = 8

    def key_to_f32(key):
        return lax.bitcast_convert_type(jnp.where(key < 0, key ^ jnp.int32(0x7FFFFFFF), key), F32)

    def count(pred):
        def slab(off, a):
            hit = jnp.where(pred(isc_ref[pl.ds(off, LANES), :], off), 1.0, 0.0)
            return a + jnp.sum(hit.reshape(LANES // sub, sub, tq), axis=0)

        def body(t, a):
            off = pl.multiple_of(LANES + t * (2 * LANES), LANES)
            return slab(pl.multiple_of(off + LANES, LANES), slab(off, a))

        a = lax.fori_loop(0, qb + 1, body, slab(0, jnp.zeros((sub, tq), F32)))
        return jnp.sum(a, axis=0, keepdims=True)

    def count_ge(cand):
        return count(lambda x, off: x >= cand)

    lowest = jnp.full((1, tq), -3.0e38, F32)
    n_real = count_ge(lowest)
    c_pos = count_ge(jnp.zeros((1, tq), F32))
    pos = c_pos >= kf
    base0 = jnp.where(pos, jnp.int32(0), jnp.int32(INT_MIN))
    cnt0 = jnp.where(n_real <= kf, kf, jnp.where(pos, c_pos, n_real))

    def unfinished(cnt):
        return jnp.max(jnp.where(cnt != kf, 1.0, 0.0))

    def try_bit(bit, base, cnt):
        cand = base | lax.shift_left(jnp.int32(1), bit)
        c = count_ge(key_to_f32(cand))
        ok = c >= kf
        return jnp.where(ok, cand, base), jnp.where(ok, c, cnt)

    base, cnt = base0, cnt0
    for bit in range(30, DSA_BITS_PER_CHECK * 7 - 1, -1):
        base, cnt = try_bit(bit, base, cnt)

    def group_cond(state):
        g, _, cnt = state
        return jnp.logical_and(g >= 0, unfinished(cnt) > 0.0)

    def group_body(state):
        g, base, cnt = state
        for k in range(DSA_BITS_PER_CHECK - 1, -1, -1):
            base, cnt = try_bit(g * DSA_BITS_PER_CHECK + k, base, cnt)
        return g - 1, base, cnt

    _, base, cnt = lax.while_loop(group_cond, group_body, (jnp.int32(6), base, cnt))

    def refine_cond(state):
        it, _, _, _, go = state
        return jnp.logical_and(it < 40, go > 0.0)

    def refine_body(state):
        it, lo, hi, cnt, _ = state
        mid = lo + 0.5 * (hi - lo)
        c = count_ge(mid)
        ok = c >= kf
        cnt = jnp.where(ok, c, cnt)
        moving = jnp.logical_and(jnp.logical_and(mid > lo, mid < hi), cnt != kf)
        return it + 1, jnp.where(ok, mid, lo), jnp.where(ok, hi, mid), cnt, jnp.max(jnp.where(moving, 1.0, 0.0))

    _, thr, _, cnt, _ = lax.while_loop(
        refine_cond, refine_body,
        (jnp.int32(0), key_to_f32(base), key_to_f32(base + 1), cnt, unfinished(cnt)))
    thr = jnp.where(n_real <= kf, lowest, thr)

    key_id = lax.broadcasted_iota(I32, (LANES, tq), 0)
    index_bits = max(1, (isc_ref.shape[0] - 1).bit_length())

    def tie_bound():
        need = kf - count(lambda x, off: x > thr)

        def ties_below(bound):
            return count(lambda x, off: jnp.logical_and(x == thr, key_id < bound - off))

        def bit_step(i, last):
            cand = last | lax.shift_left(jnp.int32(1), index_bits - 1 - i)
            return jnp.where(ties_below(cand) < need, cand, last)

        return lax.fori_loop(0, index_bits, bit_step, jnp.zeros((1, tq), I32)) + 1

    everything = jnp.full((1, tq), isc_ref.shape[0], I32)
    tie_end = lax.cond(unfinished(cnt) > 0.0, tie_bound, lambda: everything)
    tie_end = jnp.where(cnt == kf, everything, tie_end)

    def mask_tile(off, width):
        x = isc_ref[pl.ds(off, width), :]
        ids = lax.broadcasted_iota(I32, (width, tq), 0)
        keep = jnp.logical_or(x > thr, jnp.logical_and(x == thr, ids < tie_end - off))
        bias_ref[:, pl.ds(off, width)] = jnp.where(keep, 0.0, NEG).T

    def body_m(t, c):
        mask_tile(pl.multiple_of(LANES + t * tq, LANES), tq)
        return c

    mask_tile(0, LANES)
    lax.fori_loop(0, qb + 1, body_m, 0)

    m_ref[...] = jnp.full(m_ref.shape, NEG, F32)
    l_ref[...] = jnp.zeros(l_ref.shape, F32)
    acc_ref[...] = jnp.zeros(acc_ref.shape, F32)
    qa = qa_ref[...].reshape(nh * tq, DSA_DIM)

    def att_tile(koff, width):
        kt = k_ref[pl.ds(koff, width), :]
        vt = v_ref[pl.ds(koff, width), :]
        s = lax.dot_general(qa, kt, NT, preferred_element_type=F32).reshape(nh, tq, width)
        reps = width // LANES
        s = (s + bias_ref[:, pl.ds(koff, width)][None]).reshape(nh * tq, width)
        m_prev = m_ref[...]
        m_new = jnp.maximum(m_prev, jnp.max(s, axis=1, keepdims=True))
        alpha = jnp.exp(m_prev - m_new)
        m_w = m_new if reps == 1 else jnp.concatenate([m_new] * reps, axis=1)
        p = jnp.exp(s - m_w)
        l_ref[...] = alpha * l_ref[...] + jnp.sum(p, axis=1, keepdims=True)
        acc_ref[...] = alpha * acc_ref[...] + jnp.dot(p.astype(BF16), vt, preferred_element_type=F32)
        m_ref[...] = m_new

    att_tile(0, LANES)

    def body_c(t, c):
        att_tile(pl.multiple_of(LANES + t * tq, LANES), tq)
        return c

    lax.fori_loop(0, qb + 1, body_c, 0)

    o = (acc_ref[...] / l_ref[...]).astype(BF16).reshape(nh, tq, DSA_DIM)
    for h in range(nh):
        out_ref[:, h * DSA_OUT_DIM:(h + 1) * DSA_OUT_DIM] = jnp.dot(
            o[h], wuv_ref[h], preferred_element_type=F32).astype(BF16)


def _dsa(dq, iq, misc, kpad, vpad, kipad, wuv, batch, seq, topk):
    n = misc.shape[0]
    nq = seq // DSA_BLOCK
    tp = kpad.shape[1]
    head = lambda b, q: (0, b * nq + q, 0)
    row = lambda b, q: (b * nq + q, 0)
    per_b = lambda b, q: (b, 0, 0)
    return pl.pallas_call(
        functools.partial(_dsa_body, topk=topk),
        grid=(batch, nq),
        in_specs=[pl.BlockSpec((DSA_HEADS, DSA_BLOCK, DSA_DIM), head),
                  pl.BlockSpec((IDX_HEADS, DSA_BLOCK, IDX_DIM), head),
                  pl.BlockSpec((DSA_BLOCK, 128), row),
                  pl.BlockSpec((None, tp, DSA_DIM), per_b),
                  pl.BlockSpec((None, tp, DSA_DIM), per_b),
                  pl.BlockSpec((None, tp, IDX_DIM), per_b),
                  pl.BlockSpec((DSA_HEADS, DSA_DIM, DSA_OUT_DIM), lambda b, q: (0, 0, 0))],
        out_specs=pl.BlockSpec((DSA_BLOCK, 512), row),
        out_shape=jax.ShapeDtypeStruct((n, 512), BF16),
        scratch_shapes=[pltpu.VMEM((tp, DSA_BLOCK), F32),
                        pltpu.VMEM((DSA_BLOCK, tp), F32),
                        pltpu.VMEM((DSA_HEADS * DSA_BLOCK, LANES), F32),
                        pltpu.VMEM((DSA_HEADS * DSA_BLOCK, LANES), F32),
                        pltpu.VMEM((DSA_HEADS * DSA_BLOCK, DSA_DIM), F32)],
        compiler_params=_params(("arbitrary", "arbitrary")),
        name="dsa",
    )(dq, iq, misc, kpad, vpad, kipad, wuv)


def _post_body(ret_ref, att_ref, x_ref, w_ref, g_ref, h1_ref, hnt_ref):
    y = jnp.dot(ret_ref[...], w_ref[0:512, :], preferred_element_type=F32)
    y = y + jnp.dot(att_ref[...], w_ref[512:1024, :], preferred_element_type=F32)
    h1 = x_ref[...] + y
    h1_ref[...] = h1
    hnt_ref[...] = _rms(h1, g_ref[...]).T.astype(BF16)


def _post(ret, att, x2d, w_out, g, tm):
    n = x2d.shape[0]
    row = lambda i: (i, 0)
    return pl.pallas_call(
        _post_body,
        grid=(n // tm,),
        in_specs=[pl.BlockSpec((tm, 512), row), pl.BlockSpec((tm, 512), row),
                  pl.BlockSpec((tm, D_MODEL), row),
                  pl.BlockSpec((D_MODEL, D_MODEL), lambda i: (0, 0)),
                  pl.BlockSpec((1, D_MODEL), lambda i: (0, 0))],
        out_specs=(pl.BlockSpec((tm, D_MODEL), row), pl.BlockSpec((D_MODEL, tm), lambda i: (0, i))),
        out_shape=(jax.ShapeDtypeStruct((n, D_MODEL), F32), jax.ShapeDtypeStruct((D_MODEL, n), BF16)),
        compiler_params=_params(("arbitrary",)),
        name="post",
    )(ret, att, x2d, w_out, g)


def _top_desc(s, k, dst_ref, ts, with_rank=False):
    cur = s
    rank = jnp.full(s.shape, float(k), F32) if with_rank else None
    for r in range(k):
        m = jnp.max(cur, axis=0, keepdims=True)
        dst_ref[r:r + 1, ts] = m
        if with_rank or r + 1 < k:
            hit = cur == m
            if with_rank:
                rank = jnp.where(hit, float(r), rank)
            cur = jnp.where(hit, -jnp.inf, cur)
    return rank


def _psel_body(hnt_ref, wq_ref, sk_ref, cnt_ref, e1_ref, r2_ref, e2_ref, s_ref, v1_ref, v2_ref, vc_ref):
    hnt = hnt_ref[...]
    for h in range(PEER_HEADS):
        q = jnp.dot(wq_ref[h * 256:(h + 1) * 256, :], hnt, preferred_element_type=F32)
        s_ref[0] = jnp.dot(sk_ref[h, 0], q[:128].astype(BF16), preferred_element_type=F32)
        s_ref[1] = jnp.dot(sk_ref[h, 1], q[128:].astype(BF16), preferred_element_type=F32)
        for tc in range(hnt.shape[1] // LANES):
            ts = slice(tc * LANES, (tc + 1) * LANES)
            s1 = s_ref[0, :, ts]
            s2 = s_ref[1, :, ts]
            _top_desc(s1, PEER_TOPK, v1_ref, ts)
            rank2 = _top_desc(s2, PEER_TOPK, v2_ref, ts, with_rank=True)
            v1 = v1_ref[:, ts]
            v2 = v2_ref[:, ts]
            cand = jnp.concatenate([v2 + v1[0:1]] + [v2[0:8] + v1[a:a + 1] for a in range(1, 8)]
                                   + [v1[8:16] + v2[0:1]], axis=0)
            _top_desc(cand, PEER_TOPK, vc_ref, ts)
            tau = vc_ref[PEER_TOPK - 1:PEER_TOPK, ts]
            top = vc_ref[0:1, ts]
            zsum = jnp.sum(jnp.where(cand >= tau, jnp.exp(cand - top), 0.0), axis=0, keepdims=True)
            cnt = jnp.zeros(s1.shape, F32)
            for b in range(PEER_TOPK):
                cnt = cnt + jnp.where(s1 + v2[b:b + 1] >= tau, 1.0, 0.0)
            cnt_ref[h, :, ts] = cnt
            e1_ref[h, :, ts] = jnp.exp(s1 - v1[0:1]) / zsum
            r2_ref[h, :, ts] = pltpu.bitcast(rank2.astype(BF16), I32)
            e2_ref[h, :, ts] = pltpu.bitcast(jnp.exp(s2 - v2[0:1]).astype(BF16), I32)


def _psel(hnt, wq_t, sk, tn):
    n = hnt.shape[1]
    spec = pl.BlockSpec((PEER_HEADS, PEER_NKEYS, tn), lambda i: (0, 0, i))
    shp32 = jax.ShapeDtypeStruct((PEER_HEADS, PEER_NKEYS, n), F32)
    shp16 = jax.ShapeDtypeStruct((PEER_HEADS, PEER_NKEYS // 2, n), I32)
    spec16 = pl.BlockSpec((PEER_HEADS, PEER_NKEYS // 2, tn), lambda i: (0, 0, i))
    return pl.pallas_call(
        _psel_body,
        grid=(n // tn,),
        in_specs=[pl.BlockSpec((D_MODEL, tn), lambda i: (0, i)),
                  pl.BlockSpec((PEER_HEADS * 256, D_MODEL), lambda i: (0, 0)),
                  pl.BlockSpec((PEER_HEADS, 2, PEER_NKEYS, 128), lambda i: (0, 0, 0, 0))],
        out_specs=(spec, spec, spec16, spec16),
        out_shape=(shp32, shp32, shp16, shp16),
        scratch_shapes=[pltpu.VMEM((2, PEER_NKEYS, tn), F32)] + [pltpu.VMEM((PEER_TOPK, tn), F32)] * 3,
        compiler_params=_params(("arbitrary",)),
        name="psel",
    )(hnt, wq_t, sk)


def _pdense_body(*refs, eb, nchunk, nblk):
    hnt_ref, u_ref, vt_ref = refs[:3]
    cnt_refs = refs[3:3 + nchunk]
    e1_refs = refs[3 + nchunk:3 + 2 * nchunk]
    r2_ref, e2_ref, h1_ref, g_ref, out_ref, acc_ref, a0_ref, a1_ref, c0_ref, c1_ref = refs[3 + 2 * nchunk:]
    a_refs = (a0_ref, a1_ref)
    c_refs = (c0_ref, c1_ref)
    s = pl.program_id(1)
    nsub = eb // PEER_NKEYS

    def row_bf16(ref, h, i1):
        return jnp.broadcast_to(ref[h, pl.ds(i1, 1), :], (PEER_NKEYS, LANES)).astype(BF16)

    def stage_a(a_ref):
        for half in range(eb // PDENSE_ROWS):
            rows = slice(half * PDENSE_ROWS, (half + 1) * PDENSE_ROWS)
            a_ref[rows, :] = jnp.dot(u_ref[rows, :], hnt_ref[...], preferred_element_type=F32)

    def stage_b(blk, a_ref, coef_ref):
        for cp in range(0, nsub, PDENSE_GROUP):
            cs = [cp + k for k in range(PDENSE_GROUP)]
            for tc in range(nchunk):
                ts = slice(tc * LANES, (tc + 1) * LANES)
                gates = [None] * PDENSE_GROUP
                for h in range(PEER_HEADS):
                    r2 = pltpu.bitcast(r2_ref[h, :, ts], BF16)
                    e2 = pltpu.bitcast(e2_ref[h, :, ts], BF16)
                    for k in range(PDENSE_GROUP):
                        cnt = row_bf16(cnt_refs[tc], h, blk * nsub + cs[k])
                        e1 = row_bf16(e1_refs[tc], h, blk * nsub + cs[k])
                        term = jnp.where(r2 < cnt, e2 * e1, jnp.zeros((), BF16))
                        gates[k] = term if gates[k] is None else gates[k] + term
                for k in range(PDENSE_GROUP):
                    rows = slice(cs[k] * PEER_NKEYS, (cs[k] + 1) * PEER_NKEYS)
                    at = a_ref[rows, ts]
                    gelu = (0.5 * at) * (1.0 + lax.erf(at * (0.5 ** 0.5)))
                    coef_ref[rows, ts] = gelu.astype(BF16) * gates[k]

    def stage_c(coef_ref):
        acc_ref[...] += jnp.dot(vt_ref[...], coef_ref[...], preferred_element_type=F32)

    steady = jnp.logical_and(s >= 2, s < nblk)
    for par in range(2):
        @pl.when(jnp.logical_and(steady, s % 2 == par))
        def _(par=par):
            stage_a(a_refs[par])
            stage_b(s - 1, a_refs[1 - par], c_refs[1 - par])
            stage_c(c_refs[par])

    @pl.when(s == 0)
    def _():
        acc_ref[...] = jnp.zeros(acc_ref.shape, F32)
        stage_a(a_refs[0])

    @pl.when(s == 1)
    def _():
        stage_a(a_refs[1])
        stage_b(0, a_refs[0], c_refs[0])

    @pl.when(s == nblk)
    def _():
        stage_b(nblk - 1, a_refs[(nblk - 1) % 2], c_refs[(nblk - 1) % 2])
        stage_c(c_refs[nblk % 2])

    @pl.when(s == nblk + 1)
    def _():
        stage_c(c_refs[(nblk - 1) % 2])
        out_ref[...] = _rms(h1_ref[...] + acc_ref[...].T, g_ref[...])


def _pdense(hnt, u, vt, cnt, e1, r2, e2, h1, g, tn, eb):
    n = hnt.shape[1]
    nblk = u.shape[0] // eb
    assert nblk >= 2 and vt.shape == (nblk, D_MODEL, eb)
    nchunk = tn // LANES
    chunk = [pl.BlockSpec((PEER_HEADS, PEER_NKEYS, LANES), lambda i, s, c=c: (0, 0, i * nchunk + c))
             for c in range(nchunk)]
    sel16 = pl.BlockSpec((PEER_HEADS, PEER_NKEYS // 2, tn), lambda i, s: (0, 0, i))
    return pl.pallas_call(
        functools.partial(_pdense_body, eb=eb, nchunk=nchunk, nblk=nblk),
        grid=(n // tn, nblk + 2),
        in_specs=[pl.BlockSpec((D_MODEL, tn), lambda i, s: (0, i)),
                  pl.BlockSpec((eb, D_MODEL), lambda i, s: (jnp.minimum(s, nblk - 1), 0)),
                  pl.BlockSpec((None, D_MODEL, eb), lambda i, s: (jnp.clip(s - 2, 0, nblk - 1), 0, 0))]
                 + chunk + chunk
                 + [sel16, sel16,
                    pl.BlockSpec((tn, D_MODEL), lambda i, s: (i, 0)),
                    pl.BlockSpec((1, D_MODEL), lambda i, s: (0, 0))],
        out_specs=pl.BlockSpec((tn, D_MODEL), lambda i, s: (i, 0)),
        out_shape=jax.ShapeDtypeStruct((n, D_MODEL), F32),
        scratch_shapes=[pltpu.VMEM((D_MODEL, tn), F32),
                        pltpu.VMEM((eb, tn), F32), pltpu.VMEM((eb, tn), F32),
                        pltpu.VMEM((eb, tn), BF16), pltpu.VMEM((eb, tn), BF16)],
        compiler_params=_params(("arbitrary", "arbitrary")),
        name="pdense",
    )(hnt, u, vt, *([cnt] * nchunk), *([e1] * nchunk), r2, e2, h1, g)


def kernel(x, meta_tokens, norm_mix, w_in, w_uv, w_out, norm_ffn, peer_wq, peer_subkeys,
           peer_u, peer_v, norm_final):
    batch, seq, d = x.shape
    assert d == D_MODEL and norm_mix.shape[0] == 1 and seq % DSA_BLOCK == 0
    n = batch * seq
    topk = min(DSA_TOPK_MAX, seq // 4)
    tm = 512 if seq % 512 == 0 else 256

    w = jnp.pad(w_in[0], ((0, 0), (0, IN_WIDTH_PAD - IN_WIDTH))).astype(BF16)
    g_mix = norm_mix[0][None, :]
    x2d = x.reshape(n, d)
    pos_f = jnp.arange(seq, dtype=F32) + float(N_META)
    pos_m = jnp.arange(N_META, dtype=F32)

    rq, rk, rv, rg, dq, dk, dv, iq, misc = _proj(x2d, g_mix, w, _rope_tables(pos_f), tm, seq // tm)
    _, mk, mv, _, _, mdk, mdv, _, mmisc = _proj(meta_tokens, g_mix, w, _rope_tables(pos_m), N_META, 1)

    ret = _retention(rq, rk, rv, rg, mk, mv, batch, seq)

    def keys(frames, meta):
        width = frames.shape[-1]
        lead = jnp.concatenate([jnp.zeros((LANES - N_META, width), frames.dtype), meta], axis=0)
        lead = jnp.broadcast_to(lead[None], (batch, LANES, width))
        return jnp.concatenate([lead, frames.reshape(batch, seq, width)], axis=1)

    kpad = keys(dk, mdk)
    vpad = keys(dv, mdv)
    kipad = keys(misc[:, :IDX_DIM].astype(BF16), mmisc[:, :IDX_DIM].astype(BF16))
    att = _dsa(dq, iq, misc, kpad, vpad, kipad, w_uv[0].astype(BF16), batch, seq, topk)

    h1, hnt = _post(ret, att, x2d, w_out[0].astype(BF16), norm_ffn[0][None, :], tm)

    wq_t = peer_wq[0].reshape(d, PEER_HEADS * 256).T.astype(BF16)
    cnt, e1, r2, e2 = _psel(hnt, wq_t, peer_subkeys[0].astype(BF16), tm)
    vt = peer_v[0].reshape(-1, PDENSE_EXPERTS, d).transpose(0, 2, 1).astype(BF16)
    out = _pdense(hnt, peer_u[0].astype(BF16), vt, cnt, e1, r2, e2, h1,
                  norm_final[None, :], tm, PDENSE_EXPERTS)
    return out.reshape(batch, seq, d)
```

```python
import functools

import jax
import jax.numpy as jnp
from jax import lax
from jax.experimental import pallas as pl
from jax.experimental.pallas import tpu as pltpu

F32 = jnp.float32
BF16 = jnp.bfloat16
I32 = jnp.int32

D_MODEL = 1024
N_META = 16
CHUNK = 64
RET_HEADS = 4
RET_DIM = 128
DSA_HEADS = 8
DSA_DIM = 128
DSA_OUT_DIM = 64
IDX_HEADS = 8
IDX_DIM = 64
DSA_TOPK_MAX = 256
ROPE_THETA = 10000.0
PEER_HEADS = 8
PEER_NKEYS = 128
PEER_TOPK = 16
EPS = 1e-6
NEG = -1e30
INT_MIN = -(2 ** 31)

IN_WIDTH = 3912
IN_WIDTH_PAD = 3968
LANES = 128
RET_BLOCK = 256
PDENSE_EXPERTS = 1024
PDENSE_ROWS = 512
PDENSE_GROUP = 2
DSA_BLOCK = 256
DSA_BITS_PER_CHECK = 4
VMEM_LIMIT = 56 * 1024 * 1024

NT = (((1,), (1,)), ((), ()))
TN = (((0,), (0,)), ((), ()))


def _rms(x, g):
    return x * lax.rsqrt(jnp.mean(x * x, axis=-1, keepdims=True) + EPS) * g


def _params(sem):
    return pltpu.CompilerParams(dimension_semantics=sem, vmem_limit_bytes=VMEM_LIMIT)


def _proj_body(x_ref, g_ref, w_ref, tab_ref, rq_ref, rk_ref, rv_ref, rg_ref, dq_ref, dk_ref,
               dv_ref, iq_ref, misc_ref):
    xn = _rms(x_ref[...], g_ref[...]).astype(BF16)
    cos_a, sin_a, cos_b, sin_b1, sin_b2, cos_m, sin_m1, sin_m2 = (tab_ref[i] for i in range(8))

    def mm(lo, n):
        return jnp.dot(xn, w_ref[:, lo:lo + n], preferred_element_type=F32)

    def rope_a(y):
        return y * cos_a + pltpu.roll(y, 64, 1) * sin_a

    def rope_b(y, c, s1, s2):
        return y * c + pltpu.roll(y, 96, 1) * s1 + pltpu.roll(y, 32, 1) * s2

    y = mm(0, 512)
    for h in range(RET_HEADS):
        rq_ref[:, h * 128:(h + 1) * 128] = rope_a(y[:, h * 128:(h + 1) * 128]).astype(BF16)
    y = mm(512, 512)
    for h in range(RET_HEADS):
        rk_ref[:, h * 128:(h + 1) * 128] = (
            rope_a(y[:, h * 128:(h + 1) * 128]) * (RET_DIM ** -0.5)).astype(BF16)
    rv_ref[...] = mm(1024, 512).astype(BF16)
    rg_ref[...] = mm(1536, 512)
    y = mm(2048, 1024)
    for h in range(DSA_HEADS):
        dq_ref[h] = (rope_a(y[:, h * 128:(h + 1) * 128]) * (DSA_DIM ** -0.5)).astype(BF16)
    dk_ref[...] = rope_a(mm(3072, 128)).astype(BF16)
    dv_ref[...] = mm(3200, 128).astype(BF16)
    y = mm(3328, 512)
    for p in range(IDX_HEADS // 2):
        o = (rope_b(y[:, p * 128:(p + 1) * 128], cos_b, sin_b1, sin_b2) * (IDX_DIM ** -0.5)).astype(BF16)
        iq_ref[2 * p] = o[:, :64]
        iq_ref[2 * p + 1] = o[:, 64:]
    misc_ref[...] = rope_b(mm(3840, 128), cos_m, sin_m1, sin_m2)


def _proj(x2d, g, w, tab, tm, tab_blocks):
    n = x2d.shape[0]
    row = lambda i: (i, 0)
    head = lambda i: (0, i, 0)
    out_shape = (
        jax.ShapeDtypeStruct((n, 512), BF16), jax.ShapeDtypeStruct((n, 512), BF16),
        jax.ShapeDtypeStruct((n, 512), BF16), jax.ShapeDtypeStruct((n, 512), F32),
        jax.ShapeDtypeStruct((DSA_HEADS, n, 128), BF16), jax.ShapeDtypeStruct((n, 128), BF16),
        jax.ShapeDtypeStruct((n, 128), BF16), jax.ShapeDtypeStruct((IDX_HEADS, n, 64), BF16),
        jax.ShapeDtypeStruct((n, 128), F32))
    out_specs = (
        pl.BlockSpec((tm, 512), row), pl.BlockSpec((tm, 512), row), pl.BlockSpec((tm, 512), row),
        pl.BlockSpec((tm, 512), row), pl.BlockSpec((DSA_HEADS, tm, 128), head),
        pl.BlockSpec((tm, 128), row), pl.BlockSpec((tm, 128), row),
        pl.BlockSpec((IDX_HEADS, tm, 64), head), pl.BlockSpec((tm, 128), row))
    return pl.pallas_call(
        _proj_body,
        grid=(n // tm,),
        in_specs=[pl.BlockSpec((tm, D_MODEL), row),
                  pl.BlockSpec((1, D_MODEL), lambda i: (0, 0)),
                  pl.BlockSpec((D_MODEL, IN_WIDTH_PAD), lambda i: (0, 0)),
                  pl.BlockSpec((8, tm, 128), lambda i: (0, i % tab_blocks, 0))],
        out_specs=out_specs,
        out_shape=out_shape,
        compiler_params=_params(("arbitrary",)),
        name="proj",
    )(x2d, g, w, tab)


def _rope_tables(pos):
    p = pos.shape[0]
    inv = ROPE_THETA ** (-jnp.arange(0, 64, dtype=F32) * 2.0 / 128)
    ang = pos[:, None] * inv[None, :]
    c, s = jnp.cos(ang), jnp.sin(ang)
    cos_a = jnp.concatenate([c, c], -1)
    sin_a = jnp.concatenate([-s, s], -1)
    inv = ROPE_THETA ** (-jnp.arange(0, 32, dtype=F32) * 2.0 / 64)
    ang = pos[:, None] * inv[None, :]
    c, s = jnp.cos(ang), jnp.sin(ang)
    z = jnp.zeros_like(s)
    z64 = jnp.zeros((p, 64), F32)
    cos_b = jnp.concatenate([c, c, c, c], -1)
    sin_b1 = jnp.concatenate([-s, z, -s, z], -1)
    sin_b2 = jnp.concatenate([z, s, z, s], -1)
    cos_m = jnp.concatenate([c, c, jnp.full((p, 64), IDX_HEADS ** -0.5, F32)], -1)
    sin_m1 = jnp.concatenate([-s, z, z64], -1)
    sin_m2 = jnp.concatenate([z, s, z64], -1)
    return jnp.stack([cos_a, sin_a, cos_b, sin_b1, sin_b2, cos_m, sin_m1, sin_m2])


def _ret_body(rq_ref, rk_ref, rv_ref, rg_ref, mk_ref, mv_ref, dmat_ref, xi_ref, zeta_ref,
              mzeta_ref, gsc_ref, out_ref, r_ref):
    @pl.when(pl.program_id(1) == 0)
    def _():
        for h in range(RET_HEADS):
            hs = slice(h * 128, (h + 1) * 128)
            kz = (mk_ref[:, hs].astype(F32) * mzeta_ref[h]).astype(BF16)
            r_ref[h] = lax.dot_general(kz, mv_ref[:, hs], TN, preferred_element_type=F32)

    for h in range(RET_HEADS):
        hs = slice(h * 128, (h + 1) * 128)
        q = rq_ref[:, hs]
        k = rk_ref[:, hs]
        v = rv_ref[:, hs]
        s = lax.dot_general(q, k, NT, preferred_element_type=F32) * dmat_ref[h]
        o = jnp.dot(s.astype(BF16), v, preferred_element_type=F32)
        r = r_ref[h]
        qx = (q.astype(F32) * xi_ref[h]).astype(BF16)
        o = o + jnp.dot(qx, r.astype(BF16), preferred_element_type=F32)
        kz = (k.astype(F32) * zeta_ref[h]).astype(BF16)
        u = lax.dot_general(kz, v, TN, preferred_element_type=F32)
        r_ref[h] = r * gsc_ref[h] + u
        mu = jnp.mean(o, axis=-1, keepdims=True)
        d = o - mu
        var = jnp.mean(d * d, axis=-1, keepdims=True)
        on = d * lax.rsqrt(var + EPS)
        g = rg_ref[:, hs]
        out_ref[:, hs] = (g * jax.nn.sigmoid(g) * on).astype(BF16)


def _retention(rq, rk, rv, rg, mk, mv, batch, seq):
    n = rq.shape[0]
    nb = seq // RET_BLOCK
    lg = jnp.log(1.0 - 2.0 ** (-5.0 - jnp.arange(RET_HEADS, dtype=F32)))
    i = jnp.arange(RET_BLOCK, dtype=F32)
    ci = jnp.arange(RET_BLOCK) // CHUNK
    vis = (ci[None, :] <= ci[:, None])
    dmat = jnp.where(vis[None], jnp.exp(lg[:, None, None] * jnp.abs(i[:, None] - i[None, :])), 0.0)
    xi = jnp.broadcast_to(jnp.exp(lg[:, None] * (i + 1.0)[None, :])[:, :, None], (RET_HEADS, RET_BLOCK, 128))
    zeta = jnp.broadcast_to(jnp.exp(lg[:, None] * (RET_BLOCK - 1.0 - i)[None, :])[:, :, None],
                            (RET_HEADS, RET_BLOCK, 128))
    im = jnp.arange(N_META, dtype=F32)
    mzeta = jnp.broadcast_to(jnp.exp(lg[:, None] * (N_META - 1.0 - im)[None, :])[:, :, None],
                             (RET_HEADS, N_META, 128))
    gsc = jnp.broadcast_to(jnp.exp(lg * RET_BLOCK)[:, None, None], (RET_HEADS, 1, 128))
    row = lambda b, s: (b * nb + s, 0)
    c2 = lambda b, s: (0, 0)
    c3 = lambda b, s: (0, 0, 0)
    return pl.pallas_call(
        _ret_body,
        grid=(batch, nb),
        in_specs=[pl.BlockSpec((RET_BLOCK, 512), row), pl.BlockSpec((RET_BLOCK, 512), row),
                  pl.BlockSpec((RET_BLOCK, 512), row), pl.BlockSpec((RET_BLOCK, 512), row),
                  pl.BlockSpec((N_META, 512), c2), pl.BlockSpec((N_META, 512), c2),
                  pl.BlockSpec((RET_HEADS, RET_BLOCK, RET_BLOCK), c3),
                  pl.BlockSpec((RET_HEADS, RET_BLOCK, 128), c3),
                  pl.BlockSpec((RET_HEADS, RET_BLOCK, 128), c3),
                  pl.BlockSpec((RET_HEADS, N_META, 128), c3),
                  pl.BlockSpec((RET_HEADS, 1, 128), c3)],
        out_specs=pl.BlockSpec((RET_BLOCK, 512), row),
        out_shape=jax.ShapeDtypeStruct((n, 512), BF16),
        scratch_shapes=[pltpu.VMEM((RET_HEADS, 128, 128), F32)],
        compiler_params=_params(("arbitrary", "arbitrary")),
        name="retention",
    )(rq, rk, rv, rg, mk, mv, dmat, xi, zeta, mzeta, gsc)


def _aligned(off):
    return off if isinstance(off, int) else pl.multiple_of(off, LANES)


def _dsa_body(qa_ref, qi_ref, misc_ref, k_ref, v_ref, ki_ref, wuv_ref, out_ref,
              isc_ref, bias_ref, m_ref, l_ref, acc_ref, *, topk):
    qb = pl.program_id(1)
    tq = DSA_BLOCK
    nh = DSA_HEADS

    wt = misc_ref[...].T
    qi = qi_ref[...].reshape(IDX_HEADS * tq, IDX_DIM)

    def isc_tile(koff, width, allowed):
        kt = ki_ref[pl.ds(koff, width), :]
        z = lax.dot_general(kt, qi, NT, preferred_element_type=F32)
        isc = None
        for h in range(IDX_HEADS):
            term = jnp.maximum(z[:, h * tq:(h + 1) * tq], 0.0) * wt[64 + h:65 + h, :]
            isc = term if isc is None else isc + term
        if allowed is not None:
            isc = jnp.where(allowed, isc, -jnp.inf)
        isc_ref[pl.ds(koff, width), :] = isc

    isc_tile(0, LANES, lax.broadcasted_iota(I32, (LANES, tq), 0) >= LANES - N_META)

    def body_a(t, c):
        isc_tile(pl.multiple_of(LANES + t * tq, LANES), tq, None)
        return c

    lax.fori_loop(0, qb, body_a, 0)
    keyc = lax.broadcasted_iota(I32, (tq, tq), 0) // CHUNK
    qryc = lax.broadcasted_iota(I32, (tq, tq), 1) // CHUNK
    diag_off = pl.multiple_of(LANES + qb * tq, LANES)
    isc_tile(diag_off, tq, keyc <= qryc)

    kf = float(topk)
    sub = 8

    def key_to_f32(key):
        return lax.bitcast_convert_type(jnp.where(key < 0, key ^ jnp.int32(0x7FFFFFFF), key), F32)

    def count(pred):
        def slab(off, a):
            hit = jnp.where(pred(isc_ref[pl.ds(off, LANES), :], off), 1.0, 0.0)
            return a + jnp.sum(hit.reshape(LANES // sub, sub, tq), axis=0)

        def body(t, a):
            off = pl.multiple_of(LANES + t * (2 * LANES), LANES)
            return slab(pl.multiple_of(off + LANES, LANES), slab(off, a))

        a = lax.fori_loop(0, qb + 1, body, slab(0, jnp.zeros((sub, tq), F32)))
        return jnp.sum(a, axis=0, keepdims=True)

    def count_ge(cand):
        return count(lambda x, off: x >= cand)

    lowest = jnp.full((1, tq), -3.0e38, F32)
    n_real = count_ge(lowest)
    c_pos = count_ge(jnp.zeros((1, tq), F32))
    pos = c_pos >= kf
    base0 = jnp.where(pos, jnp.int32(0), jnp.int32(INT_MIN))
    cnt0 = jnp.where(n_real <= kf, kf, jnp.where(pos, c_pos, n_real))

    def unfinished(cnt):
        return jnp.max(jnp.where(cnt != kf, 1.0, 0.0))

    def try_bit(bit, base, cnt):
        cand = base | lax.shift_left(jnp.int32(1), bit)
        c = count_ge(key_to_f32(cand))
        ok = c >= kf
        return jnp.where(ok, cand, base), jnp.where(ok, c, cnt)

    base, cnt = base0, cnt0
    for bit in range(30, DSA_BITS_PER_CHECK * 7 - 1, -1):
        base, cnt = try_bit(bit, base, cnt)

    zero_tie = jnp.logical_and(pos, count(lambda x, off: x > 0.0) < kf)

    def group_cond(state):
        g, _, cnt = state
        return jnp.logical_and(g >= 0, unfinished(jnp.where(zero_tie, kf, cnt)) > 0.0)

    def group_body(state):
        g, base, cnt = state
        for k in range(DSA_BITS_PER_CHECK - 1, -1, -1):
            base, cnt = try_bit(g * DSA_BITS_PER_CHECK + k, base, cnt)
        return g - 1, base, cnt

    _, base, cnt = lax.while_loop(group_cond, group_body, (jnp.int32(6), base, cnt))

    def refine_cond(state):
        it, _, _, _, go = state
        return jnp.logical_and(it < 40, go > 0.0)

    def refine_body(state):
        it, lo, hi, cnt, _ = state
        mid = lo + 0.5 * (hi - lo)
        c = count_ge(mid)
        ok = c >= kf
        cnt = jnp.where(ok, c, cnt)
        moving = jnp.logical_and(jnp.logical_and(mid > lo, mid < hi),
                                 jnp.logical_and(cnt != kf, jnp.logical_not(zero_tie)))
        return it + 1, jnp.where(ok, mid, lo), jnp.where(ok, hi, mid), cnt, jnp.max(jnp.where(moving, 1.0, 0.0))

    _, thr, _, cnt, _ = lax.while_loop(
        refine_cond, refine_body,
        (jnp.int32(0), key_to_f32(base), key_to_f32(base + 1), cnt, unfinished(jnp.where(zero_tie, kf, cnt))))
    thr = jnp.where(n_real <= kf, lowest, thr)

    key_id = lax.broadcasted_iota(I32, (LANES, tq), 0)
    index_bits = max(1, (isc_ref.shape[0] - 1).bit_length())

    def tie_bound():
        need = kf - count(lambda x, off: x > thr)

        def ties_below(bound):
            return count(lambda x, off: jnp.logical_and(x == thr, key_id < bound - off))

        def bit_step(i, last):
            cand = last | lax.shift_left(jnp.int32(1), index_bits - 1 - i)
            return jnp.where(ties_below(cand) < need, cand, last)

        return lax.fori_loop(0, index_bits, bit_step, jnp.zeros((1, tq), I32)) + 1

    everything = jnp.full((1, tq), isc_ref.shape[0], I32)
    tie_end = lax.cond(unfinished(cnt) > 0.0, tie_bound, lambda: everything)
    tie_end = jnp.where(cnt == kf, everything, tie_end)

    def mask_tile(off, width):
        x = isc_ref[pl.ds(off, width), :]
        ids = lax.broadcasted_iota(I32, (width, tq), 0)
        keep = jnp.logical_or(x > thr, jnp.logical_and(x == thr, ids < tie_end - off))
        bias_ref[:, pl.ds(off, width)] = jnp.where(keep, 0.0, NEG).T

    def body_m(t, c):
        mask_tile(pl.multiple_of(LANES + t * tq, LANES), tq)
        return c

    mask_tile(0, LANES)
    lax.fori_loop(0, qb + 1, body_m, 0)

    m_ref[...] = jnp.full(m_ref.shape, NEG, F32)
    l_ref[...] = jnp.zeros(l_ref.shape, F32)
    acc_ref[...] = jnp.zeros(acc_ref.shape, F32)
    qa = qa_ref[...].reshape(nh * tq, DSA_DIM)

    def att_tile(koff, width):
        kt = k_ref[pl.ds(koff, width), :]
        vt = v_ref[pl.ds(koff, width), :]
        s = lax.dot_general(qa, kt, NT, preferred_element_type=F32).reshape(nh, tq, width)
        reps = width // LANES
        s = (s + bias_ref[:, pl.ds(koff, width)][None]).reshape(nh * tq, width)
        m_prev = m_ref[...]
        m_new = jnp.maximum(m_prev, jnp.max(s, axis=1, keepdims=True))
        alpha = jnp.exp(m_prev - m_new)
        m_w = m_new if reps == 1 else jnp.concatenate([m_new] * reps, axis=1)
        p = jnp.exp(s - m_w)
        l_ref[...] = alpha * l_ref[...] + jnp.sum(p, axis=1, keepdims=True)
        acc_ref[...] = alpha * acc_ref[...] + jnp.dot(p.astype(BF16), vt, preferred_element_type=F32)
        m_ref[...] = m_new

    att_tile(0, LANES)

    def body_c(t, c):
        att_tile(pl.multiple_of(LANES + t * tq, LANES), tq)
        return c

    lax.fori_loop(0, qb + 1, body_c, 0)

    o = (acc_ref[...] / l_ref[...]).astype(BF16).reshape(nh, tq, DSA_DIM)
    for h in range(nh):
        out_ref[:, h * DSA_OUT_DIM:(h + 1) * DSA_OUT_DIM] = jnp.dot(
            o[h], wuv_ref[h], preferred_element_type=F32).astype(BF16)


def _dsa(dq, iq, misc, kpad, vpad, kipad, wuv, batch, seq, topk):
    n = misc.shape[0]
    nq = seq // DSA_BLOCK
    tp = kpad.shape[1]
    head = lambda b, q: (0, b * nq + q, 0)
    row = lambda b, q: (b * nq + q, 0)
    per_b = lambda b, q: (b, 0, 0)
    return pl.pallas_call(
        functools.partial(_dsa_body, topk=topk),
        grid=(batch, nq),
        in_specs=[pl.BlockSpec((DSA_HEADS, DSA_BLOCK, DSA_DIM), head),
                  pl.BlockSpec((IDX_HEADS, DSA_BLOCK, IDX_DIM), head),
                  pl.BlockSpec((DSA_BLOCK, 128), row),
                  pl.BlockSpec((None, tp, DSA_DIM), per_b),
                  pl.BlockSpec((None, tp, DSA_DIM), per_b),
                  pl.BlockSpec((None, tp, IDX_DIM), per_b),
                  pl.BlockSpec((DSA_HEADS, DSA_DIM, DSA_OUT_DIM), lambda b, q: (0, 0, 0))],
        out_specs=pl.BlockSpec((DSA_BLOCK, 512), row),
        out_shape=jax.ShapeDtypeStruct((n, 512), BF16),
        scratch_shapes=[pltpu.VMEM((tp, DSA_BLOCK), F32),
                        pltpu.VMEM((DSA_BLOCK, tp), F32),
                        pltpu.VMEM((DSA_HEADS * DSA_BLOCK, LANES), F32),
                        pltpu.VMEM((DSA_HEADS * DSA_BLOCK, LANES), F32),
                        pltpu.VMEM((DSA_HEADS * DSA_BLOCK, DSA_DIM), F32)],
        compiler_params=_params(("arbitrary", "arbitrary")),
        name="dsa",
    )(dq, iq, misc, kpad, vpad, kipad, wuv)


def _post_body(ret_ref, att_ref, x_ref, w_ref, g_ref, h1_ref, hnt_ref):
    y = jnp.dot(ret_ref[...], w_ref[0:512, :], preferred_element_type=F32)
    y = y + jnp.dot(att_ref[...], w_ref[512:1024, :], preferred_element_type=F32)
    h1 = x_ref[...] + y
    h1_ref[...] = h1
    hnt_ref[...] = _rms(h1, g_ref[...]).T.astype(BF16)


def _post(ret, att, x2d, w_out, g, tm):
    n = x2d.shape[0]
    row = lambda i: (i, 0)
    return pl.pallas_call(
        _post_body,
        grid=(n // tm,),
        in_specs=[pl.BlockSpec((tm, 512), row), pl.BlockSpec((tm, 512), row),
                  pl.BlockSpec((tm, D_MODEL), row),
                  pl.BlockSpec((D_MODEL, D_MODEL), lambda i: (0, 0)),
                  pl.BlockSpec((1, D_MODEL), lambda i: (0, 0))],
        out_specs=(pl.BlockSpec((tm, D_MODEL), row), pl.BlockSpec((D_MODEL, tm), lambda i: (0, i))),
        out_shape=(jax.ShapeDtypeStruct((n, D_MODEL), F32), jax.ShapeDtypeStruct((D_MODEL, n), BF16)),
        compiler_params=_params(("arbitrary",)),
        name="post",
    )(ret, att, x2d, w_out, g)


def _top_desc(s, k, dst_ref, ts, with_rank=False):
    cur = s
    rank = jnp.full(s.shape, float(k), F32) if with_rank else None
    for r in range(k):
        m = jnp.max(cur, axis=0, keepdims=True)
        dst_ref[r:r + 1, ts] = m
        if with_rank or r + 1 < k:
            hit = cur == m
            if with_rank:
                rank = jnp.where(hit, float(r), rank)
            cur = jnp.where(hit, -jnp.inf, cur)
    return rank


def _psel_body(hnt_ref, wq_ref, sk_ref, cnt_ref, e1_ref, r2_ref, e2_ref, s_ref, v1_ref, v2_ref, vc_ref):
    hnt = hnt_ref[...]
    for h in range(PEER_HEADS):
        q = jnp.dot(wq_ref[h * 256:(h + 1) * 256, :], hnt, preferred_element_type=F32)
        s_ref[0] = jnp.dot(sk_ref[h, 0], q[:128].astype(BF16), preferred_element_type=F32)
        s_ref[1] = jnp.dot(sk_ref[h, 1], q[128:].astype(BF16), preferred_element_type=F32)
        for tc in range(hnt.shape[1] // LANES):
            ts = slice(tc * LANES, (tc + 1) * LANES)
            s1 = s_ref[0, :, ts]
            s2 = s_ref[1, :, ts]
            _top_desc(s1, PEER_TOPK, v1_ref, ts)
            rank2 = _top_desc(s2, PEER_TOPK, v2_ref, ts, with_rank=True)
            v1 = v1_ref[:, ts]
            v2 = v2_ref[:, ts]
            cand = jnp.concatenate([v2 + v1[0:1]] + [v2[0:8] + v1[a:a + 1] for a in range(1, 8)]
                                   + [v1[8:16] + v2[0:1]], axis=0)
            _top_desc(cand, PEER_TOPK, vc_ref, ts)
            tau = vc_ref[PEER_TOPK - 1:PEER_TOPK, ts]
            top = vc_ref[0:1, ts]
            zsum = jnp.sum(jnp.where(cand >= tau, jnp.exp(cand - top), 0.0), axis=0, keepdims=True)
            cnt = jnp.zeros(s1.shape, F32)
            for b in range(PEER_TOPK):
                cnt = cnt + jnp.where(s1 + v2[b:b + 1] >= tau, 1.0, 0.0)
            cnt_ref[h, :, ts] = cnt
            e1_ref[h, :, ts] = jnp.exp(s1 - v1[0:1]) / zsum
            r2_ref[h, :, ts] = pltpu.bitcast(rank2.astype(BF16), I32)
            e2_ref[h, :, ts] = pltpu.bitcast(jnp.exp(s2 - v2[0:1]).astype(BF16), I32)


def _psel(hnt, wq_t, sk, tn):
    n = hnt.shape[1]
    spec = pl.BlockSpec((PEER_HEADS, PEER_NKEYS, tn), lambda i: (0, 0, i))
    shp32 = jax.ShapeDtypeStruct((PEER_HEADS, PEER_NKEYS, n), F32)
    shp16 = jax.ShapeDtypeStruct((PEER_HEADS, PEER_NKEYS // 2, n), I32)
    spec16 = pl.BlockSpec((PEER_HEADS, PEER_NKEYS // 2, tn), lambda i: (0, 0, i))
    return pl.pallas_call(
        _psel_body,
        grid=(n // tn,),
        in_specs=[pl.BlockSpec((D_MODEL, tn), lambda i: (0, i)),
                  pl.BlockSpec((PEER_HEADS * 256, D_MODEL), lambda i: (0, 0)),
                  pl.BlockSpec((PEER_HEADS, 2, PEER_NKEYS, 128), lambda i: (0, 0, 0, 0))],
        out_specs=(spec, spec, spec16, spec16),
        out_shape=(shp32, shp32, shp16, shp16),
        scratch_shapes=[pltpu.VMEM((2, PEER_NKEYS, tn), F32)] + [pltpu.VMEM((PEER_TOPK, tn), F32)] * 3,
        compiler_params=_params(("arbitrary",)),
        name="psel",
    )(hnt, wq_t, sk)


def _pdense_body(*refs, eb, nchunk, nblk):
    hnt_ref, u_ref, vt_ref = refs[:3]
    cnt_refs = refs[3:3 + nchunk]
    e1_refs = refs[3 + nchunk:3 + 2 * nchunk]
    r2_ref, e2_ref, h1_ref, g_ref, out_ref, acc_ref, a_ref, coef_ref = refs[3 + 2 * nchunk:]
    s = pl.program_id(1)
    nsub = eb // PEER_NKEYS

    def row_bf16(ref, h, i1):
        return jnp.broadcast_to(ref[h, pl.ds(i1, 1), :], (PEER_NKEYS, LANES)).astype(BF16)

    def stage_a(a_ref):
        for half in range(eb // PDENSE_ROWS):
            rows = slice(half * PDENSE_ROWS, (half + 1) * PDENSE_ROWS)
            a_ref[rows, :] = jnp.dot(u_ref[rows, :], hnt_ref[...], preferred_element_type=F32)

    def stage_b(blk, a_ref, coef_ref):
        for cp in range(0, nsub, PDENSE_GROUP):
            cs = [cp + k for k in range(PDENSE_GROUP)]
            for tc in range(nchunk):
                ts = slice(tc * LANES, (tc + 1) * LANES)
                gates = [None] * PDENSE_GROUP
                for h in range(PEER_HEADS):
                    r2 = pltpu.bitcast(r2_ref[h, :, ts], BF16)
                    e2 = pltpu.bitcast(e2_ref[h, :, ts], BF16)
                    for k in range(PDENSE_GROUP):
                        cnt = row_bf16(cnt_refs[tc], h, blk * nsub + cs[k])
                        e1 = row_bf16(e1_refs[tc], h, blk * nsub + cs[k])
                        term = jnp.where(r2 < cnt, e2, jnp.zeros((), BF16)) * e1
                        gates[k] = term if gates[k] is None else gates[k] + term
                for k in range(PDENSE_GROUP):
                    rows = slice(cs[k] * PEER_NKEYS, (cs[k] + 1) * PEER_NKEYS)
                    at = a_ref[rows, ts]
                    gelu = (0.5 * at) * (1.0 + lax.erf(at * (0.5 ** 0.5)))
                    coef_ref[rows, ts] = gelu.astype(BF16) * gates[k]

    def stage_c(coef_ref):
        acc_ref[...] += jnp.dot(vt_ref[...], coef_ref[...], preferred_element_type=F32)

    @pl.when(s == 0)
    def _():
        acc_ref[...] = jnp.zeros(acc_ref.shape, F32)

    stage_a(a_ref)
    stage_b(s, a_ref, coef_ref)
    stage_c(coef_ref)

    @pl.when(s == nblk - 1)
    def _():
        out_ref[...] = _rms(h1_ref[...] + acc_ref[...].T, g_ref[...])


def _pdense(hnt, u, vt, cnt, e1, r2, e2, h1, g, tn, eb):
    n = hnt.shape[1]
    nblk = u.shape[0] // eb
    assert nblk >= 2 and vt.shape == (nblk, D_MODEL, eb)
    nchunk = tn // LANES
    chunk = [pl.BlockSpec((PEER_HEADS, PEER_NKEYS, LANES), lambda i, s, c=c: (0, 0, i * nchunk + c))
             for c in range(nchunk)]
    sel16 = pl.BlockSpec((PEER_HEADS, PEER_NKEYS // 2, tn), lambda i, s: (0, 0, i))
    return pl.pallas_call(
        functools.partial(_pdense_body, eb=eb, nchunk=nchunk, nblk=nblk),
        grid=(n // tn, nblk),
        in_specs=[pl.BlockSpec((D_MODEL, tn), lambda i, s: (0, i)),
                  pl.BlockSpec((eb, D_MODEL), lambda i, s: (s, 0)),
                  pl.BlockSpec((None, D_MODEL, eb), lambda i, s: (s, 0, 0))]
                 + chunk + chunk
                 + [sel16, sel16,
                    pl.BlockSpec((tn, D_MODEL), lambda i, s: (i, 0)),
                    pl.BlockSpec((1, D_MODEL), lambda i, s: (0, 0))],
        out_specs=pl.BlockSpec((tn, D_MODEL), lambda i, s: (i, 0)),
        out_shape=jax.ShapeDtypeStruct((n, D_MODEL), F32),
        scratch_shapes=[pltpu.VMEM((D_MODEL, tn), F32),
                        pltpu.VMEM((eb, tn), F32), pltpu.VMEM((eb, tn), BF16)],
        compiler_params=_params(("arbitrary", "arbitrary")),
        name="pdense",
    )(hnt, u, vt, *([cnt] * nchunk), *([e1] * nchunk), r2, e2, h1, g)


def kernel(x, meta_tokens, norm_mix, w_in, w_uv, w_out, norm_ffn, peer_wq, peer_subkeys,
           peer_u, peer_v, norm_final):
    batch, seq, d = x.shape
    assert d == D_MODEL and norm_mix.shape[0] == 1 and seq % DSA_BLOCK == 0
    n = batch * seq
    topk = min(DSA_TOPK_MAX, seq // 4)
    tm = 512 if seq % 512 == 0 else 256

    w = jnp.pad(w_in[0], ((0, 0), (0, IN_WIDTH_PAD - IN_WIDTH))).astype(BF16)
    g_mix = norm_mix[0][None, :]
    x2d = x.reshape(n, d)
    pos_f = jnp.arange(seq, dtype=F32) + float(N_META)
    pos_m = jnp.arange(N_META, dtype=F32)

    rq, rk, rv, rg, dq, dk, dv, iq, misc = _proj(x2d, g_mix, w, _rope_tables(pos_f), tm, seq // tm)
    _, mk, mv, _, _, mdk, mdv, _, mmisc = _proj(meta_tokens, g_mix, w, _rope_tables(pos_m), N_META, 1)

    ret = _retention(rq, rk, rv, rg, mk, mv, batch, seq)

    def keys(frames, meta):
        width = frames.shape[-1]
        lead = jnp.concatenate([jnp.zeros((LANES - N_META, width), frames.dtype), meta], axis=0)
        lead = jnp.broadcast_to(lead[None], (batch, LANES, width))
        return jnp.concatenate([lead, frames.reshape(batch, seq, width)], axis=1)

    kpad = keys(dk, mdk)
    vpad = keys(dv, mdv)
    kipad = keys(misc[:, :IDX_DIM].astype(BF16), mmisc[:, :IDX_DIM].astype(BF16))
    att = _dsa(dq, iq, misc, kpad, vpad, kipad, w_uv[0].astype(BF16), batch, seq, topk)

    h1, hnt = _post(ret, att, x2d, w_out[0].astype(BF16), norm_ffn[0][None, :], tm)

    wq_t = peer_wq[0].reshape(d, PEER_HEADS * 256).T.astype(BF16)
    cnt, e1, r2, e2 = _psel(hnt, wq_t, peer_subkeys[0].astype(BF16), tm)
    vt = peer_v[0].reshape(-1, PDENSE_EXPERTS, d).transpose(0, 2, 1).astype(BF16)
    out = _pdense(hnt, peer_u[0].astype(BF16), vt, cnt, e1, r2, e2, h1,
                  norm_final[None, :], tm, PDENSE_EXPERTS)
    return out.reshape(batch, seq, d)
```

```python
import functools

import jax
import jax.numpy as jnp
from jax import lax
from jax.experimental import pallas as pl
from jax.experimental.pallas import tpu as pltpu

F32 = jnp.float32
BF16 = jnp.bfloat16
I32 = jnp.int32

D_MODEL = 1024
N_META = 16
CHUNK = 64
RET_HEADS = 4
RET_DIM = 128
DSA_HEADS = 8
DSA_DIM = 128
DSA_OUT_DIM = 64
IDX_HEADS = 8
IDX_DIM = 64
DSA_TOPK_MAX = 256
ROPE_THETA = 10000.0
PEER_HEADS = 8
PEER_NKEYS = 128
PEER_TOPK = 16
EPS = 1e-6
NEG = -1e30
INT_MIN = -(2 ** 31)

IN_WIDTH = 3912
IN_WIDTH_PAD = 3968
LANES = 128
RET_BLOCK = 256
PDENSE_EXPERTS = 1024
PDENSE_ROWS = 512
PDENSE_GROUP = 2
DSA_BLOCK = 256
DSA_BITS_PER_CHECK = 4
VMEM_LIMIT = 56 * 1024 * 1024

NT = (((1,), (1,)), ((), ()))
TN = (((0,), (0,)), ((), ()))


def _rms(x, g):
    return x * lax.rsqrt(jnp.mean(x * x, axis=-1, keepdims=True) + EPS) * g


def _params(sem):
    return pltpu.CompilerParams(dimension_semantics=sem, vmem_limit_bytes=VMEM_LIMIT)


def _proj_body(x_ref, g_ref, w_ref, tab_ref, rq_ref, rk_ref, rv_ref, rg_ref, dq_ref, dk_ref,
               dv_ref, iq_ref, misc_ref):
    xn = _rms(x_ref[...], g_ref[...]).astype(BF16)
    cos_a, sin_a, cos_b, sin_b1, sin_b2, cos_m, sin_m1, sin_m2 = (tab_ref[i] for i in range(8))

    def mm(lo, n):
        return jnp.dot(xn, w_ref[:, lo:lo + n], preferred_element_type=F32)

    def rope_a(y):
        return y * cos_a + pltpu.roll(y, 64, 1) * sin_a

    def rope_b(y, c, s1, s2):
        return y * c + pltpu.roll(y, 96, 1) * s1 + pltpu.roll(y, 32, 1) * s2

    y = mm(0, 512)
    for h in range(RET_HEADS):
        rq_ref[:, h * 128:(h + 1) * 128] = rope_a(y[:, h * 128:(h + 1) * 128]).astype(BF16)
    y = mm(512, 512)
    for h in range(RET_HEADS):
        rk_ref[:, h * 128:(h + 1) * 128] = (
            rope_a(y[:, h * 128:(h + 1) * 128]) * (RET_DIM ** -0.5)).astype(BF16)
    rv_ref[...] = mm(1024, 512).astype(BF16)
    rg_ref[...] = mm(1536, 512)
    y = mm(2048, 1024)
    for h in range(DSA_HEADS):
        dq_ref[h] = (rope_a(y[:, h * 128:(h + 1) * 128]) * (DSA_DIM ** -0.5)).astype(BF16)
    dk_ref[...] = rope_a(mm(3072, 128)).astype(BF16)
    dv_ref[...] = mm(3200, 128).astype(BF16)
    y = mm(3328, 512)
    for p in range(IDX_HEADS // 2):
        o = (rope_b(y[:, p * 128:(p + 1) * 128], cos_b, sin_b1, sin_b2) * (IDX_DIM ** -0.5)).astype(BF16)
        iq_ref[2 * p] = o[:, :64]
        iq_ref[2 * p + 1] = o[:, 64:]
    misc_ref[...] = rope_b(mm(3840, 128), cos_m, sin_m1, sin_m2)


def _proj(x2d, g, w, tab, tm, tab_blocks):
    n = x2d.shape[0]
    row = lambda i: (i, 0)
    head = lambda i: (0, i, 0)
    out_shape = (
        jax.ShapeDtypeStruct((n, 512), BF16), jax.ShapeDtypeStruct((n, 512), BF16),
        jax.ShapeDtypeStruct((n, 512), BF16), jax.ShapeDtypeStruct((n, 512), F32),
        jax.ShapeDtypeStruct((DSA_HEADS, n, 128), BF16), jax.ShapeDtypeStruct((n, 128), BF16),
        jax.ShapeDtypeStruct((n, 128), BF16), jax.ShapeDtypeStruct((IDX_HEADS, n, 64), BF16),
        jax.ShapeDtypeStruct((n, 128), F32))
    out_specs = (
        pl.BlockSpec((tm, 512), row), pl.BlockSpec((tm, 512), row), pl.BlockSpec((tm, 512), row),
        pl.BlockSpec((tm, 512), row), pl.BlockSpec((DSA_HEADS, tm, 128), head),
        pl.BlockSpec((tm, 128), row), pl.BlockSpec((tm, 128), row),
        pl.BlockSpec((IDX_HEADS, tm, 64), head), pl.BlockSpec((tm, 128), row))
    return pl.pallas_call(
        _proj_body,
        grid=(n // tm,),
        in_specs=[pl.BlockSpec((tm, D_MODEL), row),
                  pl.BlockSpec((1, D_MODEL), lambda i: (0, 0)),
                  pl.BlockSpec((D_MODEL, IN_WIDTH_PAD), lambda i: (0, 0)),
                  pl.BlockSpec((8, tm, 128), lambda i: (0, i % tab_blocks, 0))],
        out_specs=out_specs,
        out_shape=out_shape,
        compiler_params=_params(("arbitrary",)),
        name="proj",
    )(x2d, g, w, tab)


def _rope_tables(pos):
    p = pos.shape[0]
    inv = ROPE_THETA ** (-jnp.arange(0, 64, dtype=F32) * 2.0 / 128)
    ang = pos[:, None] * inv[None, :]
    c, s = jnp.cos(ang), jnp.sin(ang)
    cos_a = jnp.concatenate([c, c], -1)
    sin_a = jnp.concatenate([-s, s], -1)
    inv = ROPE_THETA ** (-jnp.arange(0, 32, dtype=F32) * 2.0 / 64)
    ang = pos[:, None] * inv[None, :]
    c, s = jnp.cos(ang), jnp.sin(ang)
    z = jnp.zeros_like(s)
    z64 = jnp.zeros((p, 64), F32)
    cos_b = jnp.concatenate([c, c, c, c], -1)
    sin_b1 = jnp.concatenate([-s, z, -s, z], -1)
    sin_b2 = jnp.concatenate([z, s, z, s], -1)
    cos_m = jnp.concatenate([c, c, jnp.full((p, 64), IDX_HEADS ** -0.5, F32)], -1)
    sin_m1 = jnp.concatenate([-s, z, z64], -1)
    sin_m2 = jnp.concatenate([z, s, z64], -1)
    return jnp.stack([cos_a, sin_a, cos_b, sin_b1, sin_b2, cos_m, sin_m1, sin_m2])


def _ret_body(rq_ref, rk_ref, rv_ref, rg_ref, mk_ref, mv_ref, dmat_ref, xi_ref, zeta_ref,
              mzeta_ref, gsc_ref, out_ref, r_ref):
    @pl.when(pl.program_id(1) == 0)
    def _():
        for h in range(RET_HEADS):
            hs = slice(h * 128, (h + 1) * 128)
            kz = (mk_ref[:, hs].astype(F32) * mzeta_ref[h]).astype(BF16)
            r_ref[h] = lax.dot_general(kz, mv_ref[:, hs], TN, preferred_element_type=F32)

    for h in range(RET_HEADS):
        hs = slice(h * 128, (h + 1) * 128)
        q = rq_ref[:, hs]
        k = rk_ref[:, hs]
        v = rv_ref[:, hs]
        s = lax.dot_general(q, k, NT, preferred_element_type=F32) * dmat_ref[h]
        o = jnp.dot(s.astype(BF16), v, preferred_element_type=F32)
        r = r_ref[h]
        qx = (q.astype(F32) * xi_ref[h]).astype(BF16)
        o = o + jnp.dot(qx, r.astype(BF16), preferred_element_type=F32)
        kz = (k.astype(F32) * zeta_ref[h]).astype(BF16)
        u = lax.dot_general(kz, v, TN, preferred_element_type=F32)
        r_ref[h] = r * gsc_ref[h] + u
        mu = jnp.mean(o, axis=-1, keepdims=True)
        d = o - mu
        var = jnp.mean(d * d, axis=-1, keepdims=True)
        on = d * lax.rsqrt(var + EPS)
        g = rg_ref[:, hs]
        out_ref[:, hs] = (g * jax.nn.sigmoid(g) * on).astype(BF16)


def _retention(rq, rk, rv, rg, mk, mv, batch, seq):
    n = rq.shape[0]
    nb = seq // RET_BLOCK
    lg = jnp.log(1.0 - 2.0 ** (-5.0 - jnp.arange(RET_HEADS, dtype=F32)))
    i = jnp.arange(RET_BLOCK, dtype=F32)
    ci = jnp.arange(RET_BLOCK) // CHUNK
    vis = (ci[None, :] <= ci[:, None])
    dmat = jnp.where(vis[None], jnp.exp(lg[:, None, None] * jnp.abs(i[:, None] - i[None, :])), 0.0)
    xi = jnp.broadcast_to(jnp.exp(lg[:, None] * (i + 1.0)[None, :])[:, :, None], (RET_HEADS, RET_BLOCK, 128))
    zeta = jnp.broadcast_to(jnp.exp(lg[:, None] * (RET_BLOCK - 1.0 - i)[None, :])[:, :, None],
                            (RET_HEADS, RET_BLOCK, 128))
    im = jnp.arange(N_META, dtype=F32)
    mzeta = jnp.broadcast_to(jnp.exp(lg[:, None] * (N_META - 1.0 - im)[None, :])[:, :, None],
                             (RET_HEADS, N_META, 128))
    gsc = jnp.broadcast_to(jnp.exp(lg * RET_BLOCK)[:, None, None], (RET_HEADS, 1, 128))
    row = lambda b, s: (b * nb + s, 0)
    c2 = lambda b, s: (0, 0)
    c3 = lambda b, s: (0, 0, 0)
    return pl.pallas_call(
        _ret_body,
        grid=(batch, nb),
        in_specs=[pl.BlockSpec((RET_BLOCK, 512), row), pl.BlockSpec((RET_BLOCK, 512), row),
                  pl.BlockSpec((RET_BLOCK, 512), row), pl.BlockSpec((RET_BLOCK, 512), row),
                  pl.BlockSpec((N_META, 512), c2), pl.BlockSpec((N_META, 512), c2),
                  pl.BlockSpec((RET_HEADS, RET_BLOCK, RET_BLOCK), c3),
                  pl.BlockSpec((RET_HEADS, RET_BLOCK, 128), c3),
                  pl.BlockSpec((RET_HEADS, RET_BLOCK, 128), c3),
                  pl.BlockSpec((RET_HEADS, N_META, 128), c3),
                  pl.BlockSpec((RET_HEADS, 1, 128), c3)],
        out_specs=pl.BlockSpec((RET_BLOCK, 512), row),
        out_shape=jax.ShapeDtypeStruct((n, 512), BF16),
        scratch_shapes=[pltpu.VMEM((RET_HEADS, 128, 128), F32)],
        compiler_params=_params(("arbitrary", "arbitrary")),
        name="retention",
    )(rq, rk, rv, rg, mk, mv, dmat, xi, zeta, mzeta, gsc)


def _aligned(off):
    return off if isinstance(off, int) else pl.multiple_of(off, LANES)


def _dsa_body(qa_ref, qi_ref, misc_ref, k_ref, v_ref, ki_ref, wuv_ref, out_ref,
              isc_ref, bias_ref, s_ref, m_ref, l_ref, acc_ref, *, topk):
    qb = pl.program_id(1)
    tq = DSA_BLOCK
    nh = DSA_HEADS

    wt = misc_ref[...].T
    qi = qi_ref[...].reshape(IDX_HEADS * tq, IDX_DIM)

    def isc_tile(koff, width, allowed):
        kt = ki_ref[pl.ds(koff, width), :]
        z = lax.dot_general(kt, qi, NT, preferred_element_type=F32)
        isc = None
        for h in range(IDX_HEADS):
            term = jnp.maximum(z[:, h * tq:(h + 1) * tq], 0.0) * wt[64 + h:65 + h, :]
            isc = term if isc is None else isc + term
        if allowed is not None:
            isc = jnp.where(allowed, isc, -jnp.inf)
        isc_ref[pl.ds(koff, width), :] = isc

    isc_tile(0, LANES, lax.broadcasted_iota(I32, (LANES, tq), 0) >= LANES - N_META)

    def body_a(t, c):
        isc_tile(pl.multiple_of(LANES + t * tq, LANES), tq, None)
        return c

    lax.fori_loop(0, qb, body_a, 0)
    keyc = lax.broadcasted_iota(I32, (tq, tq), 0) // CHUNK
    qryc = lax.broadcasted_iota(I32, (tq, tq), 1) // CHUNK
    diag_off = pl.multiple_of(LANES + qb * tq, LANES)
    isc_tile(diag_off, tq, keyc <= qryc)

    kf = float(topk)
    sub = 8

    def key_to_f32(key):
        return lax.bitcast_convert_type(jnp.where(key < 0, key ^ jnp.int32(0x7FFFFFFF), key), F32)

    def count(pred):
        def slab(off, a):
            hit = jnp.where(pred(isc_ref[pl.ds(off, LANES), :], off), 1.0, 0.0)
            return a + jnp.sum(hit.reshape(LANES // sub, sub, tq), axis=0)

        def body(t, a):
            off = pl.multiple_of(LANES + t * (2 * LANES), LANES)
            return slab(pl.multiple_of(off + LANES, LANES), slab(off, a))

        a = lax.fori_loop(0, qb + 1, body, slab(0, jnp.zeros((sub, tq), F32)))
        return jnp.sum(a, axis=0, keepdims=True)

    def count_ge(cand):
        return count(lambda x, off: x >= cand)

    lowest = jnp.full((1, tq), -3.0e38, F32)
    n_real = count_ge(lowest)
    c_pos = count_ge(jnp.zeros((1, tq), F32))
    pos = c_pos >= kf
    base0 = jnp.where(pos, jnp.int32(0), jnp.int32(INT_MIN))
    cnt0 = jnp.where(n_real <= kf, kf, jnp.where(pos, c_pos, n_real))

    def unfinished(cnt):
        return jnp.max(jnp.where(cnt != kf, 1.0, 0.0))

    def try_bit(bit, base, cnt):
        cand = base | lax.shift_left(jnp.int32(1), bit)
        c = count_ge(key_to_f32(cand))
        ok = c >= kf
        return jnp.where(ok, cand, base), jnp.where(ok, c, cnt)

    base, cnt = base0, cnt0
    for bit in range(30, DSA_BITS_PER_CHECK * 7 - 1, -1):
        base, cnt = try_bit(bit, base, cnt)

    zero_tie = jnp.logical_and(pos, count(lambda x, off: x > 0.0) < kf)

    def group_cond(state):
        g, _, cnt = state
        return jnp.logical_and(g >= 0, unfinished(jnp.where(zero_tie, kf, cnt)) > 0.0)

    def group_body(state):
        g, base, cnt = state
        for k in range(DSA_BITS_PER_CHECK - 1, -1, -1):
            base, cnt = try_bit(g * DSA_BITS_PER_CHECK + k, base, cnt)
        return g - 1, base, cnt

    _, base, cnt = lax.while_loop(group_cond, group_body, (jnp.int32(6), base, cnt))

    def refine_cond(state):
        it, _, _, _, go = state
        return jnp.logical_and(it < 40, go > 0.0)

    def refine_body(state):
        it, lo, hi, cnt, _ = state
        mid = lo + 0.5 * (hi - lo)
        c = count_ge(mid)
        ok = c >= kf
        cnt = jnp.where(ok, c, cnt)
        moving = jnp.logical_and(jnp.logical_and(mid > lo, mid < hi),
                                 jnp.logical_and(cnt != kf, jnp.logical_not(zero_tie)))
        return it + 1, jnp.where(ok, mid, lo), jnp.where(ok, hi, mid), cnt, jnp.max(jnp.where(moving, 1.0, 0.0))

    _, thr, _, cnt, _ = lax.while_loop(
        refine_cond, refine_body,
        (jnp.int32(0), key_to_f32(base), key_to_f32(base + 1), cnt, unfinished(jnp.where(zero_tie, kf, cnt))))
    thr = jnp.where(n_real <= kf, lowest, thr)

    key_id = lax.broadcasted_iota(I32, (LANES, tq), 0)
    index_bits = max(1, (isc_ref.shape[0] - 1).bit_length())

    def tie_bound():
        need = kf - count(lambda x, off: x > thr)

        def ties_below(bound):
            return count(lambda x, off: jnp.logical_and(x == thr, key_id < bound - off))

        def bit_step(i, last):
            cand = last | lax.shift_left(jnp.int32(1), index_bits - 1 - i)
            return jnp.where(ties_below(cand) < need, cand, last)

        return lax.fori_loop(0, index_bits, bit_step, jnp.zeros((1, tq), I32)) + 1

    everything = jnp.full((1, tq), isc_ref.shape[0], I32)
    tie_end = lax.cond(unfinished(cnt) > 0.0, tie_bound, lambda: everything)
    tie_end = jnp.where(cnt == kf, everything, tie_end)

    def mask_tile(off, width):
        x = isc_ref[pl.ds(off, width), :]
        ids = lax.broadcasted_iota(I32, (width, tq), 0)
        keep = jnp.logical_or(x > thr, jnp.logical_and(x == thr, ids < tie_end - off))
        bias_ref[:, pl.ds(off, width)] = jnp.where(keep, 0.0, NEG).T

    def body_m(t, c):
        mask_tile(pl.multiple_of(LANES + t * tq, LANES), tq)
        return c

    mask_tile(0, LANES)
    lax.fori_loop(0, qb + 1, body_m, 0)

    qa = qa_ref[...].reshape(nh * tq, DSA_DIM)

    def lane_fold(x, op):
        out = x[:, :LANES]
        for c in range(1, x.shape[1] // LANES):
            out = op(out, x[:, c * LANES:(c + 1) * LANES])
        return out

    def score_tile(koff, width):
        s = lax.dot_general(qa, k_ref[pl.ds(koff, width), :], NT, preferred_element_type=F32)
        s = (s.reshape(nh, tq, width) + bias_ref[:, pl.ds(koff, width)][None]).reshape(nh * tq, width)
        s_ref[:, pl.ds(koff, width)] = s
        m_ref[...] = jnp.maximum(m_ref[...], lane_fold(s, jnp.maximum))

    def prob_tile(koff, width):
        p = jnp.exp(s_ref[:, pl.ds(koff, width)] - jnp.concatenate([m_ref[...]] * (width // LANES), axis=1))
        l_ref[...] += lane_fold(p, jnp.add)
        acc_ref[...] += jnp.dot(p.astype(BF16), v_ref[pl.ds(koff, width), :], preferred_element_type=F32)

    def over_tiles(tile_fn):
        tile_fn(0, LANES)

        def body(t, c):
            tile_fn(pl.multiple_of(LANES + t * tq, LANES), tq)
            return c

        lax.fori_loop(0, qb + 1, body, 0)

    m_ref[...] = jnp.full(m_ref.shape, NEG, F32)
    over_tiles(score_tile)
    m_ref[...] = jnp.broadcast_to(jnp.max(m_ref[...], axis=1, keepdims=True), m_ref.shape)
    l_ref[...] = jnp.zeros(l_ref.shape, F32)
    acc_ref[...] = jnp.zeros(acc_ref.shape, F32)
    over_tiles(prob_tile)

    l = jnp.sum(l_ref[...], axis=1, keepdims=True)
    o = (acc_ref[...] / l).astype(BF16).reshape(nh, tq, DSA_DIM)
    for h in range(nh):
        out_ref[:, h * DSA_OUT_DIM:(h + 1) * DSA_OUT_DIM] = jnp.dot(
            o[h], wuv_ref[h], preferred_element_type=F32).astype(BF16)


def _dsa(dq, iq, misc, kpad, vpad, kipad, wuv, batch, seq, topk):
    n = misc.shape[0]
    nq = seq // DSA_BLOCK
    tp = kpad.shape[1]
    head = lambda b, q: (0, b * nq + q, 0)
    row = lambda b, q: (b * nq + q, 0)
    per_b = lambda b, q: (b, 0, 0)
    return pl.pallas_call(
        functools.partial(_dsa_body, topk=topk),
        grid=(batch, nq),
        in_specs=[pl.BlockSpec((DSA_HEADS, DSA_BLOCK, DSA_DIM), head),
                  pl.BlockSpec((IDX_HEADS, DSA_BLOCK, IDX_DIM), head),
                  pl.BlockSpec((DSA_BLOCK, 128), row),
                  pl.BlockSpec((None, tp, DSA_DIM), per_b),
                  pl.BlockSpec((None, tp, DSA_DIM), per_b),
                  pl.BlockSpec((None, tp, IDX_DIM), per_b),
                  pl.BlockSpec((DSA_HEADS, DSA_DIM, DSA_OUT_DIM), lambda b, q: (0, 0, 0))],
        out_specs=pl.BlockSpec((DSA_BLOCK, 512), row),
        out_shape=jax.ShapeDtypeStruct((n, 512), BF16),
        scratch_shapes=[pltpu.VMEM((tp, DSA_BLOCK), F32),
                        pltpu.VMEM((DSA_BLOCK, tp), F32),
                        pltpu.VMEM((DSA_HEADS * DSA_BLOCK, tp), F32),
                        pltpu.VMEM((DSA_HEADS * DSA_BLOCK, LANES), F32),
                        pltpu.VMEM((DSA_HEADS * DSA_BLOCK, LANES), F32),
                        pltpu.VMEM((DSA_HEADS * DSA_BLOCK, DSA_DIM), F32)],
        compiler_params=_params(("arbitrary", "arbitrary")),
        name="dsa",
    )(dq, iq, misc, kpad, vpad, kipad, wuv)


def _post_body(ret_ref, att_ref, x_ref, w_ref, g_ref, h1_ref, hnt_ref):
    y = jnp.dot(ret_ref[...], w_ref[0:512, :], preferred_element_type=F32)
    y = y + jnp.dot(att_ref[...], w_ref[512:1024, :], preferred_element_type=F32)
    h1 = x_ref[...] + y
    h1_ref[...] = h1
    hnt_ref[...] = _rms(h1, g_ref[...]).T.astype(BF16)


def _post(ret, att, x2d, w_out, g, tm):
    n = x2d.shape[0]
    row = lambda i: (i, 0)
    return pl.pallas_call(
        _post_body,
        grid=(n // tm,),
        in_specs=[pl.BlockSpec((tm, 512), row), pl.BlockSpec((tm, 512), row),
                  pl.BlockSpec((tm, D_MODEL), row),
                  pl.BlockSpec((D_MODEL, D_MODEL), lambda i: (0, 0)),
                  pl.BlockSpec((1, D_MODEL), lambda i: (0, 0))],
        out_specs=(pl.BlockSpec((tm, D_MODEL), row), pl.BlockSpec((D_MODEL, tm), lambda i: (0, i))),
        out_shape=(jax.ShapeDtypeStruct((n, D_MODEL), F32), jax.ShapeDtypeStruct((D_MODEL, n), BF16)),
        compiler_params=_params(("arbitrary",)),
        name="post",
    )(ret, att, x2d, w_out, g)


def _top_desc(s, k, dst_ref, ts, with_rank=False):
    cur = s
    rank = jnp.full(s.shape, float(k), F32) if with_rank else None
    for r in range(k):
        m = jnp.max(cur, axis=0, keepdims=True)
        dst_ref[r:r + 1, ts] = m
        if with_rank or r + 1 < k:
            hit = cur == m
            if with_rank:
                rank = jnp.where(hit, float(r), rank)
            cur = jnp.where(hit, -jnp.inf, cur)
    return rank


def _psel_body(hnt_ref, wq_ref, sk_ref, cnt_ref, e1_ref, r2_ref, e2_ref, s_ref, v1_ref, v2_ref, vc_ref):
    hnt = hnt_ref[...]
    for h in range(PEER_HEADS):
        q = jnp.dot(wq_ref[h * 256:(h + 1) * 256, :], hnt, preferred_element_type=F32)
        s_ref[0] = jnp.dot(sk_ref[h, 0], q[:128].astype(BF16), preferred_element_type=F32)
        s_ref[1] = jnp.dot(sk_ref[h, 1], q[128:].astype(BF16), preferred_element_type=F32)
        for tc in range(hnt.shape[1] // LANES):
            ts = slice(tc * LANES, (tc + 1) * LANES)
            s1 = s_ref[0, :, ts]
            s2 = s_ref[1, :, ts]
            _top_desc(s1, PEER_TOPK, v1_ref, ts)
            rank2 = _top_desc(s2, PEER_TOPK, v2_ref, ts, with_rank=True)
            v1 = v1_ref[:, ts]
            v2 = v2_ref[:, ts]
            cand = jnp.concatenate([v2 + v1[0:1]] + [v2[0:8] + v1[a:a + 1] for a in range(1, 8)]
                                   + [v1[8:16] + v2[0:1]], axis=0)
            _top_desc(cand, PEER_TOPK, vc_ref, ts)
            tau = vc_ref[PEER_TOPK - 1:PEER_TOPK, ts]
            top = vc_ref[0:1, ts]
            zsum = jnp.sum(jnp.where(cand >= tau, jnp.exp(cand - top), 0.0), axis=0, keepdims=True)
            cnt = jnp.zeros(s1.shape, F32)
            for b in range(PEER_TOPK):
                cnt = cnt + jnp.where(s1 + v2[b:b + 1] >= tau, 1.0, 0.0)
            cnt_ref[h, :, ts] = cnt
            e1_ref[h, :, ts] = jnp.exp(s1 - v1[0:1]) / zsum
            r2_ref[h, :, ts] = pltpu.bitcast(rank2.astype(BF16), I32)
            e2_ref[h, :, ts] = pltpu.bitcast(jnp.exp(s2 - v2[0:1]).astype(BF16), I32)


def _psel(hnt, wq_t, sk, tn):
    n = hnt.shape[1]
    spec = pl.BlockSpec((PEER_HEADS, PEER_NKEYS, tn), lambda i: (0, 0, i))
    shp32 = jax.ShapeDtypeStruct((PEER_HEADS, PEER_NKEYS, n), F32)
    shp16 = jax.ShapeDtypeStruct((PEER_HEADS, PEER_NKEYS // 2, n), I32)
    spec16 = pl.BlockSpec((PEER_HEADS, PEER_NKEYS // 2, tn), lambda i: (0, 0, i))
    return pl.pallas_call(
        _psel_body,
        grid=(n // tn,),
        in_specs=[pl.BlockSpec((D_MODEL, tn), lambda i: (0, i)),
                  pl.BlockSpec((PEER_HEADS * 256, D_MODEL), lambda i: (0, 0)),
                  pl.BlockSpec((PEER_HEADS, 2, PEER_NKEYS, 128), lambda i: (0, 0, 0, 0))],
        out_specs=(spec, spec, spec16, spec16),
        out_shape=(shp32, shp32, shp16, shp16),
        scratch_shapes=[pltpu.VMEM((2, PEER_NKEYS, tn), F32)] + [pltpu.VMEM((PEER_TOPK, tn), F32)] * 3,
        compiler_params=_params(("arbitrary",)),
        name="psel",
    )(hnt, wq_t, sk)


def _pdense_body(*refs, eb, nchunk, nblk):
    hnt_ref, u_ref, vt_ref = refs[:3]
    cnt_refs = refs[3:3 + nchunk]
    e1_refs = refs[3 + nchunk:3 + 2 * nchunk]
    r2_ref, e2_ref, h1_ref, g_ref, out_ref, acc_ref, a_ref, coef_ref = refs[3 + 2 * nchunk:]
    s = pl.program_id(1)
    nsub = eb // PEER_NKEYS

    def row_bf16(ref, h, i1):
        return jnp.broadcast_to(ref[h, pl.ds(i1, 1), :], (PEER_NKEYS, LANES)).astype(BF16)

    def stage_a(a_ref):
        for half in range(eb // PDENSE_ROWS):
            rows = slice(half * PDENSE_ROWS, (half + 1) * PDENSE_ROWS)
            a_ref[rows, :] = jnp.dot(u_ref[rows, :], hnt_ref[...], preferred_element_type=F32)

    def stage_b(blk, a_ref, coef_ref):
        for cp in range(0, nsub, PDENSE_GROUP):
            cs = [cp + k for k in range(PDENSE_GROUP)]
            for tc in range(nchunk):
                ts = slice(tc * LANES, (tc + 1) * LANES)
                gates = [None] * PDENSE_GROUP
                for h in range(PEER_HEADS):
                    r2 = pltpu.bitcast(r2_ref[h, :, ts], BF16)
                    e2 = pltpu.bitcast(e2_ref[h, :, ts], BF16)
                    for k in range(PDENSE_GROUP):
                        cnt = row_bf16(cnt_refs[tc], h, blk * nsub + cs[k])
                        e1 = row_bf16(e1_refs[tc], h, blk * nsub + cs[k])
                        term = jnp.where(r2 < cnt, e2, jnp.zeros((), BF16)) * e1
                        gates[k] = term if gates[k] is None else gates[k] + term
                for k in range(PDENSE_GROUP):
                    rows = slice(cs[k] * PEER_NKEYS, (cs[k] + 1) * PEER_NKEYS)
                    at = a_ref[rows, ts]
                    gelu = (0.5 * at) * (1.0 + lax.erf(at * (0.5 ** 0.5)))
                    coef_ref[rows, ts] = gelu.astype(BF16) * gates[k]

    def stage_c(coef_ref):
        acc_ref[...] += jnp.dot(vt_ref[...], coef_ref[...], preferred_element_type=F32)

    @pl.when(s == 0)
    def _():
        acc_ref[...] = jnp.zeros(acc_ref.shape, F32)

    stage_a(a_ref)
    stage_b(s, a_ref, coef_ref)
    stage_c(coef_ref)

    @pl.when(s == nblk - 1)
    def _():
        out_ref[...] = _rms(h1_ref[...] + acc_ref[...].T, g_ref[...])


def _pdense(hnt, u, vt, cnt, e1, r2, e2, h1, g, tn, eb):
    n = hnt.shape[1]
    nblk = u.shape[0] // eb
    assert nblk >= 2 and vt.shape == (nblk, D_MODEL, eb)
    nchunk = tn // LANES
    chunk = [pl.BlockSpec((PEER_HEADS, PEER_NKEYS, LANES), lambda i, s, c=c: (0, 0, i * nchunk + c))
             for c in range(nchunk)]
    sel16 = pl.BlockSpec((PEER_HEADS, PEER_NKEYS // 2, tn), lambda i, s: (0, 0, i))
    return pl.pallas_call(
        functools.partial(_pdense_body, eb=eb, nchunk=nchunk, nblk=nblk),
        grid=(n // tn, nblk),
        in_specs=[pl.BlockSpec((D_MODEL, tn), lambda i, s: (0, i)),
                  pl.BlockSpec((eb, D_MODEL), lambda i, s: (s, 0)),
                  pl.BlockSpec((None, D_MODEL, eb), lambda i, s: (s, 0, 0))]
                 + chunk + chunk
                 + [sel16, sel16,
                    pl.BlockSpec((tn, D_MODEL), lambda i, s: (i, 0)),
                    pl.BlockSpec((1, D_MODEL), lambda i, s: (0, 0))],
        out_specs=pl.BlockSpec((tn, D_MODEL), lambda i, s: (i, 0)),
        out_shape=jax.ShapeDtypeStruct((n, D_MODEL), F32),
        scratch_shapes=[pltpu.VMEM((D_MODEL, tn), F32),
                        pltpu.VMEM((eb, tn), F32), pltpu.VMEM((eb, tn), BF16)],
        compiler_params=_params(("arbitrary", "arbitrary")),
        name="pdense",
    )(hnt, u, vt, *([cnt] * nchunk), *([e1] * nchunk), r2, e2, h1, g)


def kernel(x, meta_tokens, norm_mix, w_in, w_uv, w_out, norm_ffn, peer_wq, peer_subkeys,
           peer_u, peer_v, norm_final):
    batch, seq, d = x.shape
    assert d == D_MODEL and norm_mix.shape[0] == 1 and seq % DSA_BLOCK == 0
    n = batch * seq
    topk = min(DSA_TOPK_MAX, seq // 4)
    tm = 512 if seq % 512 == 0 else 256

    w = jnp.pad(w_in[0], ((0, 0), (0, IN_WIDTH_PAD - IN_WIDTH))).astype(BF16)
    g_mix = norm_mix[0][None, :]
    x2d = x.reshape(n, d)
    pos_f = jnp.arange(seq, dtype=F32) + float(N_META)
    pos_m = jnp.arange(N_META, dtype=F32)

    rq, rk, rv, rg, dq, dk, dv, iq, misc = _proj(x2d, g_mix, w, _rope_tables(pos_f), tm, seq // tm)
    _, mk, mv, _, _, mdk, mdv, _, mmisc = _proj(meta_tokens, g_mix, w, _rope_tables(pos_m), N_META, 1)

    ret = _retention(rq, rk, rv, rg, mk, mv, batch, seq)

    def keys(frames, meta):
        width = frames.shape[-1]
        lead = jnp.concatenate([jnp.zeros((LANES - N_META, width), frames.dtype), meta], axis=0)
        lead = jnp.broadcast_to(lead[None], (batch, LANES, width))
        return jnp.concatenate([lead, frames.reshape(batch, seq, width)], axis=1)

    kpad = keys(dk, mdk)
    vpad = keys(dv, mdv)
    kipad = keys(misc[:, :IDX_DIM].astype(BF16), mmisc[:, :IDX_DIM].astype(BF16))
    att = _dsa(dq, iq, misc, kpad, vpad, kipad, w_uv[0].astype(BF16), batch, seq, topk)

    h1, hnt = _post(ret, att, x2d, w_out[0].astype(BF16), norm_ffn[0][None, :], tm)

    wq_t = peer_wq[0].reshape(d, PEER_HEADS * 256).T.astype(BF16)
    cnt, e1, r2, e2 = _psel(hnt, wq_t, peer_subkeys[0].astype(BF16), tm)
    vt = peer_v[0].reshape(-1, PDENSE_EXPERTS, d).transpose(0, 2, 1).astype(BF16)
    out = _pdense(hnt, peer_u[0].astype(BF16), vt, cnt, e1, r2, e2, h1,
                  norm_final[None, :], tm, PDENSE_EXPERTS)
    return out.reshape(batch, seq, d)
```

```python
import functools

import jax
import jax.numpy as jnp
from jax import lax
from jax.experimental import pallas as pl
from jax.experimental.pallas import tpu as pltpu

F32 = jnp.float32
BF16 = jnp.bfloat16
I32 = jnp.int32

D_MODEL = 1024
N_META = 16
CHUNK = 64
RET_HEADS = 4
RET_DIM = 128
DSA_HEADS = 8
DSA_DIM = 128
DSA_OUT_DIM = 64
IDX_HEADS = 8
IDX_DIM = 64
DSA_TOPK_MAX = 256
ROPE_THETA = 10000.0
PEER_HEADS = 8
PEER_NKEYS = 128
PEER_TOPK = 16
EPS = 1e-6
NEG = -1e30
INT_MIN = -(2 ** 31)

IN_WIDTH = 3912
IN_WIDTH_PAD = 3968
LANES = 128
RET_BLOCK = 256
PDENSE_EXPERTS = 1024
PDENSE_ROWS = 512
PDENSE_GROUP = 2
DSA_BLOCK = 256
DSA_BITS_PER_CHECK = 4
VMEM_LIMIT = 56 * 1024 * 1024

NT = (((1,), (1,)), ((), ()))
TN = (((0,), (0,)), ((), ()))


def _rms(x, g):
    return x * lax.rsqrt(jnp.mean(x * x, axis=-1, keepdims=True) + EPS) * g


def _params(sem):
    return pltpu.CompilerParams(dimension_semantics=sem, vmem_limit_bytes=VMEM_LIMIT)


def _proj_body(x_ref, g_ref, w_ref, tab_ref, rq_ref, rk_ref, rv_ref, rg_ref, dq_ref, dk_ref,
               dv_ref, iq_ref, misc_ref):
    xn = _rms(x_ref[...], g_ref[...]).astype(BF16)
    cos_a, sin_a, cos_b, sin_b1, sin_b2, cos_m, sin_m1, sin_m2 = (tab_ref[i] for i in range(8))

    def mm(lo, n):
        return jnp.dot(xn, w_ref[:, lo:lo + n], preferred_element_type=F32)

    def rope_a(y):
        return y * cos_a + pltpu.roll(y, 64, 1) * sin_a

    def rope_b(y, c, s1, s2):
        return y * c + pltpu.roll(y, 96, 1) * s1 + pltpu.roll(y, 32, 1) * s2

    y = mm(0, 512)
    for h in range(RET_HEADS):
        rq_ref[:, h * 128:(h + 1) * 128] = rope_a(y[:, h * 128:(h + 1) * 128]).astype(BF16)
    y = mm(512, 512)
    for h in range(RET_HEADS):
        rk_ref[:, h * 128:(h + 1) * 128] = (
            rope_a(y[:, h * 128:(h + 1) * 128]) * (RET_DIM ** -0.5)).astype(BF16)
    rv_ref[...] = mm(1024, 512).astype(BF16)
    rg_ref[...] = mm(1536, 512)
    y = mm(2048, 1024)
    for h in range(DSA_HEADS):
        dq_ref[h] = (rope_a(y[:, h * 128:(h + 1) * 128]) * (DSA_DIM ** -0.5)).astype(BF16)
    dk_ref[...] = rope_a(mm(3072, 128)).astype(BF16)
    dv_ref[...] = mm(3200, 128).astype(BF16)
    y = mm(3328, 512)
    for p in range(IDX_HEADS // 2):
        o = (rope_b(y[:, p * 128:(p + 1) * 128], cos_b, sin_b1, sin_b2) * (IDX_DIM ** -0.5)).astype(BF16)
        iq_ref[2 * p] = o[:, :64]
        iq_ref[2 * p + 1] = o[:, 64:]
    misc_ref[...] = rope_b(mm(3840, 128), cos_m, sin_m1, sin_m2)


def _proj(x2d, g, w, tab, tm, tab_blocks):
    n = x2d.shape[0]
    row = lambda i: (i, 0)
    head = lambda i: (0, i, 0)
    out_shape = (
        jax.ShapeDtypeStruct((n, 512), BF16), jax.ShapeDtypeStruct((n, 512), BF16),
        jax.ShapeDtypeStruct((n, 512), BF16), jax.ShapeDtypeStruct((n, 512), F32),
        jax.ShapeDtypeStruct((DSA_HEADS, n, 128), BF16), jax.ShapeDtypeStruct((n, 128), BF16),
        jax.ShapeDtypeStruct((n, 128), BF16), jax.ShapeDtypeStruct((IDX_HEADS, n, 64), BF16),
        jax.ShapeDtypeStruct((n, 128), F32))
    out_specs = (
        pl.BlockSpec((tm, 512), row), pl.BlockSpec((tm, 512), row), pl.BlockSpec((tm, 512), row),
        pl.BlockSpec((tm, 512), row), pl.BlockSpec((DSA_HEADS, tm, 128), head),
        pl.BlockSpec((tm, 128), row), pl.BlockSpec((tm, 128), row),
        pl.BlockSpec((IDX_HEADS, tm, 64), head), pl.BlockSpec((tm, 128), row))
    return pl.pallas_call(
        _proj_body,
        grid=(n // tm,),
        in_specs=[pl.BlockSpec((tm, D_MODEL), row),
                  pl.BlockSpec((1, D_MODEL), lambda i: (0, 0)),
                  pl.BlockSpec((D_MODEL, IN_WIDTH_PAD), lambda i: (0, 0)),
                  pl.BlockSpec((8, tm, 128), lambda i: (0, i % tab_blocks, 0))],
        out_specs=out_specs,
        out_shape=out_shape,
        compiler_params=_params(("arbitrary",)),
        name="proj",
    )(x2d, g, w, tab)


def _rope_tables(pos):
    p = pos.shape[0]
    inv = ROPE_THETA ** (-jnp.arange(0, 64, dtype=F32) * 2.0 / 128)
    ang = pos[:, None] * inv[None, :]
    c, s = jnp.cos(ang), jnp.sin(ang)
    cos_a = jnp.concatenate([c, c], -1)
    sin_a = jnp.concatenate([-s, s], -1)
    inv = ROPE_THETA ** (-jnp.arange(0, 32, dtype=F32) * 2.0 / 64)
    ang = pos[:, None] * inv[None, :]
    c, s = jnp.cos(ang), jnp.sin(ang)
    z = jnp.zeros_like(s)
    z64 = jnp.zeros((p, 64), F32)
    cos_b = jnp.concatenate([c, c, c, c], -1)
    sin_b1 = jnp.concatenate([-s, z, -s, z], -1)
    sin_b2 = jnp.concatenate([z, s, z, s], -1)
    cos_m = jnp.concatenate([c, c, jnp.full((p, 64), IDX_HEADS ** -0.5, F32)], -1)
    sin_m1 = jnp.concatenate([-s, z, z64], -1)
    sin_m2 = jnp.concatenate([z, s, z64], -1)
    return jnp.stack([cos_a, sin_a, cos_b, sin_b1, sin_b2, cos_m, sin_m1, sin_m2])


def _ret_body(rq_ref, rk_ref, rv_ref, rg_ref, mk_ref, mv_ref, dmat_ref, xi_ref, zeta_ref,
              mzeta_ref, gsc_ref, out_ref, r_ref):
    @pl.when(pl.program_id(1) == 0)
    def _():
        for h in range(RET_HEADS):
            hs = slice(h * 128, (h + 1) * 128)
            kz = (mk_ref[:, hs].astype(F32) * mzeta_ref[h]).astype(BF16)
            r_ref[h] = lax.dot_general(kz, mv_ref[:, hs], TN, preferred_element_type=F32)

    for h in range(RET_HEADS):
        hs = slice(h * 128, (h + 1) * 128)
        q = rq_ref[:, hs]
        k = rk_ref[:, hs]
        v = rv_ref[:, hs]
        s = lax.dot_general(q, k, NT, preferred_element_type=F32) * dmat_ref[h]
        o = jnp.dot(s.astype(BF16), v, preferred_element_type=F32)
        r = r_ref[h]
        qx = (q.astype(F32) * xi_ref[h]).astype(BF16)
        o = o + jnp.dot(qx, r.astype(BF16), preferred_element_type=F32)
        kz = (k.astype(F32) * zeta_ref[h]).astype(BF16)
        u = lax.dot_general(kz, v, TN, preferred_element_type=F32)
        r_ref[h] = r * gsc_ref[h] + u
        mu = jnp.mean(o, axis=-1, keepdims=True)
        d = o - mu
        var = jnp.mean(d * d, axis=-1, keepdims=True)
        on = d * lax.rsqrt(var + EPS)
        g = rg_ref[:, hs]
        out_ref[:, hs] = (g * jax.nn.sigmoid(g) * on).astype(BF16)


def _retention(rq, rk, rv, rg, mk, mv, batch, seq):
    n = rq.shape[0]
    nb = seq // RET_BLOCK
    lg = jnp.log(1.0 - 2.0 ** (-5.0 - jnp.arange(RET_HEADS, dtype=F32)))
    i = jnp.arange(RET_BLOCK, dtype=F32)
    ci = jnp.arange(RET_BLOCK) // CHUNK
    vis = (ci[None, :] <= ci[:, None])
    dmat = jnp.where(vis[None], jnp.exp(lg[:, None, None] * jnp.abs(i[:, None] - i[None, :])), 0.0)
    xi = jnp.broadcast_to(jnp.exp(lg[:, None] * (i + 1.0)[None, :])[:, :, None], (RET_HEADS, RET_BLOCK, 128))
    zeta = jnp.broadcast_to(jnp.exp(lg[:, None] * (RET_BLOCK - 1.0 - i)[None, :])[:, :, None],
                            (RET_HEADS, RET_BLOCK, 128))
    im = jnp.arange(N_META, dtype=F32)
    mzeta = jnp.broadcast_to(jnp.exp(lg[:, None] * (N_META - 1.0 - im)[None, :])[:, :, None],
                             (RET_HEADS, N_META, 128))
    gsc = jnp.broadcast_to(jnp.exp(lg * RET_BLOCK)[:, None, None], (RET_HEADS, 1, 128))
    row = lambda b, s: (b * nb + s, 0)
    c2 = lambda b, s: (0, 0)
    c3 = lambda b, s: (0, 0, 0)
    return pl.pallas_call(
        _ret_body,
        grid=(batch, nb),
        in_specs=[pl.BlockSpec((RET_BLOCK, 512), row), pl.BlockSpec((RET_BLOCK, 512), row),
                  pl.BlockSpec((RET_BLOCK, 512), row), pl.BlockSpec((RET_BLOCK, 512), row),
                  pl.BlockSpec((N_META, 512), c2), pl.BlockSpec((N_META, 512), c2),
                  pl.BlockSpec((RET_HEADS, RET_BLOCK, RET_BLOCK), c3),
                  pl.BlockSpec((RET_HEADS, RET_BLOCK, 128), c3),
                  pl.BlockSpec((RET_HEADS, RET_BLOCK, 128), c3),
                  pl.BlockSpec((RET_HEADS, N_META, 128), c3),
                  pl.BlockSpec((RET_HEADS, 1, 128), c3)],
        out_specs=pl.BlockSpec((RET_BLOCK, 512), row),
        out_shape=jax.ShapeDtypeStruct((n, 512), BF16),
        scratch_shapes=[pltpu.VMEM((RET_HEADS, 128, 128), F32)],
        compiler_params=_params(("arbitrary", "arbitrary")),
        name="retention",
    )(rq, rk, rv, rg, mk, mv, dmat, xi, zeta, mzeta, gsc)


def _aligned(off):
    return off if isinstance(off, int) else pl.multiple_of(off, LANES)


def _dsa_body(qa_ref, qi_ref, misc_ref, k_ref, v_ref, ki_ref, wuv_ref, out_ref,
              isc_ref, bias_ref, m_ref, l_ref, acc_ref, *, topk):
    qb = pl.program_id(1)
    tq = DSA_BLOCK
    nh = DSA_HEADS

    wt = misc_ref[...].T
    qi = qi_ref[...].reshape(IDX_HEADS * tq, IDX_DIM)

    def isc_tile(koff, width, allowed):
        kt = ki_ref[pl.ds(koff, width), :]
        z = lax.dot_general(kt, qi, NT, preferred_element_type=F32)
        isc = None
        for h in range(IDX_HEADS):
            term = jnp.maximum(z[:, h * tq:(h + 1) * tq], 0.0) * wt[64 + h:65 + h, :]
            isc = term if isc is None else isc + term
        if allowed is not None:
            isc = jnp.where(allowed, isc, -jnp.inf)
        isc_ref[pl.ds(koff, width), :] = isc

    isc_tile(0, LANES, lax.broadcasted_iota(I32, (LANES, tq), 0) >= LANES - N_META)

    def body_a(t, c):
        isc_tile(pl.multiple_of(LANES + t * tq, LANES), tq, None)
        return c

    lax.fori_loop(0, qb, body_a, 0)
    keyc = lax.broadcasted_iota(I32, (tq, tq), 0) // CHUNK
    qryc = lax.broadcasted_iota(I32, (tq, tq), 1) // CHUNK
    diag_off = pl.multiple_of(LANES + qb * tq, LANES)
    isc_tile(diag_off, tq, keyc <= qryc)

    kf = float(topk)
    sub = 8

    def key_to_f32(key):
        return lax.bitcast_convert_type(jnp.where(key < 0, key ^ jnp.int32(0x7FFFFFFF), key), F32)

    def count(pred):
        def slab(off, a):
            hit = jnp.where(pred(isc_ref[pl.ds(off, LANES), :], off), 1.0, 0.0)
            return a + jnp.sum(hit.reshape(LANES // sub, sub, tq), axis=0)

        def body(t, a):
            off = pl.multiple_of(LANES + t * (2 * LANES), LANES)
            return slab(pl.multiple_of(off + LANES, LANES), slab(off, a))

        a = lax.fori_loop(0, qb + 1, body, slab(0, jnp.zeros((sub, tq), F32)))
        return jnp.sum(a, axis=0, keepdims=True)

    def count_ge(cand):
        return count(lambda x, off: x >= cand)

    lowest = jnp.full((1, tq), -3.0e38, F32)
    n_real = count_ge(lowest)
    c_pos = count_ge(jnp.zeros((1, tq), F32))
    pos = c_pos >= kf
    base0 = jnp.where(pos, jnp.int32(0), jnp.int32(INT_MIN))
    cnt0 = jnp.where(n_real <= kf, kf, jnp.where(pos, c_pos, n_real))

    def unfinished(cnt):
        return jnp.max(jnp.where(cnt != kf, 1.0, 0.0))

    def try_bit(bit, base, cnt):
        cand = base | lax.shift_left(jnp.int32(1), bit)
        c = count_ge(key_to_f32(cand))
        ok = c >= kf
        return jnp.where(ok, cand, base), jnp.where(ok, c, cnt)

    base, cnt = base0, cnt0
    for bit in range(30, DSA_BITS_PER_CHECK * 7 - 1, -1):
        base, cnt = try_bit(bit, base, cnt)

    zero_tie = jnp.logical_and(pos, count(lambda x, off: x > 0.0) < kf)

    def group_cond(state):
        g, _, cnt = state
        return jnp.logical_and(g >= 0, unfinished(jnp.where(zero_tie, kf, cnt)) > 0.0)

    def group_body(state):
        g, base, cnt = state
        for k in range(DSA_BITS_PER_CHECK - 1, -1, -1):
            base, cnt = try_bit(g * DSA_BITS_PER_CHECK + k, base, cnt)
        return g - 1, base, cnt

    _, base, cnt = lax.while_loop(group_cond, group_body, (jnp.int32(6), base, cnt))

    def refine_cond(state):
        it, _, _, _, go = state
        return jnp.logical_and(it < 40, go > 0.0)

    def refine_body(state):
        it, lo, hi, cnt, _ = state
        mid = lo + 0.5 * (hi - lo)
        c = count_ge(mid)
        ok = c >= kf
        cnt = jnp.where(ok, c, cnt)
        moving = jnp.logical_and(jnp.logical_and(mid > lo, mid < hi),
                                 jnp.logical_and(cnt != kf, jnp.logical_not(zero_tie)))
        return it + 1, jnp.where(ok, mid, lo), jnp.where(ok, hi, mid), cnt, jnp.max(jnp.where(moving, 1.0, 0.0))

    _, thr, _, cnt, _ = lax.while_loop(
        refine_cond, refine_body,
        (jnp.int32(0), key_to_f32(base), key_to_f32(base + 1), cnt, unfinished(jnp.where(zero_tie, kf, cnt))))
    thr = jnp.where(n_real <= kf, lowest, thr)

    key_id = lax.broadcasted_iota(I32, (LANES, tq), 0)
    index_bits = max(1, (isc_ref.shape[0] - 1).bit_length())

    def tie_bound():
        need = kf - count(lambda x, off: x > thr)

        def ties_below(bound):
            return count(lambda x, off: jnp.logical_and(x == thr, key_id < bound - off))

        def bit_step(i, last):
            cand = last | lax.shift_left(jnp.int32(1), index_bits - 1 - i)
            return jnp.where(ties_below(cand) < need, cand, last)

        return lax.fori_loop(0, index_bits, bit_step, jnp.zeros((1, tq), I32)) + 1

    everything = jnp.full((1, tq), isc_ref.shape[0], I32)
    tie_end = lax.cond(unfinished(cnt) > 0.0, tie_bound, lambda: everything)
    tie_end = jnp.where(cnt == kf, everything, tie_end)

    def mask_tile(off, width):
        x = isc_ref[pl.ds(off, width), :]
        ids = lax.broadcasted_iota(I32, (width, tq), 0)
        keep = jnp.logical_or(x > thr, jnp.logical_and(x == thr, ids < tie_end - off))
        bias_ref[:, pl.ds(off, width)] = jnp.where(keep, 0.0, NEG).T

    def body_m(t, c):
        mask_tile(pl.multiple_of(LANES + t * tq, LANES), tq)
        return c

    mask_tile(0, LANES)
    lax.fori_loop(0, qb + 1, body_m, 0)

    m_ref[...] = jnp.full(m_ref.shape, NEG, F32)
    l_ref[...] = jnp.zeros(l_ref.shape, F32)
    acc_ref[...] = jnp.zeros(acc_ref.shape, F32)
    qa = qa_ref[...].reshape(nh * tq, DSA_DIM)

    def att_tile(koff, width):
        kt = k_ref[pl.ds(koff, width), :]
        vt = v_ref[pl.ds(koff, width), :]
        s = lax.dot_general(qa, kt, NT, preferred_element_type=F32).reshape(nh, tq, width)
        reps = width // LANES
        s = (s + bias_ref[:, pl.ds(koff, width)][None]).reshape(nh * tq, width)
        m_prev = m_ref[...]
        m_new = jnp.maximum(m_prev, jnp.max(s, axis=1, keepdims=True))
        alpha = jnp.exp(m_prev - m_new)
        m_w = m_new if reps == 1 else jnp.concatenate([m_new] * reps, axis=1)
        p = jnp.exp(s - m_w)
        l_ref[...] = alpha * l_ref[...] + jnp.sum(p, axis=1, keepdims=True)
        acc_ref[...] = alpha * acc_ref[...] + jnp.dot(p.astype(BF16), vt, preferred_element_type=F32)
        m_ref[...] = m_new

    att_tile(0, LANES)

    def body_c(t, c):
        att_tile(pl.multiple_of(LANES + t * tq, LANES), tq)
        return c

    lax.fori_loop(0, qb + 1, body_c, 0)

    o = (acc_ref[...] / l_ref[...]).astype(BF16).reshape(nh, tq, DSA_DIM)
    for h in range(nh):
        out_ref[:, h * DSA_OUT_DIM:(h + 1) * DSA_OUT_DIM] = jnp.dot(
            o[h], wuv_ref[h], preferred_element_type=F32).astype(BF16)


def _dsa(dq, iq, misc, kpad, vpad, kipad, wuv, batch, seq, topk):
    n = misc.shape[0]
    nq = seq // DSA_BLOCK
    tp = kpad.shape[1]
    head = lambda b, q: (0, b * nq + q, 0)
    row = lambda b, q: (b * nq + q, 0)
    per_b = lambda b, q: (b, 0, 0)
    return pl.pallas_call(
        functools.partial(_dsa_body, topk=topk),
        grid=(batch, nq),
        in_specs=[pl.BlockSpec((DSA_HEADS, DSA_BLOCK, DSA_DIM), head),
                  pl.BlockSpec((IDX_HEADS, DSA_BLOCK, IDX_DIM), head),
                  pl.BlockSpec((DSA_BLOCK, 128), row),
                  pl.BlockSpec((None, tp, DSA_DIM), per_b),
                  pl.BlockSpec((None, tp, DSA_DIM), per_b),
                  pl.BlockSpec((None, tp, IDX_DIM), per_b),
                  pl.BlockSpec((DSA_HEADS, DSA_DIM, DSA_OUT_DIM), lambda b, q: (0, 0, 0))],
        out_specs=pl.BlockSpec((DSA_BLOCK, 512), row),
        out_shape=jax.ShapeDtypeStruct((n, 512), BF16),
        scratch_shapes=[pltpu.VMEM((tp, DSA_BLOCK), F32),
                        pltpu.VMEM((DSA_BLOCK, tp), F32),
                        pltpu.VMEM((DSA_HEADS * DSA_BLOCK, LANES), F32),
                        pltpu.VMEM((DSA_HEADS * DSA_BLOCK, LANES), F32),
                        pltpu.VMEM((DSA_HEADS * DSA_BLOCK, DSA_DIM), F32)],
        compiler_params=_params(("arbitrary", "arbitrary")),
        name="dsa",
    )(dq, iq, misc, kpad, vpad, kipad, wuv)


def _post_body(ret_ref, att_ref, x_ref, w_ref, g_ref, h1_ref, hnt_ref):
    y = jnp.dot(ret_ref[...], w_ref[0:512, :], preferred_element_type=F32)
    y = y + jnp.dot(att_ref[...], w_ref[512:1024, :], preferred_element_type=F32)
    h1 = x_ref[...] + y
    h1_ref[...] = h1
    hnt_ref[...] = _rms(h1, g_ref[...]).T.astype(BF16)


def _post(ret, att, x2d, w_out, g, tm):
    n = x2d.shape[0]
    row = lambda i: (i, 0)
    return pl.pallas_call(
        _post_body,
        grid=(n // tm,),
        in_specs=[pl.BlockSpec((tm, 512), row), pl.BlockSpec((tm, 512), row),
                  pl.BlockSpec((tm, D_MODEL), row),
                  pl.BlockSpec((D_MODEL, D_MODEL), lambda i: (0, 0)),
                  pl.BlockSpec((1, D_MODEL), lambda i: (0, 0))],
        out_specs=(pl.BlockSpec((tm, D_MODEL), row), pl.BlockSpec((D_MODEL, tm), lambda i: (0, i))),
        out_shape=(jax.ShapeDtypeStruct((n, D_MODEL), F32), jax.ShapeDtypeStruct((D_MODEL, n), BF16)),
        compiler_params=_params(("arbitrary",)),
        name="post",
    )(ret, att, x2d, w_out, g)


def _oddeven_merge_sort_pairs(n):
    pairs = []
    p = 1
    while p < n:
        k = p
        while k >= 1:
            for j in range(k % p, n - k, 2 * k):
                for i in range(min(k, n - j - k)):
                    if (i + j) // (2 * p) == (i + j + k) // (2 * p):
                        pairs.append((i + j, i + j + k))
            k //= 2
        p *= 2
    return tuple(pairs)


def _bitonic_merge_pairs(n):
    pairs = []
    d = n // 2
    while d >= 1:
        pairs += [(i, i + d) for i in range(n) if (i & d) == 0]
        d //= 2
    return tuple(pairs)


_SORT16 = _oddeven_merge_sort_pairs(PEER_TOPK)
_MERGE16 = _bitonic_merge_pairs(PEER_TOPK)
SUBLANES = 8


def _top16_rows(s):
    def exchange(x, pairs):
        for i, j in pairs:
            x[i], x[j] = jnp.maximum(x[i], x[j]), jnp.minimum(x[i], x[j])

    x = [s[SUBLANES * j:SUBLANES * (j + 1), :] for j in range(PEER_TOPK)]
    exchange(x, _SORT16)
    shift = SUBLANES // 2
    while shift >= 1:
        other = [pltpu.roll(v, shift, 0) for v in x]
        x = [jnp.maximum(x[i], other[PEER_TOPK - 1 - i]) for i in range(PEER_TOPK)]
        exchange(x, _MERGE16)
        shift //= 2
    return [v[0:1] for v in x]


def _prefix_count(pred, w):
    m8 = pred(w[7])
    m4 = pred(jnp.where(m8, w[11], w[3]))
    m2 = pred(jnp.where(m8, jnp.where(m4, w[13], w[9]), jnp.where(m4, w[5], w[1])))
    m1 = pred(jnp.where(m8, jnp.where(m4, jnp.where(m2, w[14], w[12]), jnp.where(m2, w[10], w[8])),
                        jnp.where(m4, jnp.where(m2, w[6], w[4]), jnp.where(m2, w[2], w[0]))))
    bit = lambda m, v: jnp.where(m, v, 0.0)
    return bit(m8, 8.0) + bit(m4, 4.0) + bit(m2, 2.0) + bit(m1, 1.0) + bit(pred(w[15]), 1.0)


def _top_desc(s, k, dst_ref, ts, with_rank=False):
    cur = s
    rank = jnp.full(s.shape, float(k), F32) if with_rank else None
    for r in range(k):
        m = jnp.max(cur, axis=0, keepdims=True)
        dst_ref[r:r + 1, ts] = m
        if with_rank or r + 1 < k:
            hit = cur == m
            if with_rank:
                rank = jnp.where(hit, float(r), rank)
            cur = jnp.where(hit, -jnp.inf, cur)
    return rank


def _psel_body(hnt_ref, wq_ref, sk_ref, cnt_ref, e1_ref, r2_ref, e2_ref, s_ref, v1_ref, v2_ref, vc_ref):
    hnt = hnt_ref[...]
    for h in range(PEER_HEADS):
        q = jnp.dot(wq_ref[h * 256:(h + 1) * 256, :], hnt, preferred_element_type=F32)
        s_ref[0] = jnp.dot(sk_ref[h, 0], q[:128].astype(BF16), preferred_element_type=F32)
        s_ref[1] = jnp.dot(sk_ref[h, 1], q[128:].astype(BF16), preferred_element_type=F32)
        for tc in range(hnt.shape[1] // LANES):
            ts = slice(tc * LANES, (tc + 1) * LANES)
            s1 = s_ref[0, :, ts]
            s2 = s_ref[1, :, ts]
            w1 = _top16_rows(s1)
            w2 = _top16_rows(s2)
            for b in range(PEER_TOPK):
                v1_ref[b:b + 1, ts] = w1[b]
                v2_ref[b:b + 1, ts] = w2[b]
            v1 = v1_ref[:, ts]
            v2 = v2_ref[:, ts]
            cand = jnp.concatenate([v2 + v1[0:1]] + [v2[0:8] + v1[a:a + 1] for a in range(1, 8)]
                                   + [v1[8:16] + v2[0:1]], axis=0)
            _top_desc(cand, PEER_TOPK, vc_ref, ts)
            tau = vc_ref[PEER_TOPK - 1:PEER_TOPK, ts]
            top = vc_ref[0:1, ts]
            zsum = jnp.sum(jnp.where(cand >= tau, jnp.exp(cand - top), 0.0), axis=0, keepdims=True)
            cnt_ref[h, :, ts] = _prefix_count(lambda t: s1 + t >= tau, w2)
            e1_ref[h, :, ts] = jnp.exp(s1 - v1[0:1]) / zsum
            rank2 = _prefix_count(lambda t: t > s2, w2)
            r2_ref[h, :, ts] = pltpu.bitcast(rank2.astype(BF16), I32)
            e2_ref[h, :, ts] = pltpu.bitcast(jnp.exp(s2 - v2[0:1]).astype(BF16), I32)


def _psel(hnt, wq_t, sk, tn):
    n = hnt.shape[1]
    spec = pl.BlockSpec((PEER_HEADS, PEER_NKEYS, tn), lambda i: (0, 0, i))
    shp32 = jax.ShapeDtypeStruct((PEER_HEADS, PEER_NKEYS, n), F32)
    shp16 = jax.ShapeDtypeStruct((PEER_HEADS, PEER_NKEYS // 2, n), I32)
    spec16 = pl.BlockSpec((PEER_HEADS, PEER_NKEYS // 2, tn), lambda i: (0, 0, i))
    return pl.pallas_call(
        _psel_body,
        grid=(n // tn,),
        in_specs=[pl.BlockSpec((D_MODEL, tn), lambda i: (0, i)),
                  pl.BlockSpec((PEER_HEADS * 256, D_MODEL), lambda i: (0, 0)),
                  pl.BlockSpec((PEER_HEADS, 2, PEER_NKEYS, 128), lambda i: (0, 0, 0, 0))],
        out_specs=(spec, spec, spec16, spec16),
        out_shape=(shp32, shp32, shp16, shp16),
        scratch_shapes=[pltpu.VMEM((2, PEER_NKEYS, tn), F32)] + [pltpu.VMEM((PEER_TOPK, tn), F32)] * 3,
        compiler_params=_params(("arbitrary",)),
        name="psel",
    )(hnt, wq_t, sk)


def _pdense_body(*refs, eb, nchunk, nblk):
    hnt_ref, u_ref, vt_ref = refs[:3]
    cnt_refs = refs[3:3 + nchunk]
    e1_refs = refs[3 + nchunk:3 + 2 * nchunk]
    r2_ref, e2_ref, h1_ref, g_ref, out_ref, acc_ref, a_ref, coef_ref = refs[3 + 2 * nchunk:]
    s = pl.program_id(1)
    nsub = eb // PEER_NKEYS

    def row_bf16(ref, h, i1):
        return jnp.broadcast_to(ref[h, pl.ds(i1, 1), :], (PEER_NKEYS, LANES)).astype(BF16)

    def stage_a(a_ref):
        for half in range(eb // PDENSE_ROWS):
            rows = slice(half * PDENSE_ROWS, (half + 1) * PDENSE_ROWS)
            a_ref[rows, :] = jnp.dot(u_ref[rows, :], hnt_ref[...], preferred_element_type=F32)

    def stage_b(blk, a_ref, coef_ref):
        for cp in range(0, nsub, PDENSE_GROUP):
            cs = [cp + k for k in range(PDENSE_GROUP)]
            for tc in range(nchunk):
                ts = slice(tc * LANES, (tc + 1) * LANES)
                gates = [None] * PDENSE_GROUP
                for h in range(PEER_HEADS):
                    r2 = pltpu.bitcast(r2_ref[h, :, ts], BF16)
                    e2 = pltpu.bitcast(e2_ref[h, :, ts], BF16)
                    for k in range(PDENSE_GROUP):
                        cnt = row_bf16(cnt_refs[tc], h, blk * nsub + cs[k])
                        e1 = row_bf16(e1_refs[tc], h, blk * nsub + cs[k])
                        term = jnp.where(r2 < cnt, e2, jnp.zeros((), BF16)) * e1
                        gates[k] = term if gates[k] is None else gates[k] + term
                for k in range(PDENSE_GROUP):
                    rows = slice(cs[k] * PEER_NKEYS, (cs[k] + 1) * PEER_NKEYS)
                    at = a_ref[rows, ts]
                    gelu = (0.5 * at) * (1.0 + lax.erf(at * (0.5 ** 0.5)))
                    coef_ref[rows, ts] = gelu.astype(BF16) * gates[k]

    def stage_c(coef_ref):
        acc_ref[...] += jnp.dot(vt_ref[...], coef_ref[...], preferred_element_type=F32)

    @pl.when(s == 0)
    def _():
        acc_ref[...] = jnp.zeros(acc_ref.shape, F32)

    stage_a(a_ref)
    stage_b(s, a_ref, coef_ref)
    stage_c(coef_ref)

    @pl.when(s == nblk - 1)
    def _():
        out_ref[...] = _rms(h1_ref[...] + acc_ref[...].T, g_ref[...])


def _pdense(hnt, u, vt, cnt, e1, r2, e2, h1, g, tn, eb):
    n = hnt.shape[1]
    nblk = u.shape[0] // eb
    assert nblk >= 2 and vt.shape == (nblk, D_MODEL, eb)
    nchunk = tn // LANES
    chunk = [pl.BlockSpec((PEER_HEADS, PEER_NKEYS, LANES), lambda i, s, c=c: (0, 0, i * nchunk + c))
             for c in range(nchunk)]
    sel16 = pl.BlockSpec((PEER_HEADS, PEER_NKEYS // 2, tn), lambda i, s: (0, 0, i))
    return pl.pallas_call(
        functools.partial(_pdense_body, eb=eb, nchunk=nchunk, nblk=nblk),
        grid=(n // tn, nblk),
        in_specs=[pl.BlockSpec((D_MODEL, tn), lambda i, s: (0, i)),
                  pl.BlockSpec((eb, D_MODEL), lambda i, s: (s, 0)),
                  pl.BlockSpec((None, D_MODEL, eb), lambda i, s: (s, 0, 0))]
                 + chunk + chunk
                 + [sel16, sel16,
                    pl.BlockSpec((tn, D_MODEL), lambda i, s: (i, 0)),
                    pl.BlockSpec((1, D_MODEL), lambda i, s: (0, 0))],
        out_specs=pl.BlockSpec((tn, D_MODEL), lambda i, s: (i, 0)),
        out_shape=jax.ShapeDtypeStruct((n, D_MODEL), F32),
        scratch_shapes=[pltpu.VMEM((D_MODEL, tn), F32),
                        pltpu.VMEM((eb, tn), F32), pltpu.VMEM((eb, tn), BF16)],
        compiler_params=_params(("arbitrary", "arbitrary")),
        name="pdense",
    )(hnt, u, vt, *([cnt] * nchunk), *([e1] * nchunk), r2, e2, h1, g)


def kernel(x, meta_tokens, norm_mix, w_in, w_uv, w_out, norm_ffn, peer_wq, peer_subkeys,
           peer_u, peer_v, norm_final):
    batch, seq, d = x.shape
    assert d == D_MODEL and norm_mix.shape[0] == 1 and seq % DSA_BLOCK == 0
    n = batch * seq
    topk = min(DSA_TOPK_MAX, seq // 4)
    tm = 512 if seq % 512 == 0 else 256

    w = jnp.pad(w_in[0], ((0, 0), (0, IN_WIDTH_PAD - IN_WIDTH))).astype(BF16)
    g_mix = norm_mix[0][None, :]
    x2d = x.reshape(n, d)
    pos_f = jnp.arange(seq, dtype=F32) + float(N_META)
    pos_m = jnp.arange(N_META, dtype=F32)

    rq, rk, rv, rg, dq, dk, dv, iq, misc = _proj(x2d, g_mix, w, _rope_tables(pos_f), tm, seq // tm)
    _, mk, mv, _, _, mdk, mdv, _, mmisc = _proj(meta_tokens, g_mix, w, _rope_tables(pos_m), N_META, 1)

    ret = _retention(rq, rk, rv, rg, mk, mv, batch, seq)

    def keys(frames, meta):
        width = frames.shape[-1]
        lead = jnp.concatenate([jnp.zeros((LANES - N_META, width), frames.dtype), meta], axis=0)
        lead = jnp.broadcast_to(lead[None], (batch, LANES, width))
        return jnp.concatenate([lead, frames.reshape(batch, seq, width)], axis=1)

    kpad = keys(dk, mdk)
    vpad = keys(dv, mdv)
    kipad = keys(misc[:, :IDX_DIM].astype(BF16), mmisc[:, :IDX_DIM].astype(BF16))
    att = _dsa(dq, iq, misc, kpad, vpad, kipad, w_uv[0].astype(BF16), batch, seq, topk)

    h1, hnt = _post(ret, att, x2d, w_out[0].astype(BF16), norm_ffn[0][None, :], tm)

    wq_t = peer_wq[0].reshape(d, PEER_HEADS * 256).T.astype(BF16)
    cnt, e1, r2, e2 = _psel(hnt, wq_t, peer_subkeys[0].astype(BF16), tm)
    vt = peer_v[0].reshape(-1, PDENSE_EXPERTS, d).transpose(0, 2, 1).astype(BF16)
    out = _pdense(hnt, peer_u[0].astype(BF16), vt, cnt, e1, r2, e2, h1,
                  norm_final[None, :], tm, PDENSE_EXPERTS)
    return out.reshape(batch, seq, d)
```

```python
import functools

import jax
import jax.numpy as jnp
from jax import lax
from jax.experimental import pallas as pl
from jax.experimental.pallas import tpu as pltpu

F32 = jnp.float32
BF16 = jnp.bfloat16
I32 = jnp.int32

D_MODEL = 1024
N_META = 16
CHUNK = 64
RET_HEADS = 4
RET_DIM = 128
DSA_HEADS = 8
DSA_DIM = 128
DSA_OUT_DIM = 64
IDX_HEADS = 8
IDX_DIM = 64
DSA_TOPK_MAX = 256
ROPE_THETA = 10000.0
PEER_HEADS = 8
PEER_NKEYS = 128
PEER_TOPK = 16
EPS = 1e-6
NEG = -1e30
INT_MIN = -(2 ** 31)

IN_WIDTH = 3912
IN_WIDTH_PAD = 3968
LANES = 128
RET_BLOCK = 256
PDENSE_EXPERTS = 2048
PDENSE_ROWS = 512
PDENSE_GROUP = 2
DSA_BLOCK = 256
DSA_BITS_PER_CHECK = 4
VMEM_LIMIT = 56 * 1024 * 1024

NT = (((1,), (1,)), ((), ()))
TN = (((0,), (0,)), ((), ()))


def _rms(x, g):
    return x * lax.rsqrt(jnp.mean(x * x, axis=-1, keepdims=True) + EPS) * g


def _params(sem):
    return pltpu.CompilerParams(dimension_semantics=sem, vmem_limit_bytes=VMEM_LIMIT)


def _proj_body(x_ref, g_ref, w_ref, tab_ref, rq_ref, rk_ref, rv_ref, rg_ref, dq_ref, dk_ref,
               dv_ref, iq_ref, misc_ref):
    xn = _rms(x_ref[...], g_ref[...]).astype(BF16)
    cos_a, sin_a, cos_b, sin_b1, sin_b2, cos_m, sin_m1, sin_m2 = (tab_ref[i] for i in range(8))

    def mm(lo, n):
        return jnp.dot(xn, w_ref[:, lo:lo + n], preferred_element_type=F32)

    def rope_a(y):
        return y * cos_a + pltpu.roll(y, 64, 1) * sin_a

    def rope_b(y, c, s1, s2):
        return y * c + pltpu.roll(y, 96, 1) * s1 + pltpu.roll(y, 32, 1) * s2

    y = mm(0, 512)
    for h in range(RET_HEADS):
        rq_ref[:, h * 128:(h + 1) * 128] = rope_a(y[:, h * 128:(h + 1) * 128]).astype(BF16)
    y = mm(512, 512)
    for h in range(RET_HEADS):
        rk_ref[:, h * 128:(h + 1) * 128] = (
            rope_a(y[:, h * 128:(h + 1) * 128]) * (RET_DIM ** -0.5)).astype(BF16)
    rv_ref[...] = mm(1024, 512).astype(BF16)
    rg_ref[...] = mm(1536, 512)
    y = mm(2048, 1024)
    for h in range(DSA_HEADS):
        dq_ref[h] = (rope_a(y[:, h * 128:(h + 1) * 128]) * (DSA_DIM ** -0.5)).astype(BF16)
    dk_ref[...] = rope_a(mm(3072, 128)).astype(BF16)
    dv_ref[...] = mm(3200, 128).astype(BF16)
    y = mm(3328, 512)
    for p in range(IDX_HEADS // 2):
        o = (rope_b(y[:, p * 128:(p + 1) * 128], cos_b, sin_b1, sin_b2) * (IDX_DIM ** -0.5)).astype(BF16)
        iq_ref[2 * p] = o[:, :64]
        iq_ref[2 * p + 1] = o[:, 64:]
    misc_ref[...] = rope_b(mm(3840, 128), cos_m, sin_m1, sin_m2)


def _proj(x2d, g, w, tab, tm, tab_blocks):
    n = x2d.shape[0]
    row = lambda i: (i, 0)
    head = lambda i: (0, i, 0)
    out_shape = (
        jax.ShapeDtypeStruct((n, 512), BF16), jax.ShapeDtypeStruct((n, 512), BF16),
        jax.ShapeDtypeStruct((n, 512), BF16), jax.ShapeDtypeStruct((n, 512), F32),
        jax.ShapeDtypeStruct((DSA_HEADS, n, 128), BF16), jax.ShapeDtypeStruct((n, 128), BF16),
        jax.ShapeDtypeStruct((n, 128), BF16), jax.ShapeDtypeStruct((IDX_HEADS, n, 64), BF16),
        jax.ShapeDtypeStruct((n, 128), F32))
    out_specs = (
        pl.BlockSpec((tm, 512), row), pl.BlockSpec((tm, 512), row), pl.BlockSpec((tm, 512), row),
        pl.BlockSpec((tm, 512), row), pl.BlockSpec((DSA_HEADS, tm, 128), head),
        pl.BlockSpec((tm, 128), row), pl.BlockSpec((tm, 128), row),
        pl.BlockSpec((IDX_HEADS, tm, 64), head), pl.BlockSpec((tm, 128), row))
    return pl.pallas_call(
        _proj_body,
        grid=(n // tm,),
        in_specs=[pl.BlockSpec((tm, D_MODEL), row),
                  pl.BlockSpec((1, D_MODEL), lambda i: (0, 0)),
                  pl.BlockSpec((D_MODEL, IN_WIDTH_PAD), lambda i: (0, 0)),
                  pl.BlockSpec((8, tm, 128), lambda i: (0, i % tab_blocks, 0))],
        out_specs=out_specs,
        out_shape=out_shape,
        compiler_params=_params(("arbitrary",)),
        name="proj",
    )(x2d, g, w, tab)


def _rope_tables(pos):
    p = pos.shape[0]
    inv = ROPE_THETA ** (-jnp.arange(0, 64, dtype=F32) * 2.0 / 128)
    ang = pos[:, None] * inv[None, :]
    c, s = jnp.cos(ang), jnp.sin(ang)
    cos_a = jnp.concatenate([c, c], -1)
    sin_a = jnp.concatenate([-s, s], -1)
    inv = ROPE_THETA ** (-jnp.arange(0, 32, dtype=F32) * 2.0 / 64)
    ang = pos[:, None] * inv[None, :]
    c, s = jnp.cos(ang), jnp.sin(ang)
    z = jnp.zeros_like(s)
    z64 = jnp.zeros((p, 64), F32)
    cos_b = jnp.concatenate([c, c, c, c], -1)
    sin_b1 = jnp.concatenate([-s, z, -s, z], -1)
    sin_b2 = jnp.concatenate([z, s, z, s], -1)
    cos_m = jnp.concatenate([c, c, jnp.full((p, 64), IDX_HEADS ** -0.5, F32)], -1)
    sin_m1 = jnp.concatenate([-s, z, z64], -1)
    sin_m2 = jnp.concatenate([z, s, z64], -1)
    return jnp.stack([cos_a, sin_a, cos_b, sin_b1, sin_b2, cos_m, sin_m1, sin_m2])


def _ret_body(rq_ref, rk_ref, rv_ref, rg_ref, mk_ref, mv_ref, dmat_ref, xi_ref, zeta_ref,
              mzeta_ref, gsc_ref, out_ref, r_ref):
    @pl.when(pl.program_id(1) == 0)
    def _():
        for h in range(RET_HEADS):
            hs = slice(h * 128, (h + 1) * 128)
            kz = (mk_ref[:, hs].astype(F32) * mzeta_ref[h]).astype(BF16)
            r_ref[h] = lax.dot_general(kz, mv_ref[:, hs], TN, preferred_element_type=F32)

    for h in range(RET_HEADS):
        hs = slice(h * 128, (h + 1) * 128)
        q = rq_ref[:, hs]
        k = rk_ref[:, hs]
        v = rv_ref[:, hs]
        s = lax.dot_general(q, k, NT, preferred_element_type=F32) * dmat_ref[h]
        o = jnp.dot(s.astype(BF16), v, preferred_element_type=F32)
        r = r_ref[h]
        qx = (q.astype(F32) * xi_ref[h]).astype(BF16)
        o = o + jnp.dot(qx, r.astype(BF16), preferred_element_type=F32)
        kz = (k.astype(F32) * zeta_ref[h]).astype(BF16)
        u = lax.dot_general(kz, v, TN, preferred_element_type=F32)
        r_ref[h] = r * gsc_ref[h] + u
        mu = jnp.mean(o, axis=-1, keepdims=True)
        d = o - mu
        var = jnp.mean(d * d, axis=-1, keepdims=True)
        on = d * lax.rsqrt(var + EPS)
        g = rg_ref[:, hs]
        out_ref[:, hs] = (g * jax.nn.sigmoid(g) * on).astype(BF16)


def _retention(rq, rk, rv, rg, mk, mv, batch, seq):
    n = rq.shape[0]
    nb = seq // RET_BLOCK
    lg = jnp.log(1.0 - 2.0 ** (-5.0 - jnp.arange(RET_HEADS, dtype=F32)))
    i = jnp.arange(RET_BLOCK, dtype=F32)
    ci = jnp.arange(RET_BLOCK) // CHUNK
    vis = (ci[None, :] <= ci[:, None])
    dmat = jnp.where(vis[None], jnp.exp(lg[:, None, None] * jnp.abs(i[:, None] - i[None, :])), 0.0)
    xi = jnp.broadcast_to(jnp.exp(lg[:, None] * (i + 1.0)[None, :])[:, :, None], (RET_HEADS, RET_BLOCK, 128))
    zeta = jnp.broadcast_to(jnp.exp(lg[:, None] * (RET_BLOCK - 1.0 - i)[None, :])[:, :, None],
                            (RET_HEADS, RET_BLOCK, 128))
    im = jnp.arange(N_META, dtype=F32)
    mzeta = jnp.broadcast_to(jnp.exp(lg[:, None] * (N_META - 1.0 - im)[None, :])[:, :, None],
                             (RET_HEADS, N_META, 128))
    gsc = jnp.broadcast_to(jnp.exp(lg * RET_BLOCK)[:, None, None], (RET_HEADS, 1, 128))
    row = lambda b, s: (b * nb + s, 0)
    c2 = lambda b, s: (0, 0)
    c3 = lambda b, s: (0, 0, 0)
    return pl.pallas_call(
        _ret_body,
        grid=(batch, nb),
        in_specs=[pl.BlockSpec((RET_BLOCK, 512), row), pl.BlockSpec((RET_BLOCK, 512), row),
                  pl.BlockSpec((RET_BLOCK, 512), row), pl.BlockSpec((RET_BLOCK, 512), row),
                  pl.BlockSpec((N_META, 512), c2), pl.BlockSpec((N_META, 512), c2),
                  pl.BlockSpec((RET_HEADS, RET_BLOCK, RET_BLOCK), c3),
                  pl.BlockSpec((RET_HEADS, RET_BLOCK, 128), c3),
                  pl.BlockSpec((RET_HEADS, RET_BLOCK, 128), c3),
                  pl.BlockSpec((RET_HEADS, N_META, 128), c3),
                  pl.BlockSpec((RET_HEADS, 1, 128), c3)],
        out_specs=pl.BlockSpec((RET_BLOCK, 512), row),
        out_shape=jax.ShapeDtypeStruct((n, 512), BF16),
        scratch_shapes=[pltpu.VMEM((RET_HEADS, 128, 128), F32)],
        compiler_params=_params(("arbitrary", "arbitrary")),
        name="retention",
    )(rq, rk, rv, rg, mk, mv, dmat, xi, zeta, mzeta, gsc)


def _aligned(off):
    return off if isinstance(off, int) else pl.multiple_of(off, LANES)


def _dsa_body(qa_ref, qi_ref, misc_ref, k_ref, v_ref, ki_ref, wuv_ref, out_ref,
              isc_ref, bias_ref, m_ref, l_ref, acc_ref, *, topk):
    qb = pl.program_id(1)
    tq = DSA_BLOCK
    nh = DSA_HEADS

    wt = misc_ref[...].T
    qi = qi_ref[...].reshape(IDX_HEADS * tq, IDX_DIM)

    def isc_tile(koff, width, allowed):
        kt = ki_ref[pl.ds(koff, width), :]
        z = lax.dot_general(kt, qi, NT, preferred_element_type=F32)
        isc = None
        for h in range(IDX_HEADS):
            term = jnp.maximum(z[:, h * tq:(h + 1) * tq], 0.0) * wt[64 + h:65 + h, :]
            isc = term if isc is None else isc + term
        if allowed is not None:
            isc = jnp.where(allowed, isc, -jnp.inf)
        isc_ref[pl.ds(koff, width), :] = isc

    isc_tile(0, LANES, lax.broadcasted_iota(I32, (LANES, tq), 0) >= LANES - N_META)

    def body_a(t, c):
        isc_tile(pl.multiple_of(LANES + t * tq, LANES), tq, None)
        return c

    lax.fori_loop(0, qb, body_a, 0)
    keyc = lax.broadcasted_iota(I32, (tq, tq), 0) // CHUNK
    qryc = lax.broadcasted_iota(I32, (tq, tq), 1) // CHUNK
    diag_off = pl.multiple_of(LANES + qb * tq, LANES)
    isc_tile(diag_off, tq, keyc <= qryc)

    kf = float(topk)
    sub = 8

    def key_to_f32(key):
        return lax.bitcast_convert_type(jnp.where(key < 0, key ^ jnp.int32(0x7FFFFFFF), key), F32)

    def count(pred):
        def slab(off, a):
            hit = jnp.where(pred(isc_ref[pl.ds(off, LANES), :], off), 1.0, 0.0)
            return a + jnp.sum(hit.reshape(LANES // sub, sub, tq), axis=0)

        def body(t, a):
            off = pl.multiple_of(LANES + t * (2 * LANES), LANES)
            return slab(pl.multiple_of(off + LANES, LANES), slab(off, a))

        a = lax.fori_loop(0, qb + 1, body, slab(0, jnp.zeros((sub, tq), F32)))
        return jnp.sum(a, axis=0, keepdims=True)

    def count_ge(cand):
        return count(lambda x, off: x >= cand)

    lowest = jnp.full((1, tq), -3.0e38, F32)
    n_real = count_ge(lowest)
    c_pos = count_ge(jnp.zeros((1, tq), F32))
    pos = c_pos >= kf
    base0 = jnp.where(pos, jnp.int32(0), jnp.int32(INT_MIN))
    cnt0 = jnp.where(n_real <= kf, kf, jnp.where(pos, c_pos, n_real))

    def unfinished(cnt):
        return jnp.max(jnp.where(cnt != kf, 1.0, 0.0))

    def try_bit(bit, base, cnt):
        cand = base | lax.shift_left(jnp.int32(1), bit)
        c = count_ge(key_to_f32(cand))
        ok = c >= kf
        return jnp.where(ok, cand, base), jnp.where(ok, c, cnt)

    base, cnt = base0, cnt0
    for bit in range(30, DSA_BITS_PER_CHECK * 7 - 1, -1):
        base, cnt = try_bit(bit, base, cnt)

    zero_tie = jnp.logical_and(pos, count(lambda x, off: x > 0.0) < kf)

    def group_cond(state):
        g, _, cnt = state
        return jnp.logical_and(g >= 0, unfinished(jnp.where(zero_tie, kf, cnt)) > 0.0)

    def group_body(state):
        g, base, cnt = state
        for k in range(DSA_BITS_PER_CHECK - 1, -1, -1):
            base, cnt = try_bit(g * DSA_BITS_PER_CHECK + k, base, cnt)
        return g - 1, base, cnt

    _, base, cnt = lax.while_loop(group_cond, group_body, (jnp.int32(6), base, cnt))

    def refine_cond(state):
        it, _, _, _, go = state
        return jnp.logical_and(it < 40, go > 0.0)

    def refine_body(state):
        it, lo, hi, cnt, _ = state
        mid = lo + 0.5 * (hi - lo)
        c = count_ge(mid)
        ok = c >= kf
        cnt = jnp.where(ok, c, cnt)
        moving = jnp.logical_and(jnp.logical_and(mid > lo, mid < hi),
                                 jnp.logical_and(cnt != kf, jnp.logical_not(zero_tie)))
        return it + 1, jnp.where(ok, mid, lo), jnp.where(ok, hi, mid), cnt, jnp.max(jnp.where(moving, 1.0, 0.0))

    _, thr, _, cnt, _ = lax.while_loop(
        refine_cond, refine_body,
        (jnp.int32(0), key_to_f32(base), key_to_f32(base + 1), cnt, unfinished(jnp.where(zero_tie, kf, cnt))))
    thr = jnp.where(n_real <= kf, lowest, thr)

    key_id = lax.broadcasted_iota(I32, (LANES, tq), 0)
    index_bits = max(1, (isc_ref.shape[0] - 1).bit_length())

    def tie_bound():
        need = kf - count(lambda x, off: x > thr)

        def ties_below(bound):
            return count(lambda x, off: jnp.logical_and(x == thr, key_id < bound - off))

        def bit_step(i, last):
            cand = last | lax.shift_left(jnp.int32(1), index_bits - 1 - i)
            return jnp.where(ties_below(cand) < need, cand, last)

        return lax.fori_loop(0, index_bits, bit_step, jnp.zeros((1, tq), I32)) + 1

    everything = jnp.full((1, tq), isc_ref.shape[0], I32)
    tie_end = lax.cond(unfinished(cnt) > 0.0, tie_bound, lambda: everything)
    tie_end = jnp.where(cnt == kf, everything, tie_end)

    def mask_tile(off, width):
        x = isc_ref[pl.ds(off, width), :]
        ids = lax.broadcasted_iota(I32, (width, tq), 0)
        keep = jnp.logical_or(x > thr, jnp.logical_and(x == thr, ids < tie_end - off))
        bias_ref[:, pl.ds(off, width)] = jnp.where(keep, 0.0, NEG).T

    def body_m(t, c):
        mask_tile(pl.multiple_of(LANES + t * tq, LANES), tq)
        return c

    mask_tile(0, LANES)
    lax.fori_loop(0, qb + 1, body_m, 0)

    m_ref[...] = jnp.full(m_ref.shape, NEG, F32)
    l_ref[...] = jnp.zeros(l_ref.shape, F32)
    acc_ref[...] = jnp.zeros(acc_ref.shape, F32)
    qa = qa_ref[...].reshape(nh * tq, DSA_DIM)

    def att_tile(koff, width):
        kt = k_ref[pl.ds(koff, width), :]
        vt = v_ref[pl.ds(koff, width), :]
        s = lax.dot_general(qa, kt, NT, preferred_element_type=F32).reshape(nh, tq, width)
        reps = width // LANES
        s = (s + bias_ref[:, pl.ds(koff, width)][None]).reshape(nh * tq, width)
        m_prev = m_ref[...]
        m_new = jnp.maximum(m_prev, jnp.max(s, axis=1, keepdims=True))
        alpha = jnp.exp(m_prev - m_new)
        m_w = m_new if reps == 1 else jnp.concatenate([m_new] * reps, axis=1)
        p = jnp.exp(s - m_w)
        l_ref[...] = alpha * l_ref[...] + jnp.sum(p, axis=1, keepdims=True)
        acc_ref[...] = alpha * acc_ref[...] + jnp.dot(p.astype(BF16), vt, preferred_element_type=F32)
        m_ref[...] = m_new

    att_tile(0, LANES)

    def body_c(t, c):
        att_tile(pl.multiple_of(LANES + t * tq, LANES), tq)
        return c

    lax.fori_loop(0, qb + 1, body_c, 0)

    o = (acc_ref[...] / l_ref[...]).astype(BF16).reshape(nh, tq, DSA_DIM)
    for h in range(nh):
        out_ref[:, h * DSA_OUT_DIM:(h + 1) * DSA_OUT_DIM] = jnp.dot(
            o[h], wuv_ref[h], preferred_element_type=F32).astype(BF16)


def _dsa(dq, iq, misc, kpad, vpad, kipad, wuv, batch, seq, topk):
    n = misc.shape[0]
    nq = seq // DSA_BLOCK
    tp = kpad.shape[1]
    head = lambda b, q: (0, b * nq + q, 0)
    row = lambda b, q: (b * nq + q, 0)
    per_b = lambda b, q: (b, 0, 0)
    return pl.pallas_call(
        functools.partial(_dsa_body, topk=topk),
        grid=(batch, nq),
        in_specs=[pl.BlockSpec((DSA_HEADS, DSA_BLOCK, DSA_DIM), head),
                  pl.BlockSpec((IDX_HEADS, DSA_BLOCK, IDX_DIM), head),
                  pl.BlockSpec((DSA_BLOCK, 128), row),
                  pl.BlockSpec((None, tp, DSA_DIM), per_b),
                  pl.BlockSpec((None, tp, DSA_DIM), per_b),
                  pl.BlockSpec((None, tp, IDX_DIM), per_b),
                  pl.BlockSpec((DSA_HEADS, DSA_DIM, DSA_OUT_DIM), lambda b, q: (0, 0, 0))],
        out_specs=pl.BlockSpec((DSA_BLOCK, 512), row),
        out_shape=jax.ShapeDtypeStruct((n, 512), BF16),
        scratch_shapes=[pltpu.VMEM((tp, DSA_BLOCK), F32),
                        pltpu.VMEM((DSA_BLOCK, tp), F32),
                        pltpu.VMEM((DSA_HEADS * DSA_BLOCK, LANES), F32),
                        pltpu.VMEM((DSA_HEADS * DSA_BLOCK, LANES), F32),
                        pltpu.VMEM((DSA_HEADS * DSA_BLOCK, DSA_DIM), F32)],
        compiler_params=_params(("arbitrary", "arbitrary")),
        name="dsa",
    )(dq, iq, misc, kpad, vpad, kipad, wuv)


def _post_body(ret_ref, att_ref, x_ref, w_ref, g_ref, h1_ref, hnt_ref):
    y = jnp.dot(ret_ref[...], w_ref[0:512, :], preferred_element_type=F32)
    y = y + jnp.dot(att_ref[...], w_ref[512:1024, :], preferred_element_type=F32)
    h1 = x_ref[...] + y
    h1_ref[...] = h1
    hnt_ref[...] = _rms(h1, g_ref[...]).T.astype(BF16)


def _post(ret, att, x2d, w_out, g, tm):
    n = x2d.shape[0]
    row = lambda i: (i, 0)
    return pl.pallas_call(
        _post_body,
        grid=(n // tm,),
        in_specs=[pl.BlockSpec((tm, 512), row), pl.BlockSpec((tm, 512), row),
                  pl.BlockSpec((tm, D_MODEL), row),
                  pl.BlockSpec((D_MODEL, D_MODEL), lambda i: (0, 0)),
                  pl.BlockSpec((1, D_MODEL), lambda i: (0, 0))],
        out_specs=(pl.BlockSpec((tm, D_MODEL), row), pl.BlockSpec((D_MODEL, tm), lambda i: (0, i))),
        out_shape=(jax.ShapeDtypeStruct((n, D_MODEL), F32), jax.ShapeDtypeStruct((D_MODEL, n), BF16)),
        compiler_params=_params(("arbitrary",)),
        name="post",
    )(ret, att, x2d, w_out, g)


def _oddeven_merge_sort_pairs(n):
    pairs = []
    p = 1
    while p < n:
        k = p
        while k >= 1:
            for j in range(k % p, n - k, 2 * k):
                for i in range(min(k, n - j - k)):
                    if (i + j) // (2 * p) == (i + j + k) // (2 * p):
                        pairs.append((i + j, i + j + k))
            k //= 2
        p *= 2
    return tuple(pairs)


def _bitonic_merge_pairs(n):
    pairs = []
    d = n // 2
    while d >= 1:
        pairs += [(i, i + d) for i in range(n) if (i & d) == 0]
        d //= 2
    return tuple(pairs)


_SORT16 = _oddeven_merge_sort_pairs(PEER_TOPK)
_MERGE16 = _bitonic_merge_pairs(PEER_TOPK)
SUBLANES = 8


def _top16_rows(s):
    def exchange(x, pairs):
        for i, j in pairs:
            x[i], x[j] = jnp.maximum(x[i], x[j]), jnp.minimum(x[i], x[j])

    x = [s[SUBLANES * j:SUBLANES * (j + 1), :] for j in range(PEER_TOPK)]
    exchange(x, _SORT16)
    shift = SUBLANES // 2
    while shift >= 1:
        other = [pltpu.roll(v, shift, 0) for v in x]
        x = [jnp.maximum(x[i], other[PEER_TOPK - 1 - i]) for i in range(PEER_TOPK)]
        exchange(x, _MERGE16)
        shift //= 2
    return [v[0:1] for v in x]


def _prefix_count(pred, w):
    m8 = pred(w[7])
    m4 = pred(jnp.where(m8, w[11], w[3]))
    m2 = pred(jnp.where(m8, jnp.where(m4, w[13], w[9]), jnp.where(m4, w[5], w[1])))
    m1 = pred(jnp.where(m8, jnp.where(m4, jnp.where(m2, w[14], w[12]), jnp.where(m2, w[10], w[8])),
                        jnp.where(m4, jnp.where(m2, w[6], w[4]), jnp.where(m2, w[2], w[0]))))
    bit = lambda m, v: jnp.where(m, v, 0.0)
    return bit(m8, 8.0) + bit(m4, 4.0) + bit(m2, 2.0) + bit(m1, 1.0) + bit(pred(w[15]), 1.0)


def _top_desc(s, k, dst_ref, ts, with_rank=False):
    cur = s
    rank = jnp.full(s.shape, float(k), F32) if with_rank else None
    for r in range(k):
        m = jnp.max(cur, axis=0, keepdims=True)
        dst_ref[r:r + 1, ts] = m
        if with_rank or r + 1 < k:
            hit = cur == m
            if with_rank:
                rank = jnp.where(hit, float(r), rank)
            cur = jnp.where(hit, -jnp.inf, cur)
    return rank


def _psel_body(hnt_ref, wq_ref, sk_ref, cnt_ref, e1_ref, r2_ref, e2_ref, s_ref, v1_ref, v2_ref, vc_ref):
    hnt = hnt_ref[...]
    for h in range(PEER_HEADS):
        q = jnp.dot(wq_ref[h * 256:(h + 1) * 256, :], hnt, preferred_element_type=F32)
        s_ref[0] = jnp.dot(sk_ref[h, 0], q[:128].astype(BF16), preferred_element_type=F32)
        s_ref[1] = jnp.dot(sk_ref[h, 1], q[128:].astype(BF16), preferred_element_type=F32)
        for tc in range(hnt.shape[1] // LANES):
            ts = slice(tc * LANES, (tc + 1) * LANES)
            s1 = s_ref[0, :, ts]
            s2 = s_ref[1, :, ts]
            w1 = _top16_rows(s1)
            w2 = _top16_rows(s2)
            for b in range(PEER_TOPK):
                v1_ref[b:b + 1, ts] = w1[b]
                v2_ref[b:b + 1, ts] = w2[b]
            v1 = v1_ref[:, ts]
            v2 = v2_ref[:, ts]
            cand = jnp.concatenate([v2 + v1[0:1]] + [v2[0:8] + v1[a:a + 1] for a in range(1, 8)]
                                   + [v1[8:16] + v2[0:1]], axis=0)
            _top_desc(cand, PEER_TOPK, vc_ref, ts)
            tau = vc_ref[PEER_TOPK - 1:PEER_TOPK, ts]
            top = vc_ref[0:1, ts]
            zsum = jnp.sum(jnp.where(cand >= tau, jnp.exp(cand - top), 0.0), axis=0, keepdims=True)
            cnt_ref[h, :, ts] = _prefix_count(lambda t: s1 + t >= tau, w2)
            e1_ref[h, :, ts] = jnp.exp(s1 - v1[0:1]) / zsum
            rank2 = _prefix_count(lambda t: t > s2, w2)
            r2_ref[h, :, ts] = pltpu.bitcast(rank2.astype(BF16), I32)
            e2_ref[h, :, ts] = pltpu.bitcast(jnp.exp(s2 - v2[0:1]).astype(BF16), I32)


def _psel(hnt, wq_t, sk, tn):
    n = hnt.shape[1]
    spec = pl.BlockSpec((PEER_HEADS, PEER_NKEYS, tn), lambda i: (0, 0, i))
    shp32 = jax.ShapeDtypeStruct((PEER_HEADS, PEER_NKEYS, n), F32)
    shp16 = jax.ShapeDtypeStruct((PEER_HEADS, PEER_NKEYS // 2, n), I32)
    spec16 = pl.BlockSpec((PEER_HEADS, PEER_NKEYS // 2, tn), lambda i: (0, 0, i))
    return pl.pallas_call(
        _psel_body,
        grid=(n // tn,),
        in_specs=[pl.BlockSpec((D_MODEL, tn), lambda i: (0, i)),
                  pl.BlockSpec((PEER_HEADS * 256, D_MODEL), lambda i: (0, 0)),
                  pl.BlockSpec((PEER_HEADS, 2, PEER_NKEYS, 128), lambda i: (0, 0, 0, 0))],
        out_specs=(spec, spec, spec16, spec16),
        out_shape=(shp32, shp32, shp16, shp16),
        scratch_shapes=[pltpu.VMEM((2, PEER_NKEYS, tn), F32)] + [pltpu.VMEM((PEER_TOPK, tn), F32)] * 3,
        compiler_params=_params(("arbitrary",)),
        name="psel",
    )(hnt, wq_t, sk)


def _pdense_body(*refs, eb, nchunk, nblk):
    hnt_ref, u_ref, vt_ref = refs[:3]
    cnt_refs = refs[3:3 + nchunk]
    e1_refs = refs[3 + nchunk:3 + 2 * nchunk]
    r2_ref, e2_ref, h1_ref, g_ref, out_ref, acc_ref, a_ref, coef_ref = refs[3 + 2 * nchunk:]
    s = pl.program_id(1)
    nsub = eb // PEER_NKEYS

    def row_bf16(ref, h, i1):
        return jnp.broadcast_to(ref[h, pl.ds(i1, 1), :], (PEER_NKEYS, LANES)).astype(BF16)

    def stage_a(a_ref):
        for half in range(eb // PDENSE_ROWS):
            rows = slice(half * PDENSE_ROWS, (half + 1) * PDENSE_ROWS)
            a_ref[rows, :] = jnp.dot(u_ref[rows, :], hnt_ref[...], preferred_element_type=F32)

    def stage_b(blk, a_ref, coef_ref):
        for cp in range(0, nsub, PDENSE_GROUP):
            cs = [cp + k for k in range(PDENSE_GROUP)]
            for tc in range(nchunk):
                ts = slice(tc * LANES, (tc + 1) * LANES)
                gates = [None] * PDENSE_GROUP
                for h in range(PEER_HEADS):
                    r2 = pltpu.bitcast(r2_ref[h, :, ts], BF16)
                    e2 = pltpu.bitcast(e2_ref[h, :, ts], BF16)
                    for k in range(PDENSE_GROUP):
                        cnt = row_bf16(cnt_refs[tc], h, blk * nsub + cs[k])
                        e1 = row_bf16(e1_refs[tc], h, blk * nsub + cs[k])
                        term = jnp.where(r2 < cnt, e2, jnp.zeros((), BF16)) * e1
                        gates[k] = term if gates[k] is None else gates[k] + term
                for k in range(PDENSE_GROUP):
                    rows = slice(cs[k] * PEER_NKEYS, (cs[k] + 1) * PEER_NKEYS)
                    at = a_ref[rows, ts]
                    gelu = (0.5 * at) * (1.0 + lax.erf(at * (0.5 ** 0.5)))
                    coef_ref[rows, ts] = gelu.astype(BF16) * gates[k]

    def stage_c(coef_ref):
        acc_ref[...] += jnp.dot(vt_ref[...], coef_ref[...], preferred_element_type=F32)

    @pl.when(s == 0)
    def _():
        acc_ref[...] = jnp.zeros(acc_ref.shape, F32)

    stage_a(a_ref)
    stage_b(s, a_ref, coef_ref)
    stage_c(coef_ref)

    @pl.when(s == nblk - 1)
    def _():
        out_ref[...] = _rms(h1_ref[...] + acc_ref[...].T, g_ref[...])


def _pdense(hnt, u, vt, cnt, e1, r2, e2, h1, g, tn, eb):
    n = hnt.shape[1]
    nblk = u.shape[0] // eb
    assert nblk >= 2 and vt.shape == (nblk, D_MODEL, eb)
    nchunk = tn // LANES
    chunk = [pl.BlockSpec((PEER_HEADS, PEER_NKEYS, LANES), lambda i, s, c=c: (0, 0, i * nchunk + c))
             for c in range(nchunk)]
    sel16 = pl.BlockSpec((PEER_HEADS, PEER_NKEYS // 2, tn), lambda i, s: (0, 0, i))
    return pl.pallas_call(
        functools.partial(_pdense_body, eb=eb, nchunk=nchunk, nblk=nblk),
        grid=(n // tn, nblk),
        in_specs=[pl.BlockSpec((D_MODEL, tn), lambda i, s: (0, i)),
                  pl.BlockSpec((eb, D_MODEL), lambda i, s: (s, 0)),
                  pl.BlockSpec((None, D_MODEL, eb), lambda i, s: (s, 0, 0))]
                 + chunk + chunk
                 + [sel16, sel16,
                    pl.BlockSpec((tn, D_MODEL), lambda i, s: (i, 0)),
                    pl.BlockSpec((1, D_MODEL), lambda i, s: (0, 0))],
        out_specs=pl.BlockSpec((tn, D_MODEL), lambda i, s: (i, 0)),
        out_shape=jax.ShapeDtypeStruct((n, D_MODEL), F32),
        scratch_shapes=[pltpu.VMEM((D_MODEL, tn), F32),
                        pltpu.VMEM((eb, tn), F32), pltpu.VMEM((eb, tn), BF16)],
        compiler_params=_params(("arbitrary", "arbitrary")),
        name="pdense",
    )(hnt, u, vt, *([cnt] * nchunk), *([e1] * nchunk), r2, e2, h1, g)


def kernel(x, meta_tokens, norm_mix, w_in, w_uv, w_out, norm_ffn, peer_wq, peer_subkeys,
           peer_u, peer_v, norm_final):
    batch, seq, d = x.shape
    assert d == D_MODEL and norm_mix.shape[0] == 1 and seq % DSA_BLOCK == 0
    n = batch * seq
    topk = min(DSA_TOPK_MAX, seq // 4)
    tm = 512 if seq % 512 == 0 else 256

    w = jnp.pad(w_in[0], ((0, 0), (0, IN_WIDTH_PAD - IN_WIDTH))).astype(BF16)
    g_mix = norm_mix[0][None, :]
    x2d = x.reshape(n, d)
    pos_f = jnp.arange(seq, dtype=F32) + float(N_META)
    pos_m = jnp.arange(N_META, dtype=F32)

    rq, rk, rv, rg, dq, dk, dv, iq, misc = _proj(x2d, g_mix, w, _rope_tables(pos_f), tm, seq // tm)
    _, mk, mv, _, _, mdk, mdv, _, mmisc = _proj(meta_tokens, g_mix, w, _rope_tables(pos_m), N_META, 1)

    ret = _retention(rq, rk, rv, rg, mk, mv, batch, seq)

    def keys(frames, meta):
        width = frames.shape[-1]
        lead = jnp.concatenate([jnp.zeros((LANES - N_META, width), frames.dtype), meta], axis=0)
        lead = jnp.broadcast_to(lead[None], (batch, LANES, width))
        return jnp.concatenate([lead, frames.reshape(batch, seq, width)], axis=1)

    kpad = keys(dk, mdk)
    vpad = keys(dv, mdv)
    kipad = keys(misc[:, :IDX_DIM].astype(BF16), mmisc[:, :IDX_DIM].astype(BF16))
    att = _dsa(dq, iq, misc, kpad, vpad, kipad, w_uv[0].astype(BF16), batch, seq, topk)

    h1, hnt = _post(ret, att, x2d, w_out[0].astype(BF16), norm_ffn[0][None, :], tm)

    wq_t = peer_wq[0].reshape(d, PEER_HEADS * 256).T.astype(BF16)
    cnt, e1, r2, e2 = _psel(hnt, wq_t, peer_subkeys[0].astype(BF16), tm)
    vt = peer_v[0].reshape(-1, PDENSE_EXPERTS, d).transpose(0, 2, 1).astype(BF16)
    out = _pdense(hnt, peer_u[0].astype(BF16), vt, cnt, e1, r2, e2, h1,
                  norm_final[None, :], tm, PDENSE_EXPERTS)
    return out.reshape(batch, seq, d)
```

```python
import functools

import jax
import jax.numpy as jnp
from jax import lax
from jax.experimental import pallas as pl
from jax.experimental.pallas import tpu as pltpu

F32 = jnp.float32
BF16 = jnp.bfloat16
I32 = jnp.int32

D_MODEL = 1024
N_META = 16
CHUNK = 64
RET_HEADS = 4
RET_DIM = 128
DSA_HEADS = 8
DSA_DIM = 128
DSA_OUT_DIM = 64
IDX_HEADS = 8
IDX_DIM = 64
DSA_TOPK_MAX = 256
ROPE_THETA = 10000.0
PEER_HEADS = 8
PEER_NKEYS = 128
PEER_TOPK = 16
EPS = 1e-6
NEG = -1e30
INT_MIN = -(2 ** 31)

IN_WIDTH = 3912
IN_WIDTH_PAD = 3968
LANES = 128
RET_BLOCK = 256
PDENSE_EXPERTS = 2048
PDENSE_ROWS = 512
PDENSE_GROUP = 2
DSA_BLOCK = 256
DSA_BITS_PER_CHECK = 4
VMEM_LIMIT = 56 * 1024 * 1024

NT = (((1,), (1,)), ((), ()))
TN = (((0,), (0,)), ((), ()))


def _rms(x, g):
    return x * lax.rsqrt(jnp.mean(x * x, axis=-1, keepdims=True) + EPS) * g


def _params(sem):
    return pltpu.CompilerParams(dimension_semantics=sem, vmem_limit_bytes=VMEM_LIMIT)


def _proj_body(x_ref, g_ref, w_ref, tab_ref, rq_ref, rk_ref, rv_ref, rg_ref, dq_ref, dk_ref,
               dv_ref, iq_ref, misc_ref):
    xn = _rms(x_ref[...], g_ref[...]).astype(BF16)
    cos_a, sin_a, cos_b, sin_b1, sin_b2, cos_m, sin_m1, sin_m2 = (tab_ref[i] for i in range(8))

    def mm(lo, n):
        return jnp.dot(xn, w_ref[:, lo:lo + n], preferred_element_type=F32)

    def rope_a(y):
        return y * cos_a + pltpu.roll(y, 64, 1) * sin_a

    def rope_b(y, c, s1, s2):
        return y * c + pltpu.roll(y, 96, 1) * s1 + pltpu.roll(y, 32, 1) * s2

    y = mm(0, 512)
    for h in range(RET_HEADS):
        rq_ref[:, h * 128:(h + 1) * 128] = rope_a(y[:, h * 128:(h + 1) * 128]).astype(BF16)
    y = mm(512, 512)
    for h in range(RET_HEADS):
        rk_ref[:, h * 128:(h + 1) * 128] = (
            rope_a(y[:, h * 128:(h + 1) * 128]) * (RET_DIM ** -0.5)).astype(BF16)
    rv_ref[...] = mm(1024, 512).astype(BF16)
    rg_ref[...] = mm(1536, 512)
    y = mm(2048, 1024)
    for h in range(DSA_HEADS):
        dq_ref[h] = (rope_a(y[:, h * 128:(h + 1) * 128]) * (DSA_DIM ** -0.5)).astype(BF16)
    dk_ref[...] = rope_a(mm(3072, 128)).astype(BF16)
    dv_ref[...] = mm(3200, 128).astype(BF16)
    y = mm(3328, 512)
    for p in range(IDX_HEADS // 2):
        o = (rope_b(y[:, p * 128:(p + 1) * 128], cos_b, sin_b1, sin_b2) * (IDX_DIM ** -0.5)).astype(BF16)
        iq_ref[2 * p] = o[:, :64]
        iq_ref[2 * p + 1] = o[:, 64:]
    misc_ref[...] = rope_b(mm(3840, 128), cos_m, sin_m1, sin_m2)


def _proj(x2d, g, w, tab, tm, tab_blocks):
    n = x2d.shape[0]
    row = lambda i: (i, 0)
    head = lambda i: (0, i, 0)
    out_shape = (
        jax.ShapeDtypeStruct((n, 512), BF16), jax.ShapeDtypeStruct((n, 512), BF16),
        jax.ShapeDtypeStruct((n, 512), BF16), jax.ShapeDtypeStruct((n, 512), F32),
        jax.ShapeDtypeStruct((DSA_HEADS, n, 128), BF16), jax.ShapeDtypeStruct((n, 128), BF16),
        jax.ShapeDtypeStruct((n, 128), BF16), jax.ShapeDtypeStruct((IDX_HEADS, n, 64), BF16),
        jax.ShapeDtypeStruct((n, 128), F32))
    out_specs = (
        pl.BlockSpec((tm, 512), row), pl.BlockSpec((tm, 512), row), pl.BlockSpec((tm, 512), row),
        pl.BlockSpec((tm, 512), row), pl.BlockSpec((DSA_HEADS, tm, 128), head),
        pl.BlockSpec((tm, 128), row), pl.BlockSpec((tm, 128), row),
        pl.BlockSpec((IDX_HEADS, tm, 64), head), pl.BlockSpec((tm, 128), row))
    return pl.pallas_call(
        _proj_body,
        grid=(n // tm,),
        in_specs=[pl.BlockSpec((tm, D_MODEL), row),
                  pl.BlockSpec((1, D_MODEL), lambda i: (0, 0)),
                  pl.BlockSpec((D_MODEL, IN_WIDTH_PAD), lambda i: (0, 0)),
                  pl.BlockSpec((8, tm, 128), lambda i: (0, i % tab_blocks, 0))],
        out_specs=out_specs,
        out_shape=out_shape,
        compiler_params=_params(("arbitrary",)),
        name="proj",
    )(x2d, g, w, tab)


def _rope_tables(pos):
    p = pos.shape[0]
    inv = ROPE_THETA ** (-jnp.arange(0, 64, dtype=F32) * 2.0 / 128)
    ang = pos[:, None] * inv[None, :]
    c, s = jnp.cos(ang), jnp.sin(ang)
    cos_a = jnp.concatenate([c, c], -1)
    sin_a = jnp.concatenate([-s, s], -1)
    inv = ROPE_THETA ** (-jnp.arange(0, 32, dtype=F32) * 2.0 / 64)
    ang = pos[:, None] * inv[None, :]
    c, s = jnp.cos(ang), jnp.sin(ang)
    z = jnp.zeros_like(s)
    z64 = jnp.zeros((p, 64), F32)
    cos_b = jnp.concatenate([c, c, c, c], -1)
    sin_b1 = jnp.concatenate([-s, z, -s, z], -1)
    sin_b2 = jnp.concatenate([z, s, z, s], -1)
    cos_m = jnp.concatenate([c, c, jnp.full((p, 64), IDX_HEADS ** -0.5, F32)], -1)
    sin_m1 = jnp.concatenate([-s, z, z64], -1)
    sin_m2 = jnp.concatenate([z, s, z64], -1)
    return jnp.stack([cos_a, sin_a, cos_b, sin_b1, sin_b2, cos_m, sin_m1, sin_m2])


def _ret_body(rq_ref, rk_ref, rv_ref, rg_ref, mk_ref, mv_ref, dmat_ref, xi_ref, zeta_ref,
              mzeta_ref, gsc_ref, out_ref, r_ref):
    @pl.when(pl.program_id(1) == 0)
    def _():
        for h in range(RET_HEADS):
            hs = slice(h * 128, (h + 1) * 128)
            kz = (mk_ref[:, hs].astype(F32) * mzeta_ref[h]).astype(BF16)
            r_ref[h] = lax.dot_general(kz, mv_ref[:, hs], TN, preferred_element_type=F32)

    for h in range(RET_HEADS):
        hs = slice(h * 128, (h + 1) * 128)
        q = rq_ref[:, hs]
        k = rk_ref[:, hs]
        v = rv_ref[:, hs]
        s = lax.dot_general(q, k, NT, preferred_element_type=F32) * dmat_ref[h]
        o = jnp.dot(s.astype(BF16), v, preferred_element_type=F32)
        r = r_ref[h]
        qx = (q.astype(F32) * xi_ref[h]).astype(BF16)
        o = o + jnp.dot(qx, r.astype(BF16), preferred_element_type=F32)
        kz = (k.astype(F32) * zeta_ref[h]).astype(BF16)
        u = lax.dot_general(kz, v, TN, preferred_element_type=F32)
        r_ref[h] = r * gsc_ref[h] + u
        mu = jnp.mean(o, axis=-1, keepdims=True)
        d = o - mu
        var = jnp.mean(d * d, axis=-1, keepdims=True)
        on = d * lax.rsqrt(var + EPS)
        g = rg_ref[:, hs]
        out_ref[:, hs] = (g * jax.nn.sigmoid(g) * on).astype(BF16)


def _retention(rq, rk, rv, rg, mk, mv, batch, seq):
    n = rq.shape[0]
    nb = seq // RET_BLOCK
    lg = jnp.log(1.0 - 2.0 ** (-5.0 - jnp.arange(RET_HEADS, dtype=F32)))
    i = jnp.arange(RET_BLOCK, dtype=F32)
    ci = jnp.arange(RET_BLOCK) // CHUNK
    vis = (ci[None, :] <= ci[:, None])
    dmat = jnp.where(vis[None], jnp.exp(lg[:, None, None] * jnp.abs(i[:, None] - i[None, :])), 0.0)
    xi = jnp.broadcast_to(jnp.exp(lg[:, None] * (i + 1.0)[None, :])[:, :, None], (RET_HEADS, RET_BLOCK, 128))
    zeta = jnp.broadcast_to(jnp.exp(lg[:, None] * (RET_BLOCK - 1.0 - i)[None, :])[:, :, None],
                            (RET_HEADS, RET_BLOCK, 128))
    im = jnp.arange(N_META, dtype=F32)
    mzeta = jnp.broadcast_to(jnp.exp(lg[:, None] * (N_META - 1.0 - im)[None, :])[:, :, None],
                             (RET_HEADS, N_META, 128))
    gsc = jnp.broadcast_to(jnp.exp(lg * RET_BLOCK)[:, None, None], (RET_HEADS, 1, 128))
    row = lambda b, s: (b * nb + s, 0)
    c2 = lambda b, s: (0, 0)
    c3 = lambda b, s: (0, 0, 0)
    return pl.pallas_call(
        _ret_body,
        grid=(batch, nb),
        in_specs=[pl.BlockSpec((RET_BLOCK, 512), row), pl.BlockSpec((RET_BLOCK, 512), row),
                  pl.BlockSpec((RET_BLOCK, 512), row), pl.BlockSpec((RET_BLOCK, 512), row),
                  pl.BlockSpec((N_META, 512), c2), pl.BlockSpec((N_META, 512), c2),
                  pl.BlockSpec((RET_HEADS, RET_BLOCK, RET_BLOCK), c3),
                  pl.BlockSpec((RET_HEADS, RET_BLOCK, 128), c3),
                  pl.BlockSpec((RET_HEADS, RET_BLOCK, 128), c3),
                  pl.BlockSpec((RET_HEADS, N_META, 128), c3),
                  pl.BlockSpec((RET_HEADS, 1, 128), c3)],
        out_specs=pl.BlockSpec((RET_BLOCK, 512), row),
        out_shape=jax.ShapeDtypeStruct((n, 512), BF16),
        scratch_shapes=[pltpu.VMEM((RET_HEADS, 128, 128), F32)],
        compiler_params=_params(("arbitrary", "arbitrary")),
        name="retention",
    )(rq, rk, rv, rg, mk, mv, dmat, xi, zeta, mzeta, gsc)


def _aligned(off):
    return off if isinstance(off, int) else pl.multiple_of(off, LANES)


def _dsa_body(qa_ref, qi_ref, misc_ref, k_ref, v_ref, ki_ref, wuv_ref, out_ref,
              isc_ref, bias_ref, m_ref, l_ref, acc_ref, *, topk):
    qb = pl.program_id(1)
    tq = DSA_BLOCK
    nh = DSA_HEADS

    wt = misc_ref[...].T
    qi = qi_ref[...].reshape(IDX_HEADS * tq, IDX_DIM)

    def isc_tile(koff, width, allowed):
        kt = ki_ref[pl.ds(koff, width), :]
        z = lax.dot_general(kt, qi, NT, preferred_element_type=F32)
        isc = None
        for h in range(IDX_HEADS):
            term = jnp.maximum(z[:, h * tq:(h + 1) * tq], 0.0) * wt[64 + h:65 + h, :]
            isc = term if isc is None else isc + term
        if allowed is not None:
            isc = jnp.where(allowed, isc, -jnp.inf)
        isc_ref[pl.ds(koff, width), :] = isc

    isc_tile(0, LANES, lax.broadcasted_iota(I32, (LANES, tq), 0) >= LANES - N_META)

    def body_a(t, c):
        isc_tile(pl.multiple_of(LANES + t * tq, LANES), tq, None)
        return c

    lax.fori_loop(0, qb, body_a, 0)
    keyc = lax.broadcasted_iota(I32, (tq, tq), 0) // CHUNK
    qryc = lax.broadcasted_iota(I32, (tq, tq), 1) // CHUNK
    diag_off = pl.multiple_of(LANES + qb * tq, LANES)
    isc_tile(diag_off, tq, keyc <= qryc)

    kf = float(topk)
    sub = 8

    def key_to_f32(key):
        return lax.bitcast_convert_type(jnp.where(key < 0, key ^ jnp.int32(0x7FFFFFFF), key), F32)

    def count(pred):
        def slab(off, a):
            hit = jnp.where(pred(isc_ref[pl.ds(off, LANES), :], off), 1.0, 0.0)
            return a + jnp.sum(hit.reshape(LANES // sub, sub, tq), axis=0)

        def body(t, a):
            off = pl.multiple_of(LANES + t * (2 * LANES), LANES)
            return slab(pl.multiple_of(off + LANES, LANES), slab(off, a))

        a = lax.fori_loop(0, qb + 1, body, slab(0, jnp.zeros((sub, tq), F32)))
        return jnp.sum(a, axis=0, keepdims=True)

    def count_ge(cand):
        return count(lambda x, off: x >= cand)

    lowest = jnp.full((1, tq), -3.0e38, F32)
    n_real = count_ge(lowest)
    c_pos = count_ge(jnp.zeros((1, tq), F32))
    pos = c_pos >= kf
    base0 = jnp.where(pos, jnp.int32(0), jnp.int32(INT_MIN))
    cnt0 = jnp.where(n_real <= kf, kf, jnp.where(pos, c_pos, n_real))

    def unfinished(cnt):
        return jnp.max(jnp.where(cnt != kf, 1.0, 0.0))

    def try_bit(bit, base, cnt):
        cand = base | lax.shift_left(jnp.int32(1), bit)
        c = count_ge(key_to_f32(cand))
        ok = c >= kf
        return jnp.where(ok, cand, base), jnp.where(ok, c, cnt)

    base, cnt = base0, cnt0
    for bit in range(30, DSA_BITS_PER_CHECK * 7 - 1, -1):
        base, cnt = try_bit(bit, base, cnt)

    zero_tie = jnp.logical_and(pos, count(lambda x, off: x > 0.0) < kf)

    def group_cond(state):
        g, _, cnt = state
        return jnp.logical_and(g >= 0, unfinished(jnp.where(zero_tie, kf, cnt)) > 0.0)

    def group_body(state):
        g, base, cnt = state
        for k in range(DSA_BITS_PER_CHECK - 1, -1, -1):
            base, cnt = try_bit(g * DSA_BITS_PER_CHECK + k, base, cnt)
        return g - 1, base, cnt

    _, base, cnt = lax.while_loop(group_cond, group_body, (jnp.int32(6), base, cnt))

    def refine_cond(state):
        it, _, _, _, go = state
        return jnp.logical_and(it < 40, go > 0.0)

    def refine_body(state):
        it, lo, hi, cnt, _ = state
        mid = lo + 0.5 * (hi - lo)
        c = count_ge(mid)
        ok = c >= kf
        cnt = jnp.where(ok, c, cnt)
        moving = jnp.logical_and(jnp.logical_and(mid > lo, mid < hi),
                                 jnp.logical_and(cnt != kf, jnp.logical_not(zero_tie)))
        return it + 1, jnp.where(ok, mid, lo), jnp.where(ok, hi, mid), cnt, jnp.max(jnp.where(moving, 1.0, 0.0))

    _, thr, _, cnt, _ = lax.while_loop(
        refine_cond, refine_body,
        (jnp.int32(0), key_to_f32(base), key_to_f32(base + 1), cnt, unfinished(jnp.where(zero_tie, kf, cnt))))
    thr = jnp.where(n_real <= kf, lowest, thr)

    need = kf - count(lambda x, off: x > thr)
    lower = (lax.broadcasted_iota(I32, (tq, tq), 1) <= lax.broadcasted_iota(I32, (tq, tq), 0)).astype(BF16)

    def mask_tile(off, width, ties_before):
        x = isc_ref[pl.ds(off, width), :]
        tie = x == thr
        tie_f = jnp.where(tie, 1.0, 0.0)
        upto = jnp.dot(lower[:width, :width], tie_f.astype(BF16), preferred_element_type=F32)
        keep = jnp.logical_or(x > thr, jnp.logical_and(tie, ties_before + upto - tie_f < need))
        bias_ref[:, pl.ds(off, width)] = jnp.where(keep, 0.0, NEG).T
        return ties_before + upto[width - 1:width, :]

    def body_m(t, ties_before):
        return mask_tile(pl.multiple_of(LANES + t * tq, LANES), tq, ties_before)

    lax.fori_loop(0, qb + 1, body_m, mask_tile(0, LANES, jnp.zeros((1, tq), F32)))

    m_ref[...] = jnp.full(m_ref.shape, NEG, F32)
    l_ref[...] = jnp.zeros(l_ref.shape, F32)
    acc_ref[...] = jnp.zeros(acc_ref.shape, F32)
    qa = qa_ref[...].reshape(nh * tq, DSA_DIM)

    def att_tile(koff, width):
        kt = k_ref[pl.ds(koff, width), :]
        vt = v_ref[pl.ds(koff, width), :]
        s = lax.dot_general(qa, kt, NT, preferred_element_type=F32).reshape(nh, tq, width)
        reps = width // LANES
        s = (s + bias_ref[:, pl.ds(koff, width)][None]).reshape(nh * tq, width)
        m_prev = m_ref[...]
        m_new = jnp.maximum(m_prev, jnp.max(s, axis=1, keepdims=True))
        alpha = jnp.exp(m_prev - m_new)
        m_w = m_new if reps == 1 else jnp.concatenate([m_new] * reps, axis=1)
        p = jnp.exp(s - m_w)
        l_ref[...] = alpha * l_ref[...] + jnp.sum(p, axis=1, keepdims=True)
        acc_ref[...] = alpha * acc_ref[...] + jnp.dot(p.astype(BF16), vt, preferred_element_type=F32)
        m_ref[...] = m_new

    att_tile(0, LANES)

    def body_c(t, c):
        att_tile(pl.multiple_of(LANES + t * tq, LANES), tq)
        return c

    lax.fori_loop(0, qb + 1, body_c, 0)

    o = (acc_ref[...] / l_ref[...]).astype(BF16).reshape(nh, tq, DSA_DIM)
    for h in range(nh):
        out_ref[:, h * DSA_OUT_DIM:(h + 1) * DSA_OUT_DIM] = jnp.dot(
            o[h], wuv_ref[h], preferred_element_type=F32).astype(BF16)


def _dsa(dq, iq, misc, kpad, vpad, kipad, wuv, batch, seq, topk):
    n = misc.shape[0]
    nq = seq // DSA_BLOCK
    tp = kpad.shape[1]
    head = lambda b, q: (0, b * nq + q, 0)
    row = lambda b, q: (b * nq + q, 0)
    per_b = lambda b, q: (b, 0, 0)
    return pl.pallas_call(
        functools.partial(_dsa_body, topk=topk),
        grid=(batch, nq),
        in_specs=[pl.BlockSpec((DSA_HEADS, DSA_BLOCK, DSA_DIM), head),
                  pl.BlockSpec((IDX_HEADS, DSA_BLOCK, IDX_DIM), head),
                  pl.BlockSpec((DSA_BLOCK, 128), row),
                  pl.BlockSpec((None, tp, DSA_DIM), per_b),
                  pl.BlockSpec((None, tp, DSA_DIM), per_b),
                  pl.BlockSpec((None, tp, IDX_DIM), per_b),
                  pl.BlockSpec((DSA_HEADS, DSA_DIM, DSA_OUT_DIM), lambda b, q: (0, 0, 0))],
        out_specs=pl.BlockSpec((DSA_BLOCK, 512), row),
        out_shape=jax.ShapeDtypeStruct((n, 512), BF16),
        scratch_shapes=[pltpu.VMEM((tp, DSA_BLOCK), F32),
                        pltpu.VMEM((DSA_BLOCK, tp), F32),
                        pltpu.VMEM((DSA_HEADS * DSA_BLOCK, LANES), F32),
                        pltpu.VMEM((DSA_HEADS * DSA_BLOCK, LANES), F32),
                        pltpu.VMEM((DSA_HEADS * DSA_BLOCK, DSA_DIM), F32)],
        compiler_params=_params(("arbitrary", "arbitrary")),
        name="dsa",
    )(dq, iq, misc, kpad, vpad, kipad, wuv)


def _post_body(ret_ref, att_ref, x_ref, w_ref, g_ref, h1_ref, hnt_ref):
    y = jnp.dot(ret_ref[...], w_ref[0:512, :], preferred_element_type=F32)
    y = y + jnp.dot(att_ref[...], w_ref[512:1024, :], preferred_element_type=F32)
    h1 = x_ref[...] + y
    h1_ref[...] = h1
    hnt_ref[...] = _rms(h1, g_ref[...]).T.astype(BF16)


def _post(ret, att, x2d, w_out, g, tm):
    n = x2d.shape[0]
    row = lambda i: (i, 0)
    return pl.pallas_call(
        _post_body,
        grid=(n // tm,),
        in_specs=[pl.BlockSpec((tm, 512), row), pl.BlockSpec((tm, 512), row),
                  pl.BlockSpec((tm, D_MODEL), row),
                  pl.BlockSpec((D_MODEL, D_MODEL), lambda i: (0, 0)),
                  pl.BlockSpec((1, D_MODEL), lambda i: (0, 0))],
        out_specs=(pl.BlockSpec((tm, D_MODEL), row), pl.BlockSpec((D_MODEL, tm), lambda i: (0, i))),
        out_shape=(jax.ShapeDtypeStruct((n, D_MODEL), F32), jax.ShapeDtypeStruct((D_MODEL, n), BF16)),
        compiler_params=_params(("arbitrary",)),
        name="post",
    )(ret, att, x2d, w_out, g)


def _oddeven_merge_sort_pairs(n):
    pairs = []
    p = 1
    while p < n:
        k = p
        while k >= 1:
            for j in range(k % p, n - k, 2 * k):
                for i in range(min(k, n - j - k)):
                    if (i + j) // (2 * p) == (i + j + k) // (2 * p):
                        pairs.append((i + j, i + j + k))
            k //= 2
        p *= 2
    return tuple(pairs)


def _bitonic_merge_pairs(n):
    pairs = []
    d = n // 2
    while d >= 1:
        pairs += [(i, i + d) for i in range(n) if (i & d) == 0]
        d //= 2
    return tuple(pairs)


_SORT16 = _oddeven_merge_sort_pairs(PEER_TOPK)
_MERGE16 = _bitonic_merge_pairs(PEER_TOPK)
SUBLANES = 8


def _top16_rows(s):
    def exchange(x, pairs):
        for i, j in pairs:
            x[i], x[j] = jnp.maximum(x[i], x[j]), jnp.minimum(x[i], x[j])

    x = [s[SUBLANES * j:SUBLANES * (j + 1), :] for j in range(PEER_TOPK)]
    exchange(x, _SORT16)
    shift = SUBLANES // 2
    while shift >= 1:
        other = [pltpu.roll(v, shift, 0) for v in x]
        x = [jnp.maximum(x[i], other[PEER_TOPK - 1 - i]) for i in range(PEER_TOPK)]
        exchange(x, _MERGE16)
        shift //= 2
    return [v[0:1] for v in x]


def _prefix_count(pred, w):
    m8 = pred(w[7])
    m4 = pred(jnp.where(m8, w[11], w[3]))
    m2 = pred(jnp.where(m8, jnp.where(m4, w[13], w[9]), jnp.where(m4, w[5], w[1])))
    m1 = pred(jnp.where(m8, jnp.where(m4, jnp.where(m2, w[14], w[12]), jnp.where(m2, w[10], w[8])),
                        jnp.where(m4, jnp.where(m2, w[6], w[4]), jnp.where(m2, w[2], w[0]))))
    bit = lambda m, v: jnp.where(m, v, 0.0)
    return bit(m8, 8.0) + bit(m4, 4.0) + bit(m2, 2.0) + bit(m1, 1.0) + bit(pred(w[15]), 1.0)


def _top_desc(s, k, dst_ref, ts, with_rank=False):
    cur = s
    rank = jnp.full(s.shape, float(k), F32) if with_rank else None
    for r in range(k):
        m = jnp.max(cur, axis=0, keepdims=True)
        dst_ref[r:r + 1, ts] = m
        if with_rank or r + 1 < k:
            hit = cur == m
            if with_rank:
                rank = jnp.where(hit, float(r), rank)
            cur = jnp.where(hit, -jnp.inf, cur)
    return rank


def _psel_body(hnt_ref, wq_ref, sk_ref, cnt_ref, e1_ref, r2_ref, e2_ref, s_ref, v1_ref, v2_ref, vc_ref):
    hnt = hnt_ref[...]
    for h in range(PEER_HEADS):
        q = jnp.dot(wq_ref[h * 256:(h + 1) * 256, :], hnt, preferred_element_type=F32)
        s_ref[0] = jnp.dot(sk_ref[h, 0], q[:128].astype(BF16), preferred_element_type=F32)
        s_ref[1] = jnp.dot(sk_ref[h, 1], q[128:].astype(BF16), preferred_element_type=F32)
        for tc in range(hnt.shape[1] // LANES):
            ts = slice(tc * LANES, (tc + 1) * LANES)
            s1 = s_ref[0, :, ts]
            s2 = s_ref[1, :, ts]
            w1 = _top16_rows(s1)
            w2 = _top16_rows(s2)
            for b in range(PEER_TOPK):
                v1_ref[b:b + 1, ts] = w1[b]
                v2_ref[b:b + 1, ts] = w2[b]
            v1 = v1_ref[:, ts]
            v2 = v2_ref[:, ts]
            cand = jnp.concatenate([v2 + v1[0:1]] + [v2[0:8] + v1[a:a + 1] for a in range(1, 8)]
                                   + [v1[8:16] + v2[0:1]], axis=0)
            _top_desc(cand, PEER_TOPK, vc_ref, ts)
            tau = vc_ref[PEER_TOPK - 1:PEER_TOPK, ts]
            top = vc_ref[0:1, ts]
            zsum = jnp.sum(jnp.where(cand >= tau, jnp.exp(cand - top), 0.0), axis=0, keepdims=True)
            cnt_ref[h, :, ts] = _prefix_count(lambda t: s1 + t >= tau, w2)
            e1_ref[h, :, ts] = jnp.exp(s1 - v1[0:1]) / zsum
            rank2 = _prefix_count(lambda t: t > s2, w2)
            r2_ref[h, :, ts] = pltpu.bitcast(rank2.astype(BF16), I32)
            e2_ref[h, :, ts] = pltpu.bitcast(jnp.exp(s2 - v2[0:1]).astype(BF16), I32)


def _psel(hnt, wq_t, sk, tn):
    n = hnt.shape[1]
    spec = pl.BlockSpec((PEER_HEADS, PEER_NKEYS, tn), lambda i: (0, 0, i))
    shp32 = jax.ShapeDtypeStruct((PEER_HEADS, PEER_NKEYS, n), F32)
    shp16 = jax.ShapeDtypeStruct((PEER_HEADS, PEER_NKEYS // 2, n), I32)
    spec16 = pl.BlockSpec((PEER_HEADS, PEER_NKEYS // 2, tn), lambda i: (0, 0, i))
    return pl.pallas_call(
        _psel_body,
        grid=(n // tn,),
        in_specs=[pl.BlockSpec((D_MODEL, tn), lambda i: (0, i)),
                  pl.BlockSpec((PEER_HEADS * 256, D_MODEL), lambda i: (0, 0)),
                  pl.BlockSpec((PEER_HEADS, 2, PEER_NKEYS, 128), lambda i: (0, 0, 0, 0))],
        out_specs=(spec, spec, spec16, spec16),
        out_shape=(shp32, shp32, shp16, shp16),
        scratch_shapes=[pltpu.VMEM((2, PEER_NKEYS, tn), F32)] + [pltpu.VMEM((PEER_TOPK, tn), F32)] * 3,
        compiler_params=_params(("arbitrary",)),
        name="psel",
    )(hnt, wq_t, sk)


def _pdense_body(*refs, eb, nchunk, nblk):
    hnt_ref, u_ref, vt_ref = refs[:3]
    cnt_refs = refs[3:3 + nchunk]
    e1_refs = refs[3 + nchunk:3 + 2 * nchunk]
    r2_ref, e2_ref, h1_ref, g_ref, out_ref, acc_ref, a_ref, coef_ref = refs[3 + 2 * nchunk:]
    s = pl.program_id(1)
    nsub = eb // PEER_NKEYS

    def row_bf16(ref, h, i1):
        return jnp.broadcast_to(ref[h, pl.ds(i1, 1), :], (PEER_NKEYS, LANES)).astype(BF16)

    def stage_a(a_ref):
        for half in range(eb // PDENSE_ROWS):
            rows = slice(half * PDENSE_ROWS, (half + 1) * PDENSE_ROWS)
            a_ref[rows, :] = jnp.dot(u_ref[rows, :], hnt_ref[...], preferred_element_type=F32)

    def stage_b(blk, a_ref, coef_ref):
        for cp in range(0, nsub, PDENSE_GROUP):
            cs = [cp + k for k in range(PDENSE_GROUP)]
            for tc in range(nchunk):
                ts = slice(tc * LANES, (tc + 1) * LANES)
                gates = [None] * PDENSE_GROUP
                for h in range(PEER_HEADS):
                    r2 = pltpu.bitcast(r2_ref[h, :, ts], BF16)
                    e2 = pltpu.bitcast(e2_ref[h, :, ts], BF16)
                    for k in range(PDENSE_GROUP):
                        cnt = row_bf16(cnt_refs[tc], h, blk * nsub + cs[k])
                        e1 = row_bf16(e1_refs[tc], h, blk * nsub + cs[k])
                        term = jnp.where(r2 < cnt, e2, jnp.zeros((), BF16)) * e1
                        gates[k] = term if gates[k] is None else gates[k] + term
                for k in range(PDENSE_GROUP):
                    rows = slice(cs[k] * PEER_NKEYS, (cs[k] + 1) * PEER_NKEYS)
                    at = a_ref[rows, ts]
                    gelu = (0.5 * at) * (1.0 + lax.erf(at * (0.5 ** 0.5)))
                    coef_ref[rows, ts] = gelu.astype(BF16) * gates[k]

    def stage_c(coef_ref):
        acc_ref[...] += jnp.dot(vt_ref[...], coef_ref[...], preferred_element_type=F32)

    @pl.when(s == 0)
    def _():
        acc_ref[...] = jnp.zeros(acc_ref.shape, F32)

    stage_a(a_ref)
    stage_b(s, a_ref, coef_ref)
    stage_c(coef_ref)

    @pl.when(s == nblk - 1)
    def _():
        out_ref[...] = _rms(h1_ref[...] + acc_ref[...].T, g_ref[...])


def _pdense(hnt, u, vt, cnt, e1, r2, e2, h1, g, tn, eb):
    n = hnt.shape[1]
    nblk = u.shape[0] // eb
    assert nblk >= 2 and vt.shape == (nblk, D_MODEL, eb)
    nchunk = tn // LANES
    chunk = [pl.BlockSpec((PEER_HEADS, PEER_NKEYS, LANES), lambda i, s, c=c: (0, 0, i * nchunk + c))
             for c in range(nchunk)]
    sel16 = pl.BlockSpec((PEER_HEADS, PEER_NKEYS // 2, tn), lambda i, s: (0, 0, i))
    return pl.pallas_call(
        functools.partial(_pdense_body, eb=eb, nchunk=nchunk, nblk=nblk),
        grid=(n // tn, nblk),
        in_specs=[pl.BlockSpec((D_MODEL, tn), lambda i, s: (0, i)),
                  pl.BlockSpec((eb, D_MODEL), lambda i, s: (s, 0)),
                  pl.BlockSpec((None, D_MODEL, eb), lambda i, s: (s, 0, 0))]
                 + chunk + chunk
                 + [sel16, sel16,
                    pl.BlockSpec((tn, D_MODEL), lambda i, s: (i, 0)),
                    pl.BlockSpec((1, D_MODEL), lambda i, s: (0, 0))],
        out_specs=pl.BlockSpec((tn, D_MODEL), lambda i, s: (i, 0)),
        out_shape=jax.ShapeDtypeStruct((n, D_MODEL), F32),
        scratch_shapes=[pltpu.VMEM((D_MODEL, tn), F32),
                        pltpu.VMEM((eb, tn), F32), pltpu.VMEM((eb, tn), BF16)],
        compiler_params=_params(("arbitrary", "arbitrary")),
        name="pdense",
    )(hnt, u, vt, *([cnt] * nchunk), *([e1] * nchunk), r2, e2, h1, g)


def kernel(x, meta_tokens, norm_mix, w_in, w_uv, w_out, norm_ffn, peer_wq, peer_subkeys,
           peer_u, peer_v, norm_final):
    batch, seq, d = x.shape
    assert d == D_MODEL and norm_mix.shape[0] == 1 and seq % DSA_BLOCK == 0
    n = batch * seq
    topk = min(DSA_TOPK_MAX, seq // 4)
    tm = 512 if seq % 512 == 0 else 256

    w = jnp.pad(w_in[0], ((0, 0), (0, IN_WIDTH_PAD - IN_WIDTH))).astype(BF16)
    g_mix = norm_mix[0][None, :]
    x2d = x.reshape(n, d)
    pos_f = jnp.arange(seq, dtype=F32) + float(N_META)
    pos_m = jnp.arange(N_META, dtype=F32)

    rq, rk, rv, rg, dq, dk, dv, iq, misc = _proj(x2d, g_mix, w, _rope_tables(pos_f), tm, seq // tm)
    _, mk, mv, _, _, mdk, mdv, _, mmisc = _proj(meta_tokens, g_mix, w, _rope_tables(pos_m), N_META, 1)

    ret = _retention(rq, rk, rv, rg, mk, mv, batch, seq)

    def keys(frames, meta):
        width = frames.shape[-1]
        lead = jnp.concatenate([jnp.zeros((LANES - N_META, width), frames.dtype), meta], axis=0)
        lead = jnp.broadcast_to(lead[None], (batch, LANES, width))
        return jnp.concatenate([lead, frames.reshape(batch, seq, width)], axis=1)

    kpad = keys(dk, mdk)
    vpad = keys(dv, mdv)
    kipad = keys(misc[:, :IDX_DIM].astype(BF16), mmisc[:, :IDX_DIM].astype(BF16))
    att = _dsa(dq, iq, misc, kpad, vpad, kipad, w_uv[0].astype(BF16), batch, seq, topk)

    h1, hnt = _post(ret, att, x2d, w_out[0].astype(BF16), norm_ffn[0][None, :], tm)

    wq_t = peer_wq[0].reshape(d, PEER_HEADS * 256).T.astype(BF16)
    cnt, e1, r2, e2 = _psel(hnt, wq_t, peer_subkeys[0].astype(BF16), tm)
    vt = peer_v[0].reshape(-1, PDENSE_EXPERTS, d).transpose(0, 2, 1).astype(BF16)
    out = _pdense(hnt, peer_u[0].astype(BF16), vt, cnt, e1, r2, e2, h1,
                  norm_final[None, :], tm, PDENSE_EXPERTS)
    return out.reshape(batch, seq, d)
```

```python
import functools

import jax
import jax.numpy as jnp
from jax import lax
from jax.experimental import pallas as pl
from jax.experimental.pallas import tpu as pltpu

F32 = jnp.float32
BF16 = jnp.bfloat16
I32 = jnp.int32

D_MODEL = 1024
N_META = 16
CHUNK = 64
RET_HEADS = 4
RET_DIM = 128
DSA_HEADS = 8
DSA_DIM = 128
DSA_OUT_DIM = 64
IDX_HEADS = 8
IDX_DIM = 64
DSA_TOPK_MAX = 256
ROPE_THETA = 10000.0
PEER_HEADS = 8
PEER_NKEYS = 128
PEER_TOPK = 16
EPS = 1e-6
NEG = -1e30
LOG2E = 1.4426950408889634
INT_MIN = -(2 ** 31)

IN_WIDTH = 3912
IN_WIDTH_PAD = 3968
LANES = 128
RET_BLOCK = 256
PDENSE_EXPERTS = 2048
PDENSE_ROWS = 512
PDENSE_GROUP = 2
DSA_BLOCK = 256
DSA_BITS_PER_CHECK = 4
VMEM_LIMIT = 56 * 1024 * 1024

NT = (((1,), (1,)), ((), ()))
TN = (((0,), (0,)), ((), ()))


def _rms(x, g):
    return x * lax.rsqrt(jnp.mean(x * x, axis=-1, keepdims=True) + EPS) * g


def _params(sem):
    return pltpu.CompilerParams(dimension_semantics=sem, vmem_limit_bytes=VMEM_LIMIT)


def _proj_body(x_ref, g_ref, w_ref, tab_ref, rq_ref, rk_ref, rv_ref, rg_ref, dq_ref, dk_ref,
               dv_ref, iq_ref, misc_ref):
    xn = _rms(x_ref[...], g_ref[...]).astype(BF16)
    cos_a, sin_a, cos_b, sin_b1, sin_b2, cos_m, sin_m1, sin_m2 = (tab_ref[i] for i in range(8))

    def mm(lo, n):
        return jnp.dot(xn, w_ref[:, lo:lo + n], preferred_element_type=F32)

    def rope_a(y):
        return y * cos_a + pltpu.roll(y, 64, 1) * sin_a

    def rope_b(y, c, s1, s2):
        return y * c + pltpu.roll(y, 96, 1) * s1 + pltpu.roll(y, 32, 1) * s2

    y = mm(0, 512)
    for h in range(RET_HEADS):
        rq_ref[:, h * 128:(h + 1) * 128] = rope_a(y[:, h * 128:(h + 1) * 128]).astype(BF16)
    y = mm(512, 512)
    for h in range(RET_HEADS):
        rk_ref[:, h * 128:(h + 1) * 128] = (
            rope_a(y[:, h * 128:(h + 1) * 128]) * (RET_DIM ** -0.5)).astype(BF16)
    rv_ref[...] = mm(1024, 512).astype(BF16)
    rg_ref[...] = mm(1536, 512)
    y = mm(2048, 1024)
    for h in range(DSA_HEADS):
        dq_ref[h] = (rope_a(y[:, h * 128:(h + 1) * 128]) * (DSA_DIM ** -0.5 * LOG2E)).astype(BF16)
    dk_ref[...] = rope_a(mm(3072, 128)).astype(BF16)
    dv_ref[...] = mm(3200, 128).astype(BF16)
    y = mm(3328, 512)
    for p in range(IDX_HEADS // 2):
        o = (rope_b(y[:, p * 128:(p + 1) * 128], cos_b, sin_b1, sin_b2) * (IDX_DIM ** -0.5)).astype(BF16)
        iq_ref[2 * p] = o[:, :64]
        iq_ref[2 * p + 1] = o[:, 64:]
    misc_ref[...] = rope_b(mm(3840, 128), cos_m, sin_m1, sin_m2)


def _proj(x2d, g, w, tab, tm, tab_blocks):
    n = x2d.shape[0]
    row = lambda i: (i, 0)
    head = lambda i: (0, i, 0)
    out_shape = (
        jax.ShapeDtypeStruct((n, 512), BF16), jax.ShapeDtypeStruct((n, 512), BF16),
        jax.ShapeDtypeStruct((n, 512), BF16), jax.ShapeDtypeStruct((n, 512), F32),
        jax.ShapeDtypeStruct((DSA_HEADS, n, 128), BF16), jax.ShapeDtypeStruct((n, 128), BF16),
        jax.ShapeDtypeStruct((n, 128), BF16), jax.ShapeDtypeStruct((IDX_HEADS, n, 64), BF16),
        jax.ShapeDtypeStruct((n, 128), F32))
    out_specs = (
        pl.BlockSpec((tm, 512), row), pl.BlockSpec((tm, 512), row), pl.BlockSpec((tm, 512), row),
        pl.BlockSpec((tm, 512), row), pl.BlockSpec((DSA_HEADS, tm, 128), head),
        pl.BlockSpec((tm, 128), row), pl.BlockSpec((tm, 128), row),
        pl.BlockSpec((IDX_HEADS, tm, 64), head), pl.BlockSpec((tm, 128), row))
    return pl.pallas_call(
        _proj_body,
        grid=(n // tm,),
        in_specs=[pl.BlockSpec((tm, D_MODEL), row),
                  pl.BlockSpec((1, D_MODEL), lambda i: (0, 0)),
                  pl.BlockSpec((D_MODEL, IN_WIDTH_PAD), lambda i: (0, 0)),
                  pl.BlockSpec((8, tm, 128), lambda i: (0, i % tab_blocks, 0))],
        out_specs=out_specs,
        out_shape=out_shape,
        compiler_params=_params(("arbitrary",)),
        name="proj",
    )(x2d, g, w, tab)


def _rope_tables(pos):
    p = pos.shape[0]
    inv = ROPE_THETA ** (-jnp.arange(0, 64, dtype=F32) * 2.0 / 128)
    ang = pos[:, None] * inv[None, :]
    c, s = jnp.cos(ang), jnp.sin(ang)
    cos_a = jnp.concatenate([c, c], -1)
    sin_a = jnp.concatenate([-s, s], -1)
    inv = ROPE_THETA ** (-jnp.arange(0, 32, dtype=F32) * 2.0 / 64)
    ang = pos[:, None] * inv[None, :]
    c, s = jnp.cos(ang), jnp.sin(ang)
    z = jnp.zeros_like(s)
    z64 = jnp.zeros((p, 64), F32)
    cos_b = jnp.concatenate([c, c, c, c], -1)
    sin_b1 = jnp.concatenate([-s, z, -s, z], -1)
    sin_b2 = jnp.concatenate([z, s, z, s], -1)
    cos_m = jnp.concatenate([c, c, jnp.full((p, 64), IDX_HEADS ** -0.5, F32)], -1)
    sin_m1 = jnp.concatenate([-s, z, z64], -1)
    sin_m2 = jnp.concatenate([z, s, z64], -1)
    return jnp.stack([cos_a, sin_a, cos_b, sin_b1, sin_b2, cos_m, sin_m1, sin_m2])


def _ret_body(rq_ref, rk_ref, rv_ref, rg_ref, mk_ref, mv_ref, dmat_ref, xi_ref, zeta_ref,
              mzeta_ref, gsc_ref, out_ref, r_ref):
    @pl.when(pl.program_id(1) == 0)
    def _():
        for h in range(RET_HEADS):
            hs = slice(h * 128, (h + 1) * 128)
            kz = (mk_ref[:, hs].astype(F32) * mzeta_ref[h]).astype(BF16)
            r_ref[h] = lax.dot_general(kz, mv_ref[:, hs], TN, preferred_element_type=F32)

    for h in range(RET_HEADS):
        hs = slice(h * 128, (h + 1) * 128)
        q = rq_ref[:, hs]
        k = rk_ref[:, hs]
        v = rv_ref[:, hs]
        s = lax.dot_general(q, k, NT, preferred_element_type=F32) * dmat_ref[h]
        o = jnp.dot(s.astype(BF16), v, preferred_element_type=F32)
        r = r_ref[h]
        qx = (q.astype(F32) * xi_ref[h]).astype(BF16)
        o = o + jnp.dot(qx, r.astype(BF16), preferred_element_type=F32)
        kz = (k.astype(F32) * zeta_ref[h]).astype(BF16)
        u = lax.dot_general(kz, v, TN, preferred_element_type=F32)
        r_ref[h] = r * gsc_ref[h] + u
        mu = jnp.mean(o, axis=-1, keepdims=True)
        d = o - mu
        var = jnp.mean(d * d, axis=-1, keepdims=True)
        on = d * lax.rsqrt(var + EPS)
        g = rg_ref[:, hs]
        out_ref[:, hs] = (g * jax.nn.sigmoid(g) * on).astype(BF16)


def _retention(rq, rk, rv, rg, mk, mv, batch, seq):
    n = rq.shape[0]
    nb = seq // RET_BLOCK
    lg = jnp.log(1.0 - 2.0 ** (-5.0 - jnp.arange(RET_HEADS, dtype=F32)))
    i = jnp.arange(RET_BLOCK, dtype=F32)
    ci = jnp.arange(RET_BLOCK) // CHUNK
    vis = (ci[None, :] <= ci[:, None])
    dmat = jnp.where(vis[None], jnp.exp(lg[:, None, None] * jnp.abs(i[:, None] - i[None, :])), 0.0)
    xi = jnp.broadcast_to(jnp.exp(lg[:, None] * (i + 1.0)[None, :])[:, :, None], (RET_HEADS, RET_BLOCK, 128))
    zeta = jnp.broadcast_to(jnp.exp(lg[:, None] * (RET_BLOCK - 1.0 - i)[None, :])[:, :, None],
                            (RET_HEADS, RET_BLOCK, 128))
    im = jnp.arange(N_META, dtype=F32)
    mzeta = jnp.broadcast_to(jnp.exp(lg[:, None] * (N_META - 1.0 - im)[None, :])[:, :, None],
                             (RET_HEADS, N_META, 128))
    gsc = jnp.broadcast_to(jnp.exp(lg * RET_BLOCK)[:, None, None], (RET_HEADS, 1, 128))
    row = lambda b, s: (b * nb + s, 0)
    c2 = lambda b, s: (0, 0)
    c3 = lambda b, s: (0, 0, 0)
    return pl.pallas_call(
        _ret_body,
        grid=(batch, nb),
        in_specs=[pl.BlockSpec((RET_BLOCK, 512), row), pl.BlockSpec((RET_BLOCK, 512), row),
                  pl.BlockSpec((RET_BLOCK, 512), row), pl.BlockSpec((RET_BLOCK, 512), row),
                  pl.BlockSpec((N_META, 512), c2), pl.BlockSpec((N_META, 512), c2),
                  pl.BlockSpec((RET_HEADS, RET_BLOCK, RET_BLOCK), c3),
                  pl.BlockSpec((RET_HEADS, RET_BLOCK, 128), c3),
                  pl.BlockSpec((RET_HEADS, RET_BLOCK, 128), c3),
                  pl.BlockSpec((RET_HEADS, N_META, 128), c3),
                  pl.BlockSpec((RET_HEADS, 1, 128), c3)],
        out_specs=pl.BlockSpec((RET_BLOCK, 512), row),
        out_shape=jax.ShapeDtypeStruct((n, 512), BF16),
        scratch_shapes=[pltpu.VMEM((RET_HEADS, 128, 128), F32)],
        compiler_params=_params(("arbitrary", "arbitrary")),
        name="retention",
    )(rq, rk, rv, rg, mk, mv, dmat, xi, zeta, mzeta, gsc)


def _aligned(off):
    return off if isinstance(off, int) else pl.multiple_of(off, LANES)


def _dsa_body(qa_ref, qi_ref, misc_ref, k_ref, v_ref, ki_ref, wuv_ref, out_ref,
              isc_ref, bias_ref, m_ref, l_ref, acc_ref, *, topk):
    qb = pl.program_id(1)
    tq = DSA_BLOCK
    nh = DSA_HEADS

    wt = misc_ref[...].T
    qi = qi_ref[...].reshape(IDX_HEADS * tq, IDX_DIM)

    def isc_tile(koff, width, allowed):
        kt = ki_ref[pl.ds(koff, width), :]
        z = lax.dot_general(kt, qi, NT, preferred_element_type=F32)
        isc = None
        for h in range(IDX_HEADS):
            term = jnp.maximum(z[:, h * tq:(h + 1) * tq], 0.0) * wt[64 + h:65 + h, :]
            isc = term if isc is None else isc + term
        if allowed is not None:
            isc = jnp.where(allowed, isc, -jnp.inf)
        isc_ref[pl.ds(koff, width), :] = isc

    isc_tile(0, LANES, lax.broadcasted_iota(I32, (LANES, tq), 0) >= LANES - N_META)

    def body_a(t, c):
        isc_tile(pl.multiple_of(LANES + t * tq, LANES), tq, None)
        return c

    lax.fori_loop(0, qb, body_a, 0)
    keyc = lax.broadcasted_iota(I32, (tq, tq), 0) // CHUNK
    qryc = lax.broadcasted_iota(I32, (tq, tq), 1) // CHUNK
    diag_off = pl.multiple_of(LANES + qb * tq, LANES)
    isc_tile(diag_off, tq, keyc <= qryc)

    kf = float(topk)
    sub = 8

    def key_to_f32(key):
        return lax.bitcast_convert_type(jnp.where(key < 0, key ^ jnp.int32(0x7FFFFFFF), key), F32)

    def count(pred):
        def slab(off, a):
            hit = jnp.where(pred(isc_ref[pl.ds(off, LANES), :], off), 1.0, 0.0)
            return a + jnp.sum(hit.reshape(LANES // sub, sub, tq), axis=0)

        def body(t, a):
            off = pl.multiple_of(LANES + t * (2 * LANES), LANES)
            return slab(pl.multiple_of(off + LANES, LANES), slab(off, a))

        a = lax.fori_loop(0, qb + 1, body, slab(0, jnp.zeros((sub, tq), F32)))
        return jnp.sum(a, axis=0, keepdims=True)

    def count_ge(cand):
        return count(lambda x, off: x >= cand)

    lowest = jnp.full((1, tq), -3.0e38, F32)
    n_real = count_ge(lowest)
    c_pos = count_ge(jnp.zeros((1, tq), F32))
    pos = c_pos >= kf
    base0 = jnp.where(pos, jnp.int32(0), jnp.int32(INT_MIN))
    cnt0 = jnp.where(n_real <= kf, kf, jnp.where(pos, c_pos, n_real))

    def unfinished(cnt):
        return jnp.max(jnp.where(cnt != kf, 1.0, 0.0))

    def try_bit(bit, base, cnt):
        cand = base | lax.shift_left(jnp.int32(1), bit)
        c = count_ge(key_to_f32(cand))
        ok = c >= kf
        return jnp.where(ok, cand, base), jnp.where(ok, c, cnt)

    base, cnt = base0, cnt0
    for bit in range(30, DSA_BITS_PER_CHECK * 7 - 1, -1):
        base, cnt = try_bit(bit, base, cnt)

    zero_tie = jnp.logical_and(pos, count(lambda x, off: x > 0.0) < kf)

    def group_cond(state):
        g, _, cnt = state
        return jnp.logical_and(g >= 0, unfinished(jnp.where(zero_tie, kf, cnt)) > 0.0)

    def group_body(state):
        g, base, cnt = state
        for k in range(DSA_BITS_PER_CHECK - 1, -1, -1):
            base, cnt = try_bit(g * DSA_BITS_PER_CHECK + k, base, cnt)
        return g - 1, base, cnt

    _, base, cnt = lax.while_loop(group_cond, group_body, (jnp.int32(6), base, cnt))

    def refine_cond(state):
        it, _, _, _, go = state
        return jnp.logical_and(it < 40, go > 0.0)

    def refine_body(state):
        it, lo, hi, cnt, _ = state
        mid = lo + 0.5 * (hi - lo)
        c = count_ge(mid)
        ok = c >= kf
        cnt = jnp.where(ok, c, cnt)
        moving = jnp.logical_and(jnp.logical_and(mid > lo, mid < hi),
                                 jnp.logical_and(cnt != kf, jnp.logical_not(zero_tie)))
        return it + 1, jnp.where(ok, mid, lo), jnp.where(ok, hi, mid), cnt, jnp.max(jnp.where(moving, 1.0, 0.0))

    _, thr, _, cnt, _ = lax.while_loop(
        refine_cond, refine_body,
        (jnp.int32(0), key_to_f32(base), key_to_f32(base + 1), cnt, unfinished(jnp.where(zero_tie, kf, cnt))))
    thr = jnp.where(n_real <= kf, lowest, thr)

    need = kf - count(lambda x, off: x > thr)
    lower = (lax.broadcasted_iota(I32, (tq, tq), 1) <= lax.broadcasted_iota(I32, (tq, tq), 0)).astype(BF16)

    def mask_tile(off, width, ties_before):
        x = isc_ref[pl.ds(off, width), :]
        tie = x == thr
        tie_f = jnp.where(tie, 1.0, 0.0)
        upto = jnp.dot(lower[:width, :width], tie_f.astype(BF16), preferred_element_type=F32)
        keep = jnp.logical_or(x > thr, jnp.logical_and(tie, ties_before + upto - tie_f < need))
        bias_ref[:, pl.ds(off, width)] = jnp.where(keep, 0.0, NEG).T
        return ties_before + upto[width - 1:width, :]

    def body_m(t, ties_before):
        return mask_tile(pl.multiple_of(LANES + t * tq, LANES), tq, ties_before)

    lax.fori_loop(0, qb + 1, body_m, mask_tile(0, LANES, jnp.zeros((1, tq), F32)))

    m_ref[...] = jnp.full(m_ref.shape, NEG, F32)
    l_ref[...] = jnp.zeros(l_ref.shape, F32)
    acc_ref[...] = jnp.zeros(acc_ref.shape, F32)
    qa = qa_ref[...].reshape(nh * tq, DSA_DIM)

    def att_tile(koff, width):
        kt = k_ref[pl.ds(koff, width), :]
        vt = v_ref[pl.ds(koff, width), :]
        s = lax.dot_general(qa, kt, NT, preferred_element_type=F32).reshape(nh, tq, width)
        reps = width // LANES
        s = (s + bias_ref[:, pl.ds(koff, width)][None]).reshape(nh * tq, width)
        m_prev = m_ref[...]
        m_new = jnp.maximum(m_prev, jnp.max(s, axis=1, keepdims=True))
        alpha = jnp.exp2(m_prev - m_new)
        m_w = m_new if reps == 1 else jnp.concatenate([m_new] * reps, axis=1)
        p = jnp.exp2(s - m_w)
        l_ref[...] = alpha * l_ref[...] + jnp.sum(p, axis=1, keepdims=True)
        acc_ref[...] = alpha * acc_ref[...] + jnp.dot(p.astype(BF16), vt, preferred_element_type=F32)
        m_ref[...] = m_new

    att_tile(0, LANES)

    def body_c(t, c):
        att_tile(pl.multiple_of(LANES + t * tq, LANES), tq)
        return c

    lax.fori_loop(0, qb + 1, body_c, 0)

    o = (acc_ref[...] / l_ref[...]).astype(BF16).reshape(nh, tq, DSA_DIM)
    for h in range(nh):
        out_ref[:, h * DSA_OUT_DIM:(h + 1) * DSA_OUT_DIM] = jnp.dot(
            o[h], wuv_ref[h], preferred_element_type=F32).astype(BF16)


def _dsa(dq, iq, misc, kpad, vpad, kipad, wuv, batch, seq, topk):
    n = misc.shape[0]
    nq = seq // DSA_BLOCK
    tp = kpad.shape[1]
    head = lambda b, q: (0, b * nq + q, 0)
    row = lambda b, q: (b * nq + q, 0)
    per_b = lambda b, q: (b, 0, 0)
    return pl.pallas_call(
        functools.partial(_dsa_body, topk=topk),
        grid=(batch, nq),
        in_specs=[pl.BlockSpec((DSA_HEADS, DSA_BLOCK, DSA_DIM), head),
                  pl.BlockSpec((IDX_HEADS, DSA_BLOCK, IDX_DIM), head),
                  pl.BlockSpec((DSA_BLOCK, 128), row),
                  pl.BlockSpec((None, tp, DSA_DIM), per_b),
                  pl.BlockSpec((None, tp, DSA_DIM), per_b),
                  pl.BlockSpec((None, tp, IDX_DIM), per_b),
                  pl.BlockSpec((DSA_HEADS, DSA_DIM, DSA_OUT_DIM), lambda b, q: (0, 0, 0))],
        out_specs=pl.BlockSpec((DSA_BLOCK, 512), row),
        out_shape=jax.ShapeDtypeStruct((n, 512), BF16),
        scratch_shapes=[pltpu.VMEM((tp, DSA_BLOCK), F32),
                        pltpu.VMEM((DSA_BLOCK, tp), F32),
                        pltpu.VMEM((DSA_HEADS * DSA_BLOCK, LANES), F32),
                        pltpu.VMEM((DSA_HEADS * DSA_BLOCK, LANES), F32),
                        pltpu.VMEM((DSA_HEADS * DSA_BLOCK, DSA_DIM), F32)],
        compiler_params=_params(("arbitrary", "arbitrary")),
        name="dsa",
    )(dq, iq, misc, kpad, vpad, kipad, wuv)


def _post_body(ret_ref, att_ref, x_ref, w_ref, g_ref, h1_ref, hnt_ref):
    y = jnp.dot(ret_ref[...], w_ref[0:512, :], preferred_element_type=F32)
    y = y + jnp.dot(att_ref[...], w_ref[512:1024, :], preferred_element_type=F32)
    h1 = x_ref[...] + y
    h1_ref[...] = h1
    hnt_ref[...] = _rms(h1, g_ref[...]).T.astype(BF16)


def _post(ret, att, x2d, w_out, g, tm):
    n = x2d.shape[0]
    row = lambda i: (i, 0)
    return pl.pallas_call(
        _post_body,
        grid=(n // tm,),
        in_specs=[pl.BlockSpec((tm, 512), row), pl.BlockSpec((tm, 512), row),
                  pl.BlockSpec((tm, D_MODEL), row),
                  pl.BlockSpec((D_MODEL, D_MODEL), lambda i: (0, 0)),
                  pl.BlockSpec((1, D_MODEL), lambda i: (0, 0))],
        out_specs=(pl.BlockSpec((tm, D_MODEL), row), pl.BlockSpec((D_MODEL, tm), lambda i: (0, i))),
        out_shape=(jax.ShapeDtypeStruct((n, D_MODEL), F32), jax.ShapeDtypeStruct((D_MODEL, n), BF16)),
        compiler_params=_params(("arbitrary",)),
        name="post",
    )(ret, att, x2d, w_out, g)


def _oddeven_merge_sort_pairs(n):
    pairs = []
    p = 1
    while p < n:
        k = p
        while k >= 1:
            for j in range(k % p, n - k, 2 * k):
                for i in range(min(k, n - j - k)):
                    if (i + j) // (2 * p) == (i + j + k) // (2 * p):
                        pairs.append((i + j, i + j + k))
            k //= 2
        p *= 2
    return tuple(pairs)


def _bitonic_merge_pairs(n):
    pairs = []
    d = n // 2
    while d >= 1:
        pairs += [(i, i + d) for i in range(n) if (i & d) == 0]
        d //= 2
    return tuple(pairs)


_SORT16 = _oddeven_merge_sort_pairs(PEER_TOPK)
_MERGE16 = _bitonic_merge_pairs(PEER_TOPK)
SUBLANES = 8


def _top16_rows(s):
    def exchange(x, pairs):
        for i, j in pairs:
            x[i], x[j] = jnp.maximum(x[i], x[j]), jnp.minimum(x[i], x[j])

    x = [s[SUBLANES * j:SUBLANES * (j + 1), :] for j in range(PEER_TOPK)]
    exchange(x, _SORT16)
    shift = SUBLANES // 2
    while shift >= 1:
        other = [pltpu.roll(v, shift, 0) for v in x]
        x = [jnp.maximum(x[i], other[PEER_TOPK - 1 - i]) for i in range(PEER_TOPK)]
        exchange(x, _MERGE16)
        shift //= 2
    return [v[0:1] for v in x]


def _prefix_count(pred, w):
    m8 = pred(w[7])
    m4 = pred(jnp.where(m8, w[11], w[3]))
    m2 = pred(jnp.where(m8, jnp.where(m4, w[13], w[9]), jnp.where(m4, w[5], w[1])))
    m1 = pred(jnp.where(m8, jnp.where(m4, jnp.where(m2, w[14], w[12]), jnp.where(m2, w[10], w[8])),
                        jnp.where(m4, jnp.where(m2, w[6], w[4]), jnp.where(m2, w[2], w[0]))))
    bit = lambda m, v: jnp.where(m, v, 0.0)
    return bit(m8, 8.0) + bit(m4, 4.0) + bit(m2, 2.0) + bit(m1, 1.0) + bit(pred(w[15]), 1.0)


def _top_desc(s, k, dst_ref, ts, with_rank=False):
    cur = s
    rank = jnp.full(s.shape, float(k), F32) if with_rank else None
    for r in range(k):
        m = jnp.max(cur, axis=0, keepdims=True)
        dst_ref[r:r + 1, ts] = m
        if with_rank or r + 1 < k:
            hit = cur == m
            if with_rank:
                rank = jnp.where(hit, float(r), rank)
            cur = jnp.where(hit, -jnp.inf, cur)
    return rank


def _psel_body(hnt_ref, wq_ref, sk_ref, cnt_ref, e1_ref, r2_ref, e2_ref, s_ref, v1_ref, v2_ref, vc_ref):
    hnt = hnt_ref[...]
    for h in range(PEER_HEADS):
        q = jnp.dot(wq_ref[h * 256:(h + 1) * 256, :], hnt, preferred_element_type=F32)
        s_ref[0] = jnp.dot(sk_ref[h, 0], q[:128].astype(BF16), preferred_element_type=F32)
        s_ref[1] = jnp.dot(sk_ref[h, 1], q[128:].astype(BF16), preferred_element_type=F32)
        for tc in range(hnt.shape[1] // LANES):
            ts = slice(tc * LANES, (tc + 1) * LANES)
            s1 = s_ref[0, :, ts]
            s2 = s_ref[1, :, ts]
            w1 = _top16_rows(s1)
            w2 = _top16_rows(s2)
            for b in range(PEER_TOPK):
                v1_ref[b:b + 1, ts] = w1[b]
                v2_ref[b:b + 1, ts] = w2[b]
            v1 = v1_ref[:, ts]
            v2 = v2_ref[:, ts]
            cand = jnp.concatenate([v2 + v1[0:1]] + [v2[0:8] + v1[a:a + 1] for a in range(1, 8)]
                                   + [v1[8:16] + v2[0:1]], axis=0)
            _top_desc(cand, PEER_TOPK, vc_ref, ts)
            tau = vc_ref[PEER_TOPK - 1:PEER_TOPK, ts]
            top = vc_ref[0:1, ts]
            zsum = jnp.sum(jnp.where(cand >= tau, jnp.exp(cand - top), 0.0), axis=0, keepdims=True)
            cnt_ref[h, :, ts] = _prefix_count(lambda t: s1 + t >= tau, w2)
            e1_ref[h, :, ts] = jnp.exp(s1 - v1[0:1]) / zsum
            rank2 = _prefix_count(lambda t: t > s2, w2)
            r2_ref[h, :, ts] = pltpu.bitcast(rank2.astype(BF16), I32)
            e2_ref[h, :, ts] = pltpu.bitcast(jnp.exp(s2 - v2[0:1]).astype(BF16), I32)


def _psel(hnt, wq_t, sk, tn):
    n = hnt.shape[1]
    spec = pl.BlockSpec((PEER_HEADS, PEER_NKEYS, tn), lambda i: (0, 0, i))
    shp32 = jax.ShapeDtypeStruct((PEER_HEADS, PEER_NKEYS, n), F32)
    shp16 = jax.ShapeDtypeStruct((PEER_HEADS, PEER_NKEYS // 2, n), I32)
    spec16 = pl.BlockSpec((PEER_HEADS, PEER_NKEYS // 2, tn), lambda i: (0, 0, i))
    return pl.pallas_call(
        _psel_body,
        grid=(n // tn,),
        in_specs=[pl.BlockSpec((D_MODEL, tn), lambda i: (0, i)),
                  pl.BlockSpec((PEER_HEADS * 256, D_MODEL), lambda i: (0, 0)),
                  pl.BlockSpec((PEER_HEADS, 2, PEER_NKEYS, 128), lambda i: (0, 0, 0, 0))],
        out_specs=(spec, spec, spec16, spec16),
        out_shape=(shp32, shp32, shp16, shp16),
        scratch_shapes=[pltpu.VMEM((2, PEER_NKEYS, tn), F32)] + [pltpu.VMEM((PEER_TOPK, tn), F32)] * 3,
        compiler_params=_params(("arbitrary",)),
        name="psel",
    )(hnt, wq_t, sk)


def _pdense_body(*refs, eb, nchunk, nblk):
    hnt_ref, u_ref, vt_ref = refs[:3]
    cnt_refs = refs[3:3 + nchunk]
    e1_refs = refs[3 + nchunk:3 + 2 * nchunk]
    r2_ref, e2_ref, h1_ref, g_ref, out_ref, acc_ref, a_ref, coef_ref = refs[3 + 2 * nchunk:]
    s = pl.program_id(1)
    nsub = eb // PEER_NKEYS

    def row_bf16(ref, h, i1):
        return jnp.broadcast_to(ref[h, pl.ds(i1, 1), :], (PEER_NKEYS, LANES)).astype(BF16)

    def stage_a(a_ref):
        for half in range(eb // PDENSE_ROWS):
            rows = slice(half * PDENSE_ROWS, (half + 1) * PDENSE_ROWS)
            a_ref[rows, :] = jnp.dot(u_ref[rows, :], hnt_ref[...], preferred_element_type=F32)

    def stage_b(blk, a_ref, coef_ref):
        for cp in range(0, nsub, PDENSE_GROUP):
            cs = [cp + k for k in range(PDENSE_GROUP)]
            for tc in range(nchunk):
                ts = slice(tc * LANES, (tc + 1) * LANES)
                gates = [None] * PDENSE_GROUP
                for h in range(PEER_HEADS):
                    r2 = pltpu.bitcast(r2_ref[h, :, ts], BF16)
                    e2 = pltpu.bitcast(e2_ref[h, :, ts], BF16)
                    for k in range(PDENSE_GROUP):
                        cnt = row_bf16(cnt_refs[tc], h, blk * nsub + cs[k])
                        e1 = row_bf16(e1_refs[tc], h, blk * nsub + cs[k])
                        term = jnp.where(r2 < cnt, e2, jnp.zeros((), BF16)) * e1
                        gates[k] = term if gates[k] is None else gates[k] + term
                for k in range(PDENSE_GROUP):
                    rows = slice(cs[k] * PEER_NKEYS, (cs[k] + 1) * PEER_NKEYS)
                    at = a_ref[rows, ts]
                    gelu = (0.5 * at) * (1.0 + lax.erf(at * (0.5 ** 0.5)))
                    coef_ref[rows, ts] = gelu.astype(BF16) * gates[k]

    def stage_c(coef_ref):
        acc_ref[...] += jnp.dot(vt_ref[...], coef_ref[...], preferred_element_type=F32)

    @pl.when(s == 0)
    def _():
        acc_ref[...] = jnp.zeros(acc_ref.shape, F32)

    stage_a(a_ref)
    stage_b(s, a_ref, coef_ref)
    stage_c(coef_ref)

    @pl.when(s == nblk - 1)
    def _():
        out_ref[...] = _rms(h1_ref[...] + acc_ref[...].T, g_ref[...])


def _pdense(hnt, u, vt, cnt, e1, r2, e2, h1, g, tn, eb):
    n = hnt.shape[1]
    nblk = u.shape[0] // eb
    assert nblk >= 2 and vt.shape == (nblk, D_MODEL, eb)
    nchunk = tn // LANES
    chunk = [pl.BlockSpec((PEER_HEADS, PEER_NKEYS, LANES), lambda i, s, c=c: (0, 0, i * nchunk + c))
             for c in range(nchunk)]
    sel16 = pl.BlockSpec((PEER_HEADS, PEER_NKEYS // 2, tn), lambda i, s: (0, 0, i))
    return pl.pallas_call(
        functools.partial(_pdense_body, eb=eb, nchunk=nchunk, nblk=nblk),
        grid=(n // tn, nblk),
        in_specs=[pl.BlockSpec((D_MODEL, tn), lambda i, s: (0, i)),
                  pl.BlockSpec((eb, D_MODEL), lambda i, s: (s, 0)),
                  pl.BlockSpec((None, D_MODEL, eb), lambda i, s: (s, 0, 0))]
                 + chunk + chunk
                 + [sel16, sel16,
                    pl.BlockSpec((tn, D_MODEL), lambda i, s: (i, 0)),
                    pl.BlockSpec((1, D_MODEL), lambda i, s: (0, 0))],
        out_specs=pl.BlockSpec((tn, D_MODEL), lambda i, s: (i, 0)),
        out_shape=jax.ShapeDtypeStruct((n, D_MODEL), F32),
        scratch_shapes=[pltpu.VMEM((D_MODEL, tn), F32),
                        pltpu.VMEM((eb, tn), F32), pltpu.VMEM((eb, tn), BF16)],
        compiler_params=_params(("arbitrary", "arbitrary")),
        name="pdense",
    )(hnt, u, vt, *([cnt] * nchunk), *([e1] * nchunk), r2, e2, h1, g)


def kernel(x, meta_tokens, norm_mix, w_in, w_uv, w_out, norm_ffn, peer_wq, peer_subkeys,
           peer_u, peer_v, norm_final):
    batch, seq, d = x.shape
    assert d == D_MODEL and norm_mix.shape[0] == 1 and seq % DSA_BLOCK == 0
    n = batch * seq
    topk = min(DSA_TOPK_MAX, seq // 4)
    tm = 512 if seq % 512 == 0 else 256

    w = jnp.pad(w_in[0], ((0, 0), (0, IN_WIDTH_PAD - IN_WIDTH))).astype(BF16)
    g_mix = norm_mix[0][None, :]
    x2d = x.reshape(n, d)
    pos_f = jnp.arange(seq, dtype=F32) + float(N_META)
    pos_m = jnp.arange(N_META, dtype=F32)

    rq, rk, rv, rg, dq, dk, dv, iq, misc = _proj(x2d, g_mix, w, _rope_tables(pos_f), tm, seq // tm)
    _, mk, mv, _, _, mdk, mdv, _, mmisc = _proj(meta_tokens, g_mix, w, _rope_tables(pos_m), N_META, 1)

    ret = _retention(rq, rk, rv, rg, mk, mv, batch, seq)

    def keys(frames, meta):
        width = frames.shape[-1]
        lead = jnp.concatenate([jnp.zeros((LANES - N_META, width), frames.dtype), meta], axis=0)
        lead = jnp.broadcast_to(lead[None], (batch, LANES, width))
        return jnp.concatenate([lead, frames.reshape(batch, seq, width)], axis=1)

    kpad = keys(dk, mdk)
    vpad = keys(dv, mdv)
    kipad = keys(misc[:, :IDX_DIM].astype(BF16), mmisc[:, :IDX_DIM].astype(BF16))
    att = _dsa(dq, iq, misc, kpad, vpad, kipad, w_uv[0].astype(BF16), batch, seq, topk)

    h1, hnt = _post(ret, att, x2d, w_out[0].astype(BF16), norm_ffn[0][None, :], tm)

    wq_t = peer_wq[0].reshape(d, PEER_HEADS * 256).T.astype(BF16)
    cnt, e1, r2, e2 = _psel(hnt, wq_t, peer_subkeys[0].astype(BF16), tm)
    vt = peer_v[0].reshape(-1, PDENSE_EXPERTS, d).transpose(0, 2, 1).astype(BF16)
    out = _pdense(hnt, peer_u[0].astype(BF16), vt, cnt, e1, r2, e2, h1,
                  norm_final[None, :], tm, PDENSE_EXPERTS)
    return out.reshape(batch, seq, d)
```

```python
import functools

import jax
import jax.numpy as jnp
from jax import lax
from jax.experimental import pallas as pl
from jax.experimental.pallas import tpu as pltpu

F32 = jnp.float32
BF16 = jnp.bfloat16
I32 = jnp.int32

D_MODEL = 1024
N_META = 16
CHUNK = 64
RET_HEADS = 4
RET_DIM = 128
DSA_HEADS = 8
DSA_DIM = 128
DSA_OUT_DIM = 64
IDX_HEADS = 8
IDX_DIM = 64
DSA_TOPK_MAX = 256
ROPE_THETA = 10000.0
PEER_HEADS = 8
PEER_NKEYS = 128
PEER_TOPK = 16
EPS = 1e-6
NEG = -1e30
LOG2E = 1.4426950408889634
INT_MIN = -(2 ** 31)

IN_WIDTH = 3912
IN_WIDTH_PAD = 3968
LANES = 128
RET_BLOCK = 256
PDENSE_EXPERTS = 2048
PDENSE_ROWS = 512
DSA_BLOCK = 256
DSA_KEY_BITS = 31
DSA_BITS_PER_CHECK = 4
DSA_REFINE_STEPS = 40
SUBLANES = 8
TOKEN_TILE = 512
V7X_VMEM_BYTES = 64 * 1024 * 1024
VMEM_LIMIT = V7X_VMEM_BYTES * 7 // 8

NT = (((1,), (1,)), ((), ()))
TN = (((0,), (0,)), ((), ()))


def _rms(x, g):
    return x * lax.rsqrt(jnp.mean(x * x, axis=-1, keepdims=True) + EPS) * g


def _params(sem):
    return pltpu.CompilerParams(dimension_semantics=sem, vmem_limit_bytes=VMEM_LIMIT)


def _proj_body(x_ref, g_ref, w_ref, tab_ref, rq_ref, rk_ref, rv_ref, rg_ref, dq_ref, dk_ref,
               dv_ref, iq_ref, misc_ref):
    xn = _rms(x_ref[...], g_ref[...]).astype(BF16)
    cos_a, sin_a, cos_b, sin_b1, sin_b2, cos_m, sin_m1, sin_m2 = (tab_ref[i] for i in range(8))

    def mm(lo, n):
        return jnp.dot(xn, w_ref[:, lo:lo + n], preferred_element_type=F32)

    def rope_a(y):
        return y * cos_a + pltpu.roll(y, 64, 1) * sin_a

    def rope_b(y, c, s1, s2):
        return y * c + pltpu.roll(y, 96, 1) * s1 + pltpu.roll(y, 32, 1) * s2

    y = mm(0, 512)
    for h in range(RET_HEADS):
        rq_ref[:, h * 128:(h + 1) * 128] = rope_a(y[:, h * 128:(h + 1) * 128]).astype(BF16)
    y = mm(512, 512)
    for h in range(RET_HEADS):
        rk_ref[:, h * 128:(h + 1) * 128] = (
            rope_a(y[:, h * 128:(h + 1) * 128]) * (RET_DIM ** -0.5)).astype(BF16)
    rv_ref[...] = mm(1024, 512).astype(BF16)
    rg_ref[...] = mm(1536, 512)
    y = mm(2048, 1024)
    for h in range(DSA_HEADS):
        dq_ref[h] = (rope_a(y[:, h * 128:(h + 1) * 128]) * (DSA_DIM ** -0.5 * LOG2E)).astype(BF16)
    dk_ref[...] = rope_a(mm(3072, 128)).astype(BF16)
    dv_ref[...] = mm(3200, 128).astype(BF16)
    y = mm(3328, 512)
    for p in range(IDX_HEADS // 2):
        o = (rope_b(y[:, p * 128:(p + 1) * 128], cos_b, sin_b1, sin_b2) * (IDX_DIM ** -0.5)).astype(BF16)
        iq_ref[2 * p] = o[:, :64]
        iq_ref[2 * p + 1] = o[:, 64:]
    misc_ref[...] = rope_b(mm(3840, 128), cos_m, sin_m1, sin_m2)


def _proj(x2d, g, w, tab, tm, tab_blocks):
    n = x2d.shape[0]
    row = lambda i: (i, 0)
    head = lambda i: (0, i, 0)
    out_shape = (
        jax.ShapeDtypeStruct((n, 512), BF16), jax.ShapeDtypeStruct((n, 512), BF16),
        jax.ShapeDtypeStruct((n, 512), BF16), jax.ShapeDtypeStruct((n, 512), F32),
        jax.ShapeDtypeStruct((DSA_HEADS, n, 128), BF16), jax.ShapeDtypeStruct((n, 128), BF16),
        jax.ShapeDtypeStruct((n, 128), BF16), jax.ShapeDtypeStruct((IDX_HEADS, n, 64), BF16),
        jax.ShapeDtypeStruct((n, 128), F32))
    out_specs = (
        pl.BlockSpec((tm, 512), row), pl.BlockSpec((tm, 512), row), pl.BlockSpec((tm, 512), row),
        pl.BlockSpec((tm, 512), row), pl.BlockSpec((DSA_HEADS, tm, 128), head),
        pl.BlockSpec((tm, 128), row), pl.BlockSpec((tm, 128), row),
        pl.BlockSpec((IDX_HEADS, tm, 64), head), pl.BlockSpec((tm, 128), row))
    return pl.pallas_call(
        _proj_body,
        grid=(n // tm,),
        in_specs=[pl.BlockSpec((tm, D_MODEL), row),
                  pl.BlockSpec((1, D_MODEL), lambda i: (0, 0)),
                  pl.BlockSpec((D_MODEL, IN_WIDTH_PAD), lambda i: (0, 0)),
                  pl.BlockSpec((8, tm, 128), lambda i: (0, i % tab_blocks, 0))],
        out_specs=out_specs,
        out_shape=out_shape,
        compiler_params=_params(("arbitrary",)),
        name="proj",
    )(x2d, g, w, tab)


def _rope_tables(pos):
    p = pos.shape[0]
    inv = ROPE_THETA ** (-jnp.arange(0, 64, dtype=F32) * 2.0 / 128)
    ang = pos[:, None] * inv[None, :]
    c, s = jnp.cos(ang), jnp.sin(ang)
    cos_a = jnp.concatenate([c, c], -1)
    sin_a = jnp.concatenate([-s, s], -1)
    inv = ROPE_THETA ** (-jnp.arange(0, 32, dtype=F32) * 2.0 / 64)
    ang = pos[:, None] * inv[None, :]
    c, s = jnp.cos(ang), jnp.sin(ang)
    z = jnp.zeros_like(s)
    z64 = jnp.zeros((p, 64), F32)
    cos_b = jnp.concatenate([c, c, c, c], -1)
    sin_b1 = jnp.concatenate([-s, z, -s, z], -1)
    sin_b2 = jnp.concatenate([z, s, z, s], -1)
    cos_m = jnp.concatenate([c, c, jnp.full((p, 64), IDX_HEADS ** -0.5, F32)], -1)
    sin_m1 = jnp.concatenate([-s, z, z64], -1)
    sin_m2 = jnp.concatenate([z, s, z64], -1)
    return jnp.stack([cos_a, sin_a, cos_b, sin_b1, sin_b2, cos_m, sin_m1, sin_m2])


def _ret_body(rq_ref, rk_ref, rv_ref, rg_ref, mk_ref, mv_ref, dmat_ref, xi_ref, zeta_ref,
              mzeta_ref, gsc_ref, out_ref, r_ref):
    @pl.when(pl.program_id(1) == 0)
    def _():
        for h in range(RET_HEADS):
            hs = slice(h * 128, (h + 1) * 128)
            kz = (mk_ref[:, hs].astype(F32) * mzeta_ref[h]).astype(BF16)
            r_ref[h] = lax.dot_general(kz, mv_ref[:, hs], TN, preferred_element_type=F32)

    for h in range(RET_HEADS):
        hs = slice(h * 128, (h + 1) * 128)
        q = rq_ref[:, hs]
        k = rk_ref[:, hs]
        v = rv_ref[:, hs]
        s = lax.dot_general(q, k, NT, preferred_element_type=F32) * dmat_ref[h]
        o = jnp.dot(s.astype(BF16), v, preferred_element_type=F32)
        r = r_ref[h]
        qx = (q.astype(F32) * xi_ref[h]).astype(BF16)
        o = o + jnp.dot(qx, r.astype(BF16), preferred_element_type=F32)
        kz = (k.astype(F32) * zeta_ref[h]).astype(BF16)
        u = lax.dot_general(kz, v, TN, preferred_element_type=F32)
        r_ref[h] = r * gsc_ref[h] + u
        mu = jnp.mean(o, axis=-1, keepdims=True)
        d = o - mu
        var = jnp.mean(d * d, axis=-1, keepdims=True)
        on = d * lax.rsqrt(var + EPS)
        g = rg_ref[:, hs]
        out_ref[:, hs] = (g * jax.nn.sigmoid(g) * on).astype(BF16)


def _retention(rq, rk, rv, rg, mk, mv, batch, seq):
    n = rq.shape[0]
    nb = seq // RET_BLOCK
    lg = jnp.log(1.0 - 2.0 ** (-5.0 - jnp.arange(RET_HEADS, dtype=F32)))
    i = jnp.arange(RET_BLOCK, dtype=F32)
    ci = jnp.arange(RET_BLOCK) // CHUNK
    vis = (ci[None, :] <= ci[:, None])
    dmat = jnp.where(vis[None], jnp.exp(lg[:, None, None] * jnp.abs(i[:, None] - i[None, :])), 0.0)
    xi = jnp.broadcast_to(jnp.exp(lg[:, None] * (i + 1.0)[None, :])[:, :, None], (RET_HEADS, RET_BLOCK, 128))
    zeta = jnp.broadcast_to(jnp.exp(lg[:, None] * (RET_BLOCK - 1.0 - i)[None, :])[:, :, None],
                            (RET_HEADS, RET_BLOCK, 128))
    im = jnp.arange(N_META, dtype=F32)
    mzeta = jnp.broadcast_to(jnp.exp(lg[:, None] * (N_META - 1.0 - im)[None, :])[:, :, None],
                             (RET_HEADS, N_META, 128))
    gsc = jnp.broadcast_to(jnp.exp(lg * RET_BLOCK)[:, None, None], (RET_HEADS, 1, 128))
    row = lambda b, s: (b * nb + s, 0)
    c2 = lambda b, s: (0, 0)
    c3 = lambda b, s: (0, 0, 0)
    return pl.pallas_call(
        _ret_body,
        grid=(batch, nb),
        in_specs=[pl.BlockSpec((RET_BLOCK, 512), row), pl.BlockSpec((RET_BLOCK, 512), row),
                  pl.BlockSpec((RET_BLOCK, 512), row), pl.BlockSpec((RET_BLOCK, 512), row),
                  pl.BlockSpec((N_META, 512), c2), pl.BlockSpec((N_META, 512), c2),
                  pl.BlockSpec((RET_HEADS, RET_BLOCK, RET_BLOCK), c3),
                  pl.BlockSpec((RET_HEADS, RET_BLOCK, 128), c3),
                  pl.BlockSpec((RET_HEADS, RET_BLOCK, 128), c3),
                  pl.BlockSpec((RET_HEADS, N_META, 128), c3),
                  pl.BlockSpec((RET_HEADS, 1, 128), c3)],
        out_specs=pl.BlockSpec((RET_BLOCK, 512), row),
        out_shape=jax.ShapeDtypeStruct((n, 512), BF16),
        scratch_shapes=[pltpu.VMEM((RET_HEADS, 128, 128), F32)],
        compiler_params=_params(("arbitrary", "arbitrary")),
        name="retention",
    )(rq, rk, rv, rg, mk, mv, dmat, xi, zeta, mzeta, gsc)


def _dsa_body(qa_ref, qi_ref, misc_ref, k_ref, v_ref, ki_ref, wuv_ref, out_ref,
              isc_ref, bias_ref, m_ref, l_ref, acc_ref, *, topk):
    qb = pl.program_id(1)
    tq = DSA_BLOCK
    nh = DSA_HEADS

    wt = misc_ref[...].T
    qi = qi_ref[...].reshape(IDX_HEADS * tq, IDX_DIM)

    def isc_tile(koff, width, allowed):
        kt = ki_ref[pl.ds(koff, width), :]
        z = lax.dot_general(kt, qi, NT, preferred_element_type=F32)
        isc = None
        for h in range(IDX_HEADS):
            term = jnp.maximum(z[:, h * tq:(h + 1) * tq], 0.0) * wt[64 + h:65 + h, :]
            isc = term if isc is None else isc + term
        if allowed is not None:
            isc = jnp.where(allowed, isc, -jnp.inf)
        isc_ref[pl.ds(koff, width), :] = isc

    isc_tile(0, LANES, lax.broadcasted_iota(I32, (LANES, tq), 0) >= LANES - N_META)

    def body_a(t, c):
        isc_tile(pl.multiple_of(LANES + t * tq, LANES), tq, None)
        return c

    lax.fori_loop(0, qb, body_a, 0)
    keyc = lax.broadcasted_iota(I32, (tq, tq), 0) // CHUNK
    qryc = lax.broadcasted_iota(I32, (tq, tq), 1) // CHUNK
    diag_off = pl.multiple_of(LANES + qb * tq, LANES)
    isc_tile(diag_off, tq, keyc <= qryc)

    kf = float(topk)
    sub = SUBLANES

    def key_to_f32(key):
        return lax.bitcast_convert_type(jnp.where(key < 0, key ^ jnp.int32(0x7FFFFFFF), key), F32)

    def count(pred):
        def slab(off, a):
            hit = jnp.where(pred(isc_ref[pl.ds(off, LANES), :]), 1.0, 0.0)
            return a + jnp.sum(hit.reshape(LANES // sub, sub, tq), axis=0)

        def body(t, a):
            off = pl.multiple_of(LANES + t * (2 * LANES), LANES)
            return slab(pl.multiple_of(off + LANES, LANES), slab(off, a))

        a = lax.fori_loop(0, qb + 1, body, slab(0, jnp.zeros((sub, tq), F32)))
        return jnp.sum(a, axis=0, keepdims=True)

    def count_ge(cand):
        return count(lambda x: x >= cand)

    lowest = jnp.full((1, tq), -3.0e38, F32)
    n_real = count_ge(lowest)
    c_pos = count_ge(jnp.zeros((1, tq), F32))
    pos = c_pos >= kf
    base0 = jnp.where(pos, jnp.int32(0), jnp.int32(INT_MIN))
    cnt0 = jnp.where(n_real <= kf, kf, jnp.where(pos, c_pos, n_real))

    def unfinished(cnt):
        return jnp.max(jnp.where(cnt != kf, 1.0, 0.0))

    def try_bit(bit, base, cnt):
        cand = base | lax.shift_left(jnp.int32(1), bit)
        c = count_ge(key_to_f32(cand))
        ok = c >= kf
        return jnp.where(ok, cand, base), jnp.where(ok, c, cnt)

    base, cnt = base0, cnt0
    groups = DSA_KEY_BITS // DSA_BITS_PER_CHECK
    for bit in range(DSA_KEY_BITS - 1, DSA_BITS_PER_CHECK * groups - 1, -1):
        base, cnt = try_bit(bit, base, cnt)

    zero_tie = jnp.logical_and(pos, count(lambda x: x > 0.0) < kf)

    def group_cond(state):
        g, _, cnt = state
        return jnp.logical_and(g >= 0, unfinished(jnp.where(zero_tie, kf, cnt)) > 0.0)

    def group_body(state):
        g, base, cnt = state
        for k in range(DSA_BITS_PER_CHECK - 1, -1, -1):
            base, cnt = try_bit(g * DSA_BITS_PER_CHECK + k, base, cnt)
        return g - 1, base, cnt

    _, base, cnt = lax.while_loop(group_cond, group_body, (jnp.int32(groups - 1), base, cnt))

    def refine_cond(state):
        it, _, _, _, go = state
        return jnp.logical_and(it < DSA_REFINE_STEPS, go > 0.0)

    def refine_body(state):
        it, lo, hi, cnt, _ = state
        mid = lo + 0.5 * (hi - lo)
        c = count_ge(mid)
        ok = c >= kf
        cnt = jnp.where(ok, c, cnt)
        moving = jnp.logical_and(jnp.logical_and(mid > lo, mid < hi),
                                 jnp.logical_and(cnt != kf, jnp.logical_not(zero_tie)))
        return it + 1, jnp.where(ok, mid, lo), jnp.where(ok, hi, mid), cnt, jnp.max(jnp.where(moving, 1.0, 0.0))

    _, thr, _, cnt, _ = lax.while_loop(
        refine_cond, refine_body,
        (jnp.int32(0), key_to_f32(base), key_to_f32(base + 1), cnt, unfinished(jnp.where(zero_tie, kf, cnt))))
    thr = jnp.where(n_real <= kf, lowest, thr)

    need = kf - count(lambda x: x > thr)
    lower = (lax.broadcasted_iota(I32, (tq, tq), 1) <= lax.broadcasted_iota(I32, (tq, tq), 0)).astype(BF16)

    def mask_tile(off, width, ties_before):
        x = isc_ref[pl.ds(off, width), :]
        tie = x == thr
        tie_f = jnp.where(tie, 1.0, 0.0)
        upto = jnp.dot(lower[:width, :width], tie_f.astype(BF16), preferred_element_type=F32)
        keep = jnp.logical_or(x > thr, jnp.logical_and(tie, ties_before + upto - tie_f < need))
        bias_ref[:, pl.ds(off, width)] = jnp.where(keep, 0.0, NEG).T
        return ties_before + upto[width - 1:width, :]

    def body_m(t, ties_before):
        return mask_tile(pl.multiple_of(LANES + t * tq, LANES), tq, ties_before)

    lax.fori_loop(0, qb + 1, body_m, mask_tile(0, LANES, jnp.zeros((1, tq), F32)))

    m_ref[...] = jnp.full(m_ref.shape, NEG, F32)
    l_ref[...] = jnp.zeros(l_ref.shape, F32)
    acc_ref[...] = jnp.zeros(acc_ref.shape, F32)
    qa = qa_ref[...].reshape(nh * tq, DSA_DIM)

    def att_tile(koff, width):
        kt = k_ref[pl.ds(koff, width), :]
        vt = v_ref[pl.ds(koff, width), :]
        s = lax.dot_general(qa, kt, NT, preferred_element_type=F32).reshape(nh, tq, width)
        reps = width // LANES
        s = (s + bias_ref[:, pl.ds(koff, width)][None]).reshape(nh * tq, width)
        m_prev = m_ref[...]
        m_new = jnp.maximum(m_prev, jnp.max(s, axis=1, keepdims=True))
        alpha = jnp.exp2(m_prev - m_new)
        m_w = m_new if reps == 1 else jnp.concatenate([m_new] * reps, axis=1)
        p = jnp.exp2(s - m_w)
        l_ref[...] = alpha * l_ref[...] + jnp.sum(p, axis=1, keepdims=True)
        acc_ref[...] = alpha * acc_ref[...] + jnp.dot(p.astype(BF16), vt, preferred_element_type=F32)
        m_ref[...] = m_new

    att_tile(0, LANES)

    def body_c(t, c):
        att_tile(pl.multiple_of(LANES + t * (2 * tq), LANES), 2 * tq)
        return c

    lax.fori_loop(0, (qb + 1) // 2, body_c, 0)

    @pl.when(qb % 2 == 0)
    def _():
        att_tile(diag_off, tq)

    o = (acc_ref[...] / l_ref[...]).astype(BF16).reshape(nh, tq, DSA_DIM)
    for h in range(nh):
        out_ref[:, h * DSA_OUT_DIM:(h + 1) * DSA_OUT_DIM] = jnp.dot(
            o[h], wuv_ref[h], preferred_element_type=F32).astype(BF16)


def _dsa(dq, iq, misc, kpad, vpad, kipad, wuv, batch, seq, topk):
    n = misc.shape[0]
    nq = seq // DSA_BLOCK
    tp = kpad.shape[1]
    head = lambda b, q: (0, b * nq + q, 0)
    row = lambda b, q: (b * nq + q, 0)
    per_b = lambda b, q: (b, 0, 0)
    return pl.pallas_call(
        functools.partial(_dsa_body, topk=topk),
        grid=(batch, nq),
        in_specs=[pl.BlockSpec((DSA_HEADS, DSA_BLOCK, DSA_DIM), head),
                  pl.BlockSpec((IDX_HEADS, DSA_BLOCK, IDX_DIM), head),
                  pl.BlockSpec((DSA_BLOCK, 128), row),
                  pl.BlockSpec((None, tp, DSA_DIM), per_b),
                  pl.BlockSpec((None, tp, DSA_DIM), per_b),
                  pl.BlockSpec((None, tp, IDX_DIM), per_b),
                  pl.BlockSpec((DSA_HEADS, DSA_DIM, DSA_OUT_DIM), lambda b, q: (0, 0, 0))],
        out_specs=pl.BlockSpec((DSA_BLOCK, 512), row),
        out_shape=jax.ShapeDtypeStruct((n, 512), BF16),
        scratch_shapes=[pltpu.VMEM((tp, DSA_BLOCK), F32),
                        pltpu.VMEM((DSA_BLOCK, tp), F32),
                        pltpu.VMEM((DSA_HEADS * DSA_BLOCK, LANES), F32),
                        pltpu.VMEM((DSA_HEADS * DSA_BLOCK, LANES), F32),
                        pltpu.VMEM((DSA_HEADS * DSA_BLOCK, DSA_DIM), F32)],
        compiler_params=_params(("arbitrary", "arbitrary")),
        name="dsa",
    )(dq, iq, misc, kpad, vpad, kipad, wuv)


def _post_body(ret_ref, att_ref, x_ref, w_ref, g_ref, h1_ref, hnt_ref):
    y = jnp.dot(ret_ref[...], w_ref[0:512, :], preferred_element_type=F32)
    y = y + jnp.dot(att_ref[...], w_ref[512:1024, :], preferred_element_type=F32)
    h1 = x_ref[...] + y
    h1_ref[...] = h1
    hnt_ref[...] = _rms(h1, g_ref[...]).T.astype(BF16)


def _post(ret, att, x2d, w_out, g, tm):
    n = x2d.shape[0]
    row = lambda i: (i, 0)
    return pl.pallas_call(
        _post_body,
        grid=(n // tm,),
        in_specs=[pl.BlockSpec((tm, 512), row), pl.BlockSpec((tm, 512), row),
                  pl.BlockSpec((tm, D_MODEL), row),
                  pl.BlockSpec((D_MODEL, D_MODEL), lambda i: (0, 0)),
                  pl.BlockSpec((1, D_MODEL), lambda i: (0, 0))],
        out_specs=(pl.BlockSpec((tm, D_MODEL), row), pl.BlockSpec((D_MODEL, tm), lambda i: (0, i))),
        out_shape=(jax.ShapeDtypeStruct((n, D_MODEL), F32), jax.ShapeDtypeStruct((D_MODEL, n), BF16)),
        compiler_params=_params(("arbitrary",)),
        name="post",
    )(ret, att, x2d, w_out, g)


def _oddeven_merge_sort_pairs(n):
    pairs = []
    p = 1
    while p < n:
        k = p
        while k >= 1:
            for j in range(k % p, n - k, 2 * k):
                for i in range(min(k, n - j - k)):
                    if (i + j) // (2 * p) == (i + j + k) // (2 * p):
                        pairs.append((i + j, i + j + k))
            k //= 2
        p *= 2
    return tuple(pairs)


def _bitonic_merge_pairs(n):
    pairs = []
    d = n // 2
    while d >= 1:
        pairs += [(i, i + d) for i in range(n) if (i & d) == 0]
        d //= 2
    return tuple(pairs)


_SORT16 = _oddeven_merge_sort_pairs(PEER_TOPK)
_MERGE16 = _bitonic_merge_pairs(PEER_TOPK)


def _top16_rows(s):
    def exchange(x, pairs):
        for i, j in pairs:
            x[i], x[j] = jnp.maximum(x[i], x[j]), jnp.minimum(x[i], x[j])

    x = [s[SUBLANES * j:SUBLANES * (j + 1), :] for j in range(PEER_TOPK)]
    exchange(x, _SORT16)
    shift = SUBLANES // 2
    while shift >= 1:
        other = [pltpu.roll(v, shift, 0) for v in x]
        x = [jnp.maximum(x[i], other[PEER_TOPK - 1 - i]) for i in range(PEER_TOPK)]
        exchange(x, _MERGE16)
        shift //= 2
    return [v[0:1] for v in x]


def _prefix_count(pred, w):
    m8 = pred(w[7])
    m4 = pred(jnp.where(m8, w[11], w[3]))
    m2 = pred(jnp.where(m8, jnp.where(m4, w[13], w[9]), jnp.where(m4, w[5], w[1])))
    m1 = pred(jnp.where(m8, jnp.where(m4, jnp.where(m2, w[14], w[12]), jnp.where(m2, w[10], w[8])),
                        jnp.where(m4, jnp.where(m2, w[6], w[4]), jnp.where(m2, w[2], w[0]))))
    bit = lambda m, v: jnp.where(m, v, 0.0)
    return bit(m8, 8.0) + bit(m4, 4.0) + bit(m2, 2.0) + bit(m1, 1.0) + bit(pred(w[15]), 1.0)


def _top_desc(s, k, dst_ref, ts):
    cur = s
    for r in range(k):
        m = jnp.max(cur, axis=0, keepdims=True)
        dst_ref[r:r + 1, ts] = m
        if r + 1 < k:
            cur = jnp.where(cur == m, -jnp.inf, cur)


def _psel_body(hnt_ref, wq_ref, sk_ref, cnt_ref, e1_ref, r2_ref, e2_ref, s_ref, v1_ref, v2_ref, vc_ref):
    hnt = hnt_ref[...]
    for h in range(PEER_HEADS):
        q = jnp.dot(wq_ref[h * 256:(h + 1) * 256, :], hnt, preferred_element_type=F32)
        s_ref[0] = jnp.dot(sk_ref[h, 0], q[:128].astype(BF16), preferred_element_type=F32)
        s_ref[1] = jnp.dot(sk_ref[h, 1], q[128:].astype(BF16), preferred_element_type=F32)
        for tc in range(hnt.shape[1] // LANES):
            ts = slice(tc * LANES, (tc + 1) * LANES)
            s1 = s_ref[0, :, ts]
            s2 = s_ref[1, :, ts]
            w1 = _top16_rows(s1)
            w2 = _top16_rows(s2)
            for b in range(PEER_TOPK):
                v1_ref[b:b + 1, ts] = w1[b]
                v2_ref[b:b + 1, ts] = w2[b]
            v1 = v1_ref[:, ts]
            v2 = v2_ref[:, ts]
            cand = jnp.concatenate([v2 + v1[0:1]] + [v2[0:8] + v1[a:a + 1] for a in range(1, 8)]
                                   + [v1[8:16] + v2[0:1]], axis=0)
            _top_desc(cand, PEER_TOPK, vc_ref, ts)
            tau = vc_ref[PEER_TOPK - 1:PEER_TOPK, ts]
            top = vc_ref[0:1, ts]
            zsum = jnp.sum(jnp.where(cand >= tau, jnp.exp(cand - top), 0.0), axis=0, keepdims=True)
            cnt_ref[h, :, ts] = _prefix_count(lambda t: s1 + t >= tau, w2)
            e1_ref[h, :, ts] = jnp.exp(s1 - v1[0:1]) / zsum
            rank2 = _prefix_count(lambda t: t > s2, w2)
            r2_ref[h, :, ts] = pltpu.bitcast(rank2.astype(BF16), I32)
            e2_ref[h, :, ts] = pltpu.bitcast(jnp.exp(s2 - v2[0:1]).astype(BF16), I32)


def _psel(hnt, wq_t, sk, tn):
    n = hnt.shape[1]
    spec = pl.BlockSpec((PEER_HEADS, PEER_NKEYS, tn), lambda i: (0, 0, i))
    shp32 = jax.ShapeDtypeStruct((PEER_HEADS, PEER_NKEYS, n), F32)
    shp16 = jax.ShapeDtypeStruct((PEER_HEADS, PEER_NKEYS // 2, n), I32)
    spec16 = pl.BlockSpec((PEER_HEADS, PEER_NKEYS // 2, tn), lambda i: (0, 0, i))
    return pl.pallas_call(
        _psel_body,
        grid=(n // tn,),
        in_specs=[pl.BlockSpec((D_MODEL, tn), lambda i: (0, i)),
                  pl.BlockSpec((PEER_HEADS * 256, D_MODEL), lambda i: (0, 0)),
                  pl.BlockSpec((PEER_HEADS, 2, PEER_NKEYS, 128), lambda i: (0, 0, 0, 0))],
        out_specs=(spec, spec, spec16, spec16),
        out_shape=(shp32, shp32, shp16, shp16),
        scratch_shapes=[pltpu.VMEM((2, PEER_NKEYS, tn), F32)] + [pltpu.VMEM((PEER_TOPK, tn), F32)] * 3,
        compiler_params=_params(("arbitrary",)),
        name="psel",
    )(hnt, wq_t, sk)


def _pdense_body(*refs, eb, nchunk, nblk):
    hnt_ref, u_ref, vt_ref = refs[:3]
    cnt_refs = refs[3:3 + nchunk]
    e1_refs = refs[3 + nchunk:3 + 2 * nchunk]
    r2_ref, e2_ref, h1_ref, g_ref, out_ref, acc_ref, a_ref, coef_ref = refs[3 + 2 * nchunk:]
    s = pl.program_id(1)
    nsub = eb // PEER_NKEYS

    def row_bf16(ref, h, i1):
        return jnp.broadcast_to(ref[h, pl.ds(i1, 1), :], (PEER_NKEYS, LANES)).astype(BF16)

    @pl.when(s == 0)
    def _():
        acc_ref[...] = jnp.zeros(acc_ref.shape, F32)

    for part in range(eb // PDENSE_ROWS):
        rows = slice(part * PDENSE_ROWS, (part + 1) * PDENSE_ROWS)
        a_ref[rows, :] = jnp.dot(u_ref[rows, :], hnt_ref[...], preferred_element_type=F32)

    for c in range(nsub):
        rows = slice(c * PEER_NKEYS, (c + 1) * PEER_NKEYS)
        for tc in range(nchunk):
            ts = slice(tc * LANES, (tc + 1) * LANES)
            gate = None
            for h in range(PEER_HEADS):
                r2 = pltpu.bitcast(r2_ref[h, :, ts], BF16)
                e2 = pltpu.bitcast(e2_ref[h, :, ts], BF16)
                cnt = row_bf16(cnt_refs[tc], h, s * nsub + c)
                e1 = row_bf16(e1_refs[tc], h, s * nsub + c)
                term = jnp.where(r2 < cnt, e2, jnp.zeros((), BF16)) * e1
                gate = term if gate is None else gate + term
            at = a_ref[rows, ts]
            gelu = (0.5 * at) * (1.0 + lax.erf(at * (0.5 ** 0.5)))
            coef_ref[rows, ts] = gelu.astype(BF16) * gate

    acc_ref[...] += jnp.dot(vt_ref[...], coef_ref[...], preferred_element_type=F32)

    @pl.when(s == nblk - 1)
    def _():
        out_ref[...] = _rms(h1_ref[...] + acc_ref[...].T, g_ref[...])


def _pdense(hnt, u, vt, cnt, e1, r2, e2, h1, g, tn, eb):
    n = hnt.shape[1]
    nblk = u.shape[0] // eb
    assert nblk >= 2 and vt.shape == (nblk, D_MODEL, eb)
    nchunk = tn // LANES
    chunk = [pl.BlockSpec((PEER_HEADS, PEER_NKEYS, LANES), lambda i, s, c=c: (0, 0, i * nchunk + c))
             for c in range(nchunk)]
    sel16 = pl.BlockSpec((PEER_HEADS, PEER_NKEYS // 2, tn), lambda i, s: (0, 0, i))
    return pl.pallas_call(
        functools.partial(_pdense_body, eb=eb, nchunk=nchunk, nblk=nblk),
        grid=(n // tn, nblk),
        in_specs=[pl.BlockSpec((D_MODEL, tn), lambda i, s: (0, i)),
                  pl.BlockSpec((eb, D_MODEL), lambda i, s: (s, 0)),
                  pl.BlockSpec((None, D_MODEL, eb), lambda i, s: (s, 0, 0))]
                 + chunk + chunk
                 + [sel16, sel16,
                    pl.BlockSpec((tn, D_MODEL), lambda i, s: (i, 0)),
                    pl.BlockSpec((1, D_MODEL), lambda i, s: (0, 0))],
        out_specs=pl.BlockSpec((tn, D_MODEL), lambda i, s: (i, 0)),
        out_shape=jax.ShapeDtypeStruct((n, D_MODEL), F32),
        scratch_shapes=[pltpu.VMEM((D_MODEL, tn), F32),
                        pltpu.VMEM((eb, tn), F32), pltpu.VMEM((eb, tn), BF16)],
        compiler_params=_params(("arbitrary", "arbitrary")),
        name="pdense",
    )(hnt, u, vt, *([cnt] * nchunk), *([e1] * nchunk), r2, e2, h1, g)


def kernel(x, meta_tokens, norm_mix, w_in, w_uv, w_out, norm_ffn, peer_wq, peer_subkeys,
           peer_u, peer_v, norm_final):
    batch, seq, d = x.shape
    assert d == D_MODEL and norm_mix.shape[0] == 1 and seq % DSA_BLOCK == 0
    n = batch * seq
    topk = min(DSA_TOPK_MAX, seq // 4)
    tm = TOKEN_TILE if seq % TOKEN_TILE == 0 else DSA_BLOCK

    w = jnp.pad(w_in[0], ((0, 0), (0, IN_WIDTH_PAD - IN_WIDTH))).astype(BF16)
    g_mix = norm_mix[0][None, :]
    x2d = x.reshape(n, d)
    pos_f = jnp.arange(seq, dtype=F32) + float(N_META)
    pos_m = jnp.arange(N_META, dtype=F32)

    rq, rk, rv, rg, dq, dk, dv, iq, misc = _proj(x2d, g_mix, w, _rope_tables(pos_f), tm, seq // tm)
    _, mk, mv, _, _, mdk, mdv, _, mmisc = _proj(meta_tokens, g_mix, w, _rope_tables(pos_m), N_META, 1)

    ret = _retention(rq, rk, rv, rg, mk, mv, batch, seq)

    def keys(frames, meta):
        width = frames.shape[-1]
        lead = jnp.concatenate([jnp.zeros((LANES - N_META, width), frames.dtype), meta], axis=0)
        lead = jnp.broadcast_to(lead[None], (batch, LANES, width))
        return jnp.concatenate([lead, frames.reshape(batch, seq, width)], axis=1)

    kpad = keys(dk, mdk)
    vpad = keys(dv, mdv)
    kipad = keys(misc[:, :IDX_DIM].astype(BF16), mmisc[:, :IDX_DIM].astype(BF16))
    att = _dsa(dq, iq, misc, kpad, vpad, kipad, w_uv[0].astype(BF16), batch, seq, topk)

    h1, hnt = _post(ret, att, x2d, w_out[0].astype(BF16), norm_ffn[0][None, :], tm)

    wq_t = peer_wq[0].reshape(d, PEER_HEADS * 256).T.astype(BF16)
    cnt, e1, r2, e2 = _psel(hnt, wq_t, peer_subkeys[0].astype(BF16), tm)
    vt = peer_v[0].reshape(-1, PDENSE_EXPERTS, d).transpose(0, 2, 1).astype(BF16)
    out = _pdense(hnt, peer_u[0].astype(BF16), vt, cnt, e1, r2, e2, h1,
                  norm_final[None, :], tm, PDENSE_EXPERTS)
    return out.reshape(batch, seq, d)
```

```python
import functools

import jax
import jax.numpy as jnp
from jax import lax
from jax.experimental import pallas as pl
from jax.experimental.pallas import tpu as pltpu

F32 = jnp.float32
BF16 = jnp.bfloat16
I32 = jnp.int32

D_MODEL = 1024
N_META = 16
CHUNK = 64
RET_HEADS = 4
RET_DIM = 128
DSA_HEADS = 8
DSA_DIM = 128
DSA_OUT_DIM = 64
IDX_HEADS = 8
IDX_DIM = 64
DSA_TOPK_MAX = 256
ROPE_THETA = 10000.0
PEER_HEADS = 8
PEER_NKEYS = 128
PEER_TOPK = 16
EPS = 1e-6
NEG = -1e30
LOG2E = 1.4426950408889634
INT_MIN = -(2 ** 31)

IN_WIDTH = 3912
IN_WIDTH_PAD = 3968
LANES = 128
RET_BLOCK = 256
PDENSE_EXPERTS = 2048
PDENSE_ROWS = 512
DSA_BLOCK = 256
DSA_KEY_BITS = 31
DSA_BITS_PER_CHECK = 4
DSA_REFINE_STEPS = 40
SUBLANES = 8
TOKEN_TILE = 512
V7X_VMEM_BYTES = 64 * 1024 * 1024
VMEM_LIMIT = V7X_VMEM_BYTES * 7 // 8

NT = (((1,), (1,)), ((), ()))
TN = (((0,), (0,)), ((), ()))


def _rms(x, g):
    return x * lax.rsqrt(jnp.mean(x * x, axis=-1, keepdims=True) + EPS) * g


def _params(sem):
    return pltpu.CompilerParams(dimension_semantics=sem, vmem_limit_bytes=VMEM_LIMIT)


def _proj_body(x_ref, g_ref, w_ref, tab_ref, rq_ref, rk_ref, rv_ref, rg_ref, dq_ref, dk_ref,
               dv_ref, iq_ref, misc_ref):
    xn = _rms(x_ref[...], g_ref[...]).astype(BF16)
    cos_a, sin_a, cos_b, sin_b1, sin_b2, cos_m, sin_m1, sin_m2 = (tab_ref[i] for i in range(8))

    def mm(lo, n):
        return jnp.dot(xn, w_ref[:, lo:lo + n], preferred_element_type=F32)

    def rope_a(y):
        return y * cos_a + pltpu.roll(y, 64, 1) * sin_a

    def rope_b(y, c, s1, s2):
        return y * c + pltpu.roll(y, 96, 1) * s1 + pltpu.roll(y, 32, 1) * s2

    y = mm(0, 512)
    for h in range(RET_HEADS):
        rq_ref[:, h * 128:(h + 1) * 128] = rope_a(y[:, h * 128:(h + 1) * 128]).astype(BF16)
    y = mm(512, 512)
    for h in range(RET_HEADS):
        rk_ref[:, h * 128:(h + 1) * 128] = (
            rope_a(y[:, h * 128:(h + 1) * 128]) * (RET_DIM ** -0.5)).astype(BF16)
    rv_ref[...] = mm(1024, 512).astype(BF16)
    rg_ref[...] = mm(1536, 512)
    y = mm(2048, 1024)
    for h in range(DSA_HEADS):
        dq_ref[h] = (rope_a(y[:, h * 128:(h + 1) * 128]) * (DSA_DIM ** -0.5 * LOG2E)).astype(BF16)
    dk_ref[...] = rope_a(mm(3072, 128)).astype(BF16)
    dv_ref[...] = mm(3200, 128).astype(BF16)
    y = mm(3328, 512)
    for p in range(IDX_HEADS // 2):
        o = (rope_b(y[:, p * 128:(p + 1) * 128], cos_b, sin_b1, sin_b2) * (IDX_DIM ** -0.5)).astype(BF16)
        iq_ref[2 * p] = o[:, :64]
        iq_ref[2 * p + 1] = o[:, 64:]
    misc_ref[...] = rope_b(mm(3840, 128), cos_m, sin_m1, sin_m2)


def _proj(x2d, g, w, tab, tm, tab_blocks):
    n = x2d.shape[0]
    row = lambda i: (i, 0)
    head = lambda i: (0, i, 0)
    out_shape = (
        jax.ShapeDtypeStruct((n, 512), BF16), jax.ShapeDtypeStruct((n, 512), BF16),
        jax.ShapeDtypeStruct((n, 512), BF16), jax.ShapeDtypeStruct((n, 512), F32),
        jax.ShapeDtypeStruct((DSA_HEADS, n, 128), BF16), jax.ShapeDtypeStruct((n, 128), BF16),
        jax.ShapeDtypeStruct((n, 128), BF16), jax.ShapeDtypeStruct((IDX_HEADS, n, 64), BF16),
        jax.ShapeDtypeStruct((n, 128), F32))
    out_specs = (
        pl.BlockSpec((tm, 512), row), pl.BlockSpec((tm, 512), row), pl.BlockSpec((tm, 512), row),
        pl.BlockSpec((tm, 512), row), pl.BlockSpec((DSA_HEADS, tm, 128), head),
        pl.BlockSpec((tm, 128), row), pl.BlockSpec((tm, 128), row),
        pl.BlockSpec((IDX_HEADS, tm, 64), head), pl.BlockSpec((tm, 128), row))
    return pl.pallas_call(
        _proj_body,
        grid=(n // tm,),
        in_specs=[pl.BlockSpec((tm, D_MODEL), row),
                  pl.BlockSpec((1, D_MODEL), lambda i: (0, 0)),
                  pl.BlockSpec((D_MODEL, IN_WIDTH_PAD), lambda i: (0, 0)),
                  pl.BlockSpec((8, tm, 128), lambda i: (0, i % tab_blocks, 0))],
        out_specs=out_specs,
        out_shape=out_shape,
        compiler_params=_params(("arbitrary",)),
        name="proj",
    )(x2d, g, w, tab)


def _rope_tables(pos):
    p = pos.shape[0]
    inv = ROPE_THETA ** (-jnp.arange(0, 64, dtype=F32) * 2.0 / 128)
    ang = pos[:, None] * inv[None, :]
    c, s = jnp.cos(ang), jnp.sin(ang)
    cos_a = jnp.concatenate([c, c], -1)
    sin_a = jnp.concatenate([-s, s], -1)
    inv = ROPE_THETA ** (-jnp.arange(0, 32, dtype=F32) * 2.0 / 64)
    ang = pos[:, None] * inv[None, :]
    c, s = jnp.cos(ang), jnp.sin(ang)
    z = jnp.zeros_like(s)
    z64 = jnp.zeros((p, 64), F32)
    cos_b = jnp.concatenate([c, c, c, c], -1)
    sin_b1 = jnp.concatenate([-s, z, -s, z], -1)
    sin_b2 = jnp.concatenate([z, s, z, s], -1)
    cos_m = jnp.concatenate([c, c, jnp.full((p, 64), IDX_HEADS ** -0.5, F32)], -1)
    sin_m1 = jnp.concatenate([-s, z, z64], -1)
    sin_m2 = jnp.concatenate([z, s, z64], -1)
    return jnp.stack([cos_a, sin_a, cos_b, sin_b1, sin_b2, cos_m, sin_m1, sin_m2])


def _ret_body(rq_ref, rk_ref, rv_ref, rg_ref, mk_ref, mv_ref, dmat_ref, xi_ref, zeta_ref,
              mzeta_ref, gsc_ref, out_ref, r_ref):
    @pl.when(pl.program_id(1) == 0)
    def _():
        for h in range(RET_HEADS):
            hs = slice(h * 128, (h + 1) * 128)
            kz = (mk_ref[:, hs].astype(F32) * mzeta_ref[h]).astype(BF16)
            r_ref[h] = lax.dot_general(kz, mv_ref[:, hs], TN, preferred_element_type=F32)

    for h in range(RET_HEADS):
        hs = slice(h * 128, (h + 1) * 128)
        q = rq_ref[:, hs]
        k = rk_ref[:, hs]
        v = rv_ref[:, hs]
        s = lax.dot_general(q, k, NT, preferred_element_type=F32) * dmat_ref[h]
        o = jnp.dot(s.astype(BF16), v, preferred_element_type=F32)
        r = r_ref[h]
        qx = (q.astype(F32) * xi_ref[h]).astype(BF16)
        o = o + jnp.dot(qx, r.astype(BF16), preferred_element_type=F32)
        kz = (k.astype(F32) * zeta_ref[h]).astype(BF16)
        u = lax.dot_general(kz, v, TN, preferred_element_type=F32)
        r_ref[h] = r * gsc_ref[h] + u
        mu = jnp.mean(o, axis=-1, keepdims=True)
        d = o - mu
        var = jnp.mean(d * d, axis=-1, keepdims=True)
        on = d * lax.rsqrt(var + EPS)
        g = rg_ref[:, hs]
        out_ref[:, hs] = (g * jax.nn.sigmoid(g) * on).astype(BF16)


def _retention(rq, rk, rv, rg, mk, mv, batch, seq):
    n = rq.shape[0]
    nb = seq // RET_BLOCK
    lg = jnp.log(1.0 - 2.0 ** (-5.0 - jnp.arange(RET_HEADS, dtype=F32)))
    i = jnp.arange(RET_BLOCK, dtype=F32)
    ci = jnp.arange(RET_BLOCK) // CHUNK
    vis = (ci[None, :] <= ci[:, None])
    dmat = jnp.where(vis[None], jnp.exp(lg[:, None, None] * jnp.abs(i[:, None] - i[None, :])), 0.0)
    xi = jnp.broadcast_to(jnp.exp(lg[:, None] * (i + 1.0)[None, :])[:, :, None], (RET_HEADS, RET_BLOCK, 128))
    zeta = jnp.broadcast_to(jnp.exp(lg[:, None] * (RET_BLOCK - 1.0 - i)[None, :])[:, :, None],
                            (RET_HEADS, RET_BLOCK, 128))
    im = jnp.arange(N_META, dtype=F32)
    mzeta = jnp.broadcast_to(jnp.exp(lg[:, None] * (N_META - 1.0 - im)[None, :])[:, :, None],
                             (RET_HEADS, N_META, 128))
    gsc = jnp.broadcast_to(jnp.exp(lg * RET_BLOCK)[:, None, None], (RET_HEADS, 1, 128))
    row = lambda b, s: (b * nb + s, 0)
    c2 = lambda b, s: (0, 0)
    c3 = lambda b, s: (0, 0, 0)
    return pl.pallas_call(
        _ret_body,
        grid=(batch, nb),
        in_specs=[pl.BlockSpec((RET_BLOCK, 512), row), pl.BlockSpec((RET_BLOCK, 512), row),
                  pl.BlockSpec((RET_BLOCK, 512), row), pl.BlockSpec((RET_BLOCK, 512), row),
                  pl.BlockSpec((N_META, 512), c2), pl.BlockSpec((N_META, 512), c2),
                  pl.BlockSpec((RET_HEADS, RET_BLOCK, RET_BLOCK), c3),
                  pl.BlockSpec((RET_HEADS, RET_BLOCK, 128), c3),
                  pl.BlockSpec((RET_HEADS, RET_BLOCK, 128), c3),
                  pl.BlockSpec((RET_HEADS, N_META, 128), c3),
                  pl.BlockSpec((RET_HEADS, 1, 128), c3)],
        out_specs=pl.BlockSpec((RET_BLOCK, 512), row),
        out_shape=jax.ShapeDtypeStruct((n, 512), BF16),
        scratch_shapes=[pltpu.VMEM((RET_HEADS, 128, 128), F32)],
        compiler_params=_params(("arbitrary", "arbitrary")),
        name="retention",
    )(rq, rk, rv, rg, mk, mv, dmat, xi, zeta, mzeta, gsc)


def _dsa_body(qa_ref, qi_ref, misc_ref, k_ref, v_ref, ki_ref, wuv_ref, out_ref,
              isc_ref, bias_ref, m_ref, l_ref, acc_ref, *, topk):
    qb = pl.program_id(1)
    tq = DSA_BLOCK
    nh = DSA_HEADS

    wt = misc_ref[...].T
    qi = qi_ref[...].reshape(IDX_HEADS * tq, IDX_DIM)

    def isc_tile(koff, width, allowed):
        kt = ki_ref[pl.ds(koff, width), :]
        z = lax.dot_general(kt, qi, NT, preferred_element_type=F32)
        isc = None
        for h in range(IDX_HEADS):
            term = jnp.maximum(z[:, h * tq:(h + 1) * tq], 0.0) * wt[64 + h:65 + h, :]
            isc = term if isc is None else isc + term
        if allowed is not None:
            isc = jnp.where(allowed, isc, -jnp.inf)
        isc_ref[pl.ds(koff, width), :] = isc

    isc_tile(0, LANES, lax.broadcasted_iota(I32, (LANES, tq), 0) >= LANES - N_META)

    def body_a(t, c):
        isc_tile(pl.multiple_of(LANES + t * tq, LANES), tq, None)
        return c

    lax.fori_loop(0, qb, body_a, 0)
    keyc = lax.broadcasted_iota(I32, (tq, tq), 0) // CHUNK
    qryc = lax.broadcasted_iota(I32, (tq, tq), 1) // CHUNK
    diag_off = pl.multiple_of(LANES + qb * tq, LANES)
    isc_tile(diag_off, tq, keyc <= qryc)

    kf = float(topk)
    sub = SUBLANES

    def key_to_f32(key):
        return lax.bitcast_convert_type(jnp.where(key < 0, key ^ jnp.int32(0x7FFFFFFF), key), F32)

    def count(pred):
        def slab(off, a):
            hit = jnp.where(pred(isc_ref[pl.ds(off, LANES), :]), 1.0, 0.0)
            return a + jnp.sum(hit.reshape(LANES // sub, sub, tq), axis=0)

        def body(t, a):
            off = pl.multiple_of(LANES + t * (2 * LANES), LANES)
            return slab(pl.multiple_of(off + LANES, LANES), slab(off, a))

        a = lax.fori_loop(0, qb + 1, body, slab(0, jnp.zeros((sub, tq), F32)))
        return jnp.sum(a, axis=0, keepdims=True)

    def count_ge(cand):
        return count(lambda x: x >= cand)

    lowest = jnp.full((1, tq), -3.0e38, F32)
    n_real = count_ge(lowest)
    c_pos = count_ge(jnp.zeros((1, tq), F32))
    pos = c_pos >= kf
    base0 = jnp.where(pos, jnp.int32(0), jnp.int32(INT_MIN))
    cnt0 = jnp.where(n_real <= kf, kf, jnp.where(pos, c_pos, n_real))

    def unfinished(cnt):
        return jnp.max(jnp.where(cnt != kf, 1.0, 0.0))

    def try_bit(bit, base, cnt):
        cand = base | lax.shift_left(jnp.int32(1), bit)
        c = count_ge(key_to_f32(cand))
        ok = c >= kf
        return jnp.where(ok, cand, base), jnp.where(ok, c, cnt)

    base, cnt = base0, cnt0
    groups = DSA_KEY_BITS // DSA_BITS_PER_CHECK
    for bit in range(DSA_KEY_BITS - 1, DSA_BITS_PER_CHECK * groups - 1, -1):
        base, cnt = try_bit(bit, base, cnt)

    zero_tie = jnp.logical_and(pos, count(lambda x: x > 0.0) < kf)

    def group_cond(state):
        g, _, cnt = state
        return jnp.logical_and(g >= 0, unfinished(jnp.where(zero_tie, kf, cnt)) > 0.0)

    def group_body(state):
        g, base, cnt = state
        for k in range(DSA_BITS_PER_CHECK - 1, -1, -1):
            base, cnt = try_bit(g * DSA_BITS_PER_CHECK + k, base, cnt)
        return g - 1, base, cnt

    _, base, cnt = lax.while_loop(group_cond, group_body, (jnp.int32(groups - 1), base, cnt))

    def refine_cond(state):
        it, _, _, _, go = state
        return jnp.logical_and(it < DSA_REFINE_STEPS, go > 0.0)

    def refine_body(state):
        it, lo, hi, cnt, _ = state
        mid = lo + 0.5 * (hi - lo)
        c = count_ge(mid)
        ok = c >= kf
        cnt = jnp.where(ok, c, cnt)
        moving = jnp.logical_and(jnp.logical_and(mid > lo, mid < hi),
                                 jnp.logical_and(cnt != kf, jnp.logical_not(zero_tie)))
        return it + 1, jnp.where(ok, mid, lo), jnp.where(ok, hi, mid), cnt, jnp.max(jnp.where(moving, 1.0, 0.0))

    _, thr, _, cnt, _ = lax.while_loop(
        refine_cond, refine_body,
        (jnp.int32(0), key_to_f32(base), key_to_f32(base + 1), cnt, unfinished(jnp.where(zero_tie, kf, cnt))))
    thr = jnp.where(n_real <= kf, lowest, thr)

    need = kf - count(lambda x: x > thr)
    lower = (lax.broadcasted_iota(I32, (tq, tq), 1) <= lax.broadcasted_iota(I32, (tq, tq), 0)).astype(BF16)

    def mask_tile(off, width, ties_before):
        x = isc_ref[pl.ds(off, width), :]
        tie = x == thr
        tie_f = jnp.where(tie, 1.0, 0.0)
        upto = jnp.dot(lower[:width, :width], tie_f.astype(BF16), preferred_element_type=F32)
        keep = jnp.logical_or(x > thr, jnp.logical_and(tie, ties_before + upto - tie_f < need))
        bias_ref[:, pl.ds(off, width)] = jnp.where(keep, 0.0, NEG).T
        return ties_before + upto[width - 1:width, :]

    def body_m(t, ties_before):
        return mask_tile(pl.multiple_of(LANES + t * tq, LANES), tq, ties_before)

    lax.fori_loop(0, qb + 1, body_m, mask_tile(0, LANES, jnp.zeros((1, tq), F32)))

    m_ref[...] = jnp.full(m_ref.shape, NEG, F32)
    l_ref[...] = jnp.zeros(l_ref.shape, F32)
    acc_ref[...] = jnp.zeros(acc_ref.shape, F32)
    qa = qa_ref[...].reshape(nh * tq, DSA_DIM)

    def att_tile(koff, width):
        kt = k_ref[pl.ds(koff, width), :]
        vt = v_ref[pl.ds(koff, width), :]
        s = lax.dot_general(qa, kt, NT, preferred_element_type=F32).reshape(nh, tq, width)
        reps = width // LANES
        s = (s + bias_ref[:, pl.ds(koff, width)][None]).reshape(nh * tq, width)
        m_prev = m_ref[...]
        m_new = jnp.maximum(m_prev, jnp.max(s, axis=1, keepdims=True))
        alpha = jnp.exp2(m_prev - m_new)
        m_w = m_new if reps == 1 else jnp.concatenate([m_new] * reps, axis=1)
        p = jnp.exp2(s - m_w)
        l_ref[...] = alpha * l_ref[...] + jnp.sum(p, axis=1, keepdims=True)
        acc_ref[...] = alpha * acc_ref[...] + jnp.dot(p.astype(BF16), vt, preferred_element_type=F32)
        m_ref[...] = m_new

    att_tile(0, LANES)

    def body_c(t, c):
        att_tile(pl.multiple_of(LANES + t * (2 * tq), LANES), 2 * tq)
        return c

    lax.fori_loop(0, (qb + 1) // 2, body_c, 0)

    @pl.when(qb % 2 == 0)
    def _():
        att_tile(diag_off, tq)

    o = (acc_ref[...] / l_ref[...]).astype(BF16).reshape(nh, tq, DSA_DIM)
    for h in range(nh):
        out_ref[:, h * DSA_OUT_DIM:(h + 1) * DSA_OUT_DIM] = jnp.dot(
            o[h], wuv_ref[h], preferred_element_type=F32).astype(BF16)


def _dsa(dq, iq, misc, kpad, vpad, kipad, wuv, batch, seq, topk):
    n = misc.shape[0]
    nq = seq // DSA_BLOCK
    tp = kpad.shape[1]
    head = lambda b, q: (0, b * nq + q, 0)
    row = lambda b, q: (b * nq + q, 0)
    per_b = lambda b, q: (b, 0, 0)
    return pl.pallas_call(
        functools.partial(_dsa_body, topk=topk),
        grid=(batch, nq),
        in_specs=[pl.BlockSpec((DSA_HEADS, DSA_BLOCK, DSA_DIM), head),
                  pl.BlockSpec((IDX_HEADS, DSA_BLOCK, IDX_DIM), head),
                  pl.BlockSpec((DSA_BLOCK, 128), row),
                  pl.BlockSpec((None, tp, DSA_DIM), per_b),
                  pl.BlockSpec((None, tp, DSA_DIM), per_b),
                  pl.BlockSpec((None, tp, IDX_DIM), per_b),
                  pl.BlockSpec((DSA_HEADS, DSA_DIM, DSA_OUT_DIM), lambda b, q: (0, 0, 0))],
        out_specs=pl.BlockSpec((DSA_BLOCK, 512), row),
        out_shape=jax.ShapeDtypeStruct((n, 512), BF16),
        scratch_shapes=[pltpu.VMEM((tp, DSA_BLOCK), F32),
                        pltpu.VMEM((DSA_BLOCK, tp), F32),
                        pltpu.VMEM((DSA_HEADS * DSA_BLOCK, LANES), F32),
                        pltpu.VMEM((DSA_HEADS * DSA_BLOCK, LANES), F32),
                        pltpu.VMEM((DSA_HEADS * DSA_BLOCK, DSA_DIM), F32)],
        compiler_params=_params(("arbitrary", "arbitrary")),
        name="dsa",
    )(dq, iq, misc, kpad, vpad, kipad, wuv)


def _post_body(ret_ref, att_ref, x_ref, w_ref, g_ref, h1_ref, hnt_ref):
    y = jnp.dot(ret_ref[...], w_ref[0:512, :], preferred_element_type=F32)
    y = y + jnp.dot(att_ref[...], w_ref[512:1024, :], preferred_element_type=F32)
    h1 = x_ref[...] + y
    h1_ref[...] = h1
    hnt_ref[...] = _rms(h1, g_ref[...]).T.astype(BF16)


def _post(ret, att, x2d, w_out, g, tm):
    n = x2d.shape[0]
    row = lambda i: (i, 0)
    return pl.pallas_call(
        _post_body,
        grid=(n // tm,),
        in_specs=[pl.BlockSpec((tm, 512), row), pl.BlockSpec((tm, 512), row),
                  pl.BlockSpec((tm, D_MODEL), row),
                  pl.BlockSpec((D_MODEL, D_MODEL), lambda i: (0, 0)),
                  pl.BlockSpec((1, D_MODEL), lambda i: (0, 0))],
        out_specs=(pl.BlockSpec((tm, D_MODEL), row), pl.BlockSpec((D_MODEL, tm), lambda i: (0, i))),
        out_shape=(jax.ShapeDtypeStruct((n, D_MODEL), F32), jax.ShapeDtypeStruct((D_MODEL, n), BF16)),
        compiler_params=_params(("arbitrary",)),
        name="post",
    )(ret, att, x2d, w_out, g)


def _oddeven_merge_sort_pairs(n):
    pairs = []
    p = 1
    while p < n:
        k = p
        while k >= 1:
            for j in range(k % p, n - k, 2 * k):
                for i in range(min(k, n - j - k)):
                    if (i + j) // (2 * p) == (i + j + k) // (2 * p):
                        pairs.append((i + j, i + j + k))
            k //= 2
        p *= 2
    return tuple(pairs)


def _bitonic_merge_pairs(n):
    pairs = []
    d = n // 2
    while d >= 1:
        pairs += [(i, i + d) for i in range(n) if (i & d) == 0]
        d //= 2
    return tuple(pairs)


_SORT16 = _oddeven_merge_sort_pairs(PEER_TOPK)
_MERGE16 = _bitonic_merge_pairs(PEER_TOPK)


def _top16_rows(s):
    def exchange(x, pairs):
        for i, j in pairs:
            x[i], x[j] = jnp.maximum(x[i], x[j]), jnp.minimum(x[i], x[j])

    x = [s[SUBLANES * j:SUBLANES * (j + 1), :] for j in range(PEER_TOPK)]
    exchange(x, _SORT16)
    shift = SUBLANES // 2
    while shift >= 1:
        other = [pltpu.roll(v, shift, 0) for v in x]
        x = [jnp.maximum(x[i], other[PEER_TOPK - 1 - i]) for i in range(PEER_TOPK)]
        exchange(x, _MERGE16)
        shift //= 2
    return [v[0:1] for v in x]


def _prefix_count(pred, w):
    m8 = pred(w[7])
    m4 = pred(jnp.where(m8, w[11], w[3]))
    m2 = pred(jnp.where(m8, jnp.where(m4, w[13], w[9]), jnp.where(m4, w[5], w[1])))
    m1 = pred(jnp.where(m8, jnp.where(m4, jnp.where(m2, w[14], w[12]), jnp.where(m2, w[10], w[8])),
                        jnp.where(m4, jnp.where(m2, w[6], w[4]), jnp.where(m2, w[2], w[0]))))
    bit = lambda m, v: jnp.where(m, v, 0.0)
    return bit(m8, 8.0) + bit(m4, 4.0) + bit(m2, 2.0) + bit(m1, 1.0) + bit(pred(w[15]), 1.0)


def _top_desc(s, k, dst_ref, ts):
    cur = s
    for r in range(k):
        m = jnp.max(cur, axis=0, keepdims=True)
        dst_ref[r:r + 1, ts] = m
        if r + 1 < k:
            cur = jnp.where(cur == m, -jnp.inf, cur)


def _psel_body(hnt_ref, wq_ref, sk_ref, cnt_ref, e1_ref, r2_ref, e2_ref, s_ref, v1_ref, v2_ref, vc_ref):
    hnt = hnt_ref[...]
    for h in range(PEER_HEADS):
        q = jnp.dot(wq_ref[h * 256:(h + 1) * 256, :], hnt, preferred_element_type=F32)
        s_ref[0] = jnp.dot(sk_ref[h, 0], q[:128].astype(BF16), preferred_element_type=F32)
        s_ref[1] = jnp.dot(sk_ref[h, 1], q[128:].astype(BF16), preferred_element_type=F32)
        def lane_tile(tc, carry, h=h):
            ts = pl.ds(pl.multiple_of(tc * LANES, LANES), LANES)
            s1 = s_ref[0, :, ts]
            s2 = s_ref[1, :, ts]
            w1 = _top16_rows(s1)
            w2 = _top16_rows(s2)
            for b in range(PEER_TOPK):
                v1_ref[b:b + 1, ts] = w1[b]
                v2_ref[b:b + 1, ts] = w2[b]
            v1 = v1_ref[:, ts]
            v2 = v2_ref[:, ts]
            cand = jnp.concatenate([v2 + v1[0:1]] + [v2[0:8] + v1[a:a + 1] for a in range(1, 8)]
                                   + [v1[8:16] + v2[0:1]], axis=0)
            _top_desc(cand, PEER_TOPK, vc_ref, ts)
            tau = vc_ref[PEER_TOPK - 1:PEER_TOPK, ts]
            top = vc_ref[0:1, ts]
            zsum = jnp.sum(jnp.where(cand >= tau, jnp.exp(cand - top), 0.0), axis=0, keepdims=True)
            cnt_ref[h, :, ts] = _prefix_count(lambda t: s1 + t >= tau, w2)
            e1_ref[h, :, ts] = jnp.exp(s1 - v1[0:1]) / zsum
            rank2 = _prefix_count(lambda t: t > s2, w2)
            r2_ref[h, :, ts] = pltpu.bitcast(rank2.astype(BF16), I32)
            e2_ref[h, :, ts] = pltpu.bitcast(jnp.exp(s2 - v2[0:1]).astype(BF16), I32)
            return carry

        lax.fori_loop(0, hnt.shape[1] // LANES, lane_tile, 0)


def _psel(hnt, wq_t, sk, tn):
    n = hnt.shape[1]
    spec = pl.BlockSpec((PEER_HEADS, PEER_NKEYS, tn), lambda i: (0, 0, i))
    shp32 = jax.ShapeDtypeStruct((PEER_HEADS, PEER_NKEYS, n), F32)
    shp16 = jax.ShapeDtypeStruct((PEER_HEADS, PEER_NKEYS // 2, n), I32)
    spec16 = pl.BlockSpec((PEER_HEADS, PEER_NKEYS // 2, tn), lambda i: (0, 0, i))
    return pl.pallas_call(
        _psel_body,
        grid=(n // tn,),
        in_specs=[pl.BlockSpec((D_MODEL, tn), lambda i: (0, i)),
                  pl.BlockSpec((PEER_HEADS * 256, D_MODEL), lambda i: (0, 0)),
                  pl.BlockSpec((PEER_HEADS, 2, PEER_NKEYS, 128), lambda i: (0, 0, 0, 0))],
        out_specs=(spec, spec, spec16, spec16),
        out_shape=(shp32, shp32, shp16, shp16),
        scratch_shapes=[pltpu.VMEM((2, PEER_NKEYS, tn), F32)] + [pltpu.VMEM((PEER_TOPK, tn), F32)] * 3,
        compiler_params=_params(("arbitrary",)),
        name="psel",
    )(hnt, wq_t, sk)


def _pdense_body(*refs, eb, nchunk, nblk):
    hnt_ref, u_ref, vt_ref = refs[:3]
    cnt_refs = refs[3:3 + nchunk]
    e1_refs = refs[3 + nchunk:3 + 2 * nchunk]
    r2_ref, e2_ref, h1_ref, g_ref, out_ref, acc_ref, a_ref, coef_ref = refs[3 + 2 * nchunk:]
    s = pl.program_id(1)
    nsub = eb // PEER_NKEYS

    def row_bf16(ref, h, i1):
        return jnp.broadcast_to(ref[h, pl.ds(i1, 1), :], (PEER_NKEYS, LANES)).astype(BF16)

    @pl.when(s == 0)
    def _():
        acc_ref[...] = jnp.zeros(acc_ref.shape, F32)

    for part in range(eb // PDENSE_ROWS):
        rows = slice(part * PDENSE_ROWS, (part + 1) * PDENSE_ROWS)
        a_ref[rows, :] = jnp.dot(u_ref[rows, :], hnt_ref[...], preferred_element_type=F32)

    for c in range(nsub):
        rows = slice(c * PEER_NKEYS, (c + 1) * PEER_NKEYS)
        for tc in range(nchunk):
            ts = slice(tc * LANES, (tc + 1) * LANES)
            gate = None
            for h in range(PEER_HEADS):
                r2 = pltpu.bitcast(r2_ref[h, :, ts], BF16)
                e2 = pltpu.bitcast(e2_ref[h, :, ts], BF16)
                cnt = row_bf16(cnt_refs[tc], h, s * nsub + c)
                e1 = row_bf16(e1_refs[tc], h, s * nsub + c)
                term = jnp.where(r2 < cnt, e2, jnp.zeros((), BF16)) * e1
                gate = term if gate is None else gate + term
            at = a_ref[rows, ts]
            gelu = (0.5 * at) * (1.0 + lax.erf(at * (0.5 ** 0.5)))
            coef_ref[rows, ts] = gelu.astype(BF16) * gate

    acc_ref[...] += jnp.dot(vt_ref[...], coef_ref[...], preferred_element_type=F32)

    @pl.when(s == nblk - 1)
    def _():
        out_ref[...] = _rms(h1_ref[...] + acc_ref[...].T, g_ref[...])


def _pdense(hnt, u, vt, cnt, e1, r2, e2, h1, g, tn, eb):
    n = hnt.shape[1]
    nblk = u.shape[0] // eb
    assert nblk >= 2 and vt.shape == (nblk, D_MODEL, eb)
    nchunk = tn // LANES
    chunk = [pl.BlockSpec((PEER_HEADS, PEER_NKEYS, LANES), lambda i, s, c=c: (0, 0, i * nchunk + c))
             for c in range(nchunk)]
    sel16 = pl.BlockSpec((PEER_HEADS, PEER_NKEYS // 2, tn), lambda i, s: (0, 0, i))
    return pl.pallas_call(
        functools.partial(_pdense_body, eb=eb, nchunk=nchunk, nblk=nblk),
        grid=(n // tn, nblk),
        in_specs=[pl.BlockSpec((D_MODEL, tn), lambda i, s: (0, i)),
                  pl.BlockSpec((eb, D_MODEL), lambda i, s: (s, 0)),
                  pl.BlockSpec((None, D_MODEL, eb), lambda i, s: (s, 0, 0))]
                 + chunk + chunk
                 + [sel16, sel16,
                    pl.BlockSpec((tn, D_MODEL), lambda i, s: (i, 0)),
                    pl.BlockSpec((1, D_MODEL), lambda i, s: (0, 0))],
        out_specs=pl.BlockSpec((tn, D_MODEL), lambda i, s: (i, 0)),
        out_shape=jax.ShapeDtypeStruct((n, D_MODEL), F32),
        scratch_shapes=[pltpu.VMEM((D_MODEL, tn), F32),
                        pltpu.VMEM((eb, tn), F32), pltpu.VMEM((eb, tn), BF16)],
        compiler_params=_params(("arbitrary", "arbitrary")),
        name="pdense",
    )(hnt, u, vt, *([cnt] * nchunk), *([e1] * nchunk), r2, e2, h1, g)


def kernel(x, meta_tokens, norm_mix, w_in, w_uv, w_out, norm_ffn, peer_wq, peer_subkeys,
           peer_u, peer_v, norm_final):
    batch, seq, d = x.shape
    assert d == D_MODEL and norm_mix.shape[0] == 1 and seq % DSA_BLOCK == 0
    n = batch * seq
    topk = min(DSA_TOPK_MAX, seq // 4)
    tm = TOKEN_TILE if seq % TOKEN_TILE == 0 else DSA_BLOCK

    w = jnp.pad(w_in[0], ((0, 0), (0, IN_WIDTH_PAD - IN_WIDTH))).astype(BF16)
    g_mix = norm_mix[0][None, :]
    x2d = x.reshape(n, d)
    pos_f = jnp.arange(seq, dtype=F32) + float(N_META)
    pos_m = jnp.arange(N_META, dtype=F32)

    rq, rk, rv, rg, dq, dk, dv, iq, misc = _proj(x2d, g_mix, w, _rope_tables(pos_f), tm, seq // tm)
    _, mk, mv, _, _, mdk, mdv, _, mmisc = _proj(meta_tokens, g_mix, w, _rope_tables(pos_m), N_META, 1)

    ret = _retention(rq, rk, rv, rg, mk, mv, batch, seq)

    def keys(frames, meta):
        width = frames.shape[-1]
        lead = jnp.concatenate([jnp.zeros((LANES - N_META, width), frames.dtype), meta], axis=0)
        lead = jnp.broadcast_to(lead[None], (batch, LANES, width))
        return jnp.concatenate([lead, frames.reshape(batch, seq, width)], axis=1)

    kpad = keys(dk, mdk)
    vpad = keys(dv, mdv)
    kipad = keys(misc[:, :IDX_DIM].astype(BF16), mmisc[:, :IDX_DIM].astype(BF16))
    att = _dsa(dq, iq, misc, kpad, vpad, kipad, w_uv[0].astype(BF16), batch, seq, topk)

    h1, hnt = _post(ret, att, x2d, w_out[0].astype(BF16), norm_ffn[0][None, :], tm)

    wq_t = peer_wq[0].reshape(d, PEER_HEADS * 256).T.astype(BF16)
    cnt, e1, r2, e2 = _psel(hnt, wq_t, peer_subkeys[0].astype(BF16), tm)
    vt = peer_v[0].reshape(-1, PDENSE_EXPERTS, d).transpose(0, 2, 1).astype(BF16)
    out = _pdense(hnt, peer_u[0].astype(BF16), vt, cnt, e1, r2, e2, h1,
                  norm_final[None, :], tm, PDENSE_EXPERTS)
    return out.reshape(batch, seq, d)
```

```python
import functools

import jax
import jax.numpy as jnp
from jax import lax
from jax.experimental import pallas as pl
from jax.experimental.pallas import tpu as pltpu

F32 = jnp.float32
BF16 = jnp.bfloat16
I32 = jnp.int32

D_MODEL = 1024
N_META = 16
CHUNK = 64
RET_HEADS = 4
RET_DIM = 128
DSA_HEADS = 8
DSA_DIM = 128
DSA_OUT_DIM = 64
IDX_HEADS = 8
IDX_DIM = 64
DSA_TOPK_MAX = 256
ROPE_THETA = 10000.0
PEER_HEADS = 8
PEER_NKEYS = 128
PEER_TOPK = 16
EPS = 1e-6
NEG = -1e30
INT_MIN = -(2 ** 31)

IN_WIDTH = 3912
IN_WIDTH_PAD = 3968
LANES = 128
RET_BLOCK = 256
PDENSE_EXPERTS = 2048
PDENSE_ROWS = 512
DSA_BLOCK = 256
DSA_KEY_BITS = 31
DSA_BITS_PER_CHECK = 4
DSA_REFINE_STEPS = 40
SUBLANES = 8
TOKEN_TILE = 512
V7X_VMEM_BYTES = 64 * 1024 * 1024
VMEM_LIMIT = V7X_VMEM_BYTES * 7 // 8

NT = (((1,), (1,)), ((), ()))
TN = (((0,), (0,)), ((), ()))


def _rms(x, g):
    return x * lax.rsqrt(jnp.mean(x * x, axis=-1, keepdims=True) + EPS) * g


def _params(sem):
    return pltpu.CompilerParams(dimension_semantics=sem, vmem_limit_bytes=VMEM_LIMIT)


def _proj_body(x_ref, g_ref, w_ref, tab_ref, rq_ref, rk_ref, rv_ref, rg_ref, dq_ref, dk_ref,
               dv_ref, iq_ref, misc_ref):
    xn = _rms(x_ref[...], g_ref[...]).astype(BF16)
    cos_a, sin_a, cos_b, sin_b1, sin_b2, cos_m, sin_m1, sin_m2 = (tab_ref[i] for i in range(8))

    def mm(lo, n):
        return jnp.dot(xn, w_ref[:, lo:lo + n], preferred_element_type=F32)

    def rope_a(y):
        return y * cos_a + pltpu.roll(y, 64, 1) * sin_a

    def rope_b(y, c, s1, s2):
        return y * c + pltpu.roll(y, 96, 1) * s1 + pltpu.roll(y, 32, 1) * s2

    y = mm(0, 512)
    for h in range(RET_HEADS):
        rq_ref[:, h * 128:(h + 1) * 128] = rope_a(y[:, h * 128:(h + 1) * 128]).astype(BF16)
    y = mm(512, 512)
    for h in range(RET_HEADS):
        rk_ref[:, h * 128:(h + 1) * 128] = (
            rope_a(y[:, h * 128:(h + 1) * 128]) * (RET_DIM ** -0.5)).astype(BF16)
    rv_ref[...] = mm(1024, 512).astype(BF16)
    rg_ref[...] = mm(1536, 512)
    y = mm(2048, 1024)
    for h in range(DSA_HEADS):
        dq_ref[h] = (rope_a(y[:, h * 128:(h + 1) * 128]) * (DSA_DIM ** -0.5)).astype(BF16)
    dk_ref[...] = rope_a(mm(3072, 128)).astype(BF16)
    dv_ref[...] = mm(3200, 128).astype(BF16)
    y = mm(3328, 512)
    for p in range(IDX_HEADS // 2):
        o = (rope_b(y[:, p * 128:(p + 1) * 128], cos_b, sin_b1, sin_b2) * (IDX_DIM ** -0.5)).astype(BF16)
        iq_ref[2 * p] = o[:, :64]
        iq_ref[2 * p + 1] = o[:, 64:]
    misc_ref[...] = rope_b(mm(3840, 128), cos_m, sin_m1, sin_m2)


def _proj(x2d, g, w, tab, tm, tab_blocks):
    n = x2d.shape[0]
    row = lambda i: (i, 0)
    head = lambda i: (0, i, 0)
    out_shape = (
        jax.ShapeDtypeStruct((n, 512), BF16), jax.ShapeDtypeStruct((n, 512), BF16),
        jax.ShapeDtypeStruct((n, 512), BF16), jax.ShapeDtypeStruct((n, 512), F32),
        jax.ShapeDtypeStruct((DSA_HEADS, n, 128), BF16), jax.ShapeDtypeStruct((n, 128), BF16),
        jax.ShapeDtypeStruct((n, 128), BF16), jax.ShapeDtypeStruct((IDX_HEADS, n, 64), BF16),
        jax.ShapeDtypeStruct((n, 128), F32))
    out_specs = (
        pl.BlockSpec((tm, 512), row), pl.BlockSpec((tm, 512), row), pl.BlockSpec((tm, 512), row),
        pl.BlockSpec((tm, 512), row), pl.BlockSpec((DSA_HEADS, tm, 128), head),
        pl.BlockSpec((tm, 128), row), pl.BlockSpec((tm, 128), row),
        pl.BlockSpec((IDX_HEADS, tm, 64), head), pl.BlockSpec((tm, 128), row))
    return pl.pallas_call(
        _proj_body,
        grid=(n // tm,),
        in_specs=[pl.BlockSpec((tm, D_MODEL), row),
                  pl.BlockSpec((1, D_MODEL), lambda i: (0, 0)),
                  pl.BlockSpec((D_MODEL, IN_WIDTH_PAD), lambda i: (0, 0)),
                  pl.BlockSpec((8, tm, 128), lambda i: (0, i % tab_blocks, 0))],
        out_specs=out_specs,
        out_shape=out_shape,
        compiler_params=_params(("arbitrary",)),
        name="proj",
    )(x2d, g, w, tab)


def _rope_tables(pos):
    p = pos.shape[0]
    inv = ROPE_THETA ** (-jnp.arange(0, 64, dtype=F32) * 2.0 / 128)
    ang = pos[:, None] * inv[None, :]
    c, s = jnp.cos(ang), jnp.sin(ang)
    cos_a = jnp.concatenate([c, c], -1)
    sin_a = jnp.concatenate([-s, s], -1)
    inv = ROPE_THETA ** (-jnp.arange(0, 32, dtype=F32) * 2.0 / 64)
    ang = pos[:, None] * inv[None, :]
    c, s = jnp.cos(ang), jnp.sin(ang)
    z = jnp.zeros_like(s)
    z64 = jnp.zeros((p, 64), F32)
    cos_b = jnp.concatenate([c, c, c, c], -1)
    sin_b1 = jnp.concatenate([-s, z, -s, z], -1)
    sin_b2 = jnp.concatenate([z, s, z, s], -1)
    cos_m = jnp.concatenate([c, c, jnp.full((p, 64), IDX_HEADS ** -0.5, F32)], -1)
    sin_m1 = jnp.concatenate([-s, z, z64], -1)
    sin_m2 = jnp.concatenate([z, s, z64], -1)
    return jnp.stack([cos_a, sin_a, cos_b, sin_b1, sin_b2, cos_m, sin_m1, sin_m2])


def _ret_body(rq_ref, rk_ref, rv_ref, rg_ref, mk_ref, mv_ref, dmat_ref, xi_ref, zeta_ref,
              mzeta_ref, gsc_ref, out_ref, r_ref):
    @pl.when(pl.program_id(1) == 0)
    def _():
        for h in range(RET_HEADS):
            hs = slice(h * 128, (h + 1) * 128)
            kz = (mk_ref[:, hs].astype(F32) * mzeta_ref[h]).astype(BF16)
            r_ref[h] = lax.dot_general(kz, mv_ref[:, hs], TN, preferred_element_type=F32)

    for h in range(RET_HEADS):
        hs = slice(h * 128, (h + 1) * 128)
        q = rq_ref[:, hs]
        k = rk_ref[:, hs]
        v = rv_ref[:, hs]
        s = lax.dot_general(q, k, NT, preferred_element_type=F32) * dmat_ref[h]
        o = jnp.dot(s.astype(BF16), v, preferred_element_type=F32)
        r = r_ref[h]
        qx = (q.astype(F32) * xi_ref[h]).astype(BF16)
        o = o + jnp.dot(qx, r.astype(BF16), preferred_element_type=F32)
        kz = (k.astype(F32) * zeta_ref[h]).astype(BF16)
        u = lax.dot_general(kz, v, TN, preferred_element_type=F32)
        r_ref[h] = r * gsc_ref[h] + u
        mu = jnp.mean(o, axis=-1, keepdims=True)
        d = o - mu
        var = jnp.mean(d * d, axis=-1, keepdims=True)
        on = d * lax.rsqrt(var + EPS)
        g = rg_ref[:, hs]
        out_ref[:, hs] = (g * jax.nn.sigmoid(g) * on).astype(BF16)


def _retention(rq, rk, rv, rg, mk, mv, batch, seq):
    n = rq.shape[0]
    nb = seq // RET_BLOCK
    lg = jnp.log(1.0 - 2.0 ** (-5.0 - jnp.arange(RET_HEADS, dtype=F32)))
    i = jnp.arange(RET_BLOCK, dtype=F32)
    ci = jnp.arange(RET_BLOCK) // CHUNK
    vis = (ci[None, :] <= ci[:, None])
    dmat = jnp.where(vis[None], jnp.exp(lg[:, None, None] * jnp.abs(i[:, None] - i[None, :])), 0.0)
    xi = jnp.broadcast_to(jnp.exp(lg[:, None] * (i + 1.0)[None, :])[:, :, None], (RET_HEADS, RET_BLOCK, 128))
    zeta = jnp.broadcast_to(jnp.exp(lg[:, None] * (RET_BLOCK - 1.0 - i)[None, :])[:, :, None],
                            (RET_HEADS, RET_BLOCK, 128))
    im = jnp.arange(N_META, dtype=F32)
    mzeta = jnp.broadcast_to(jnp.exp(lg[:, None] * (N_META - 1.0 - im)[None, :])[:, :, None],
                             (RET_HEADS, N_META, 128))
    gsc = jnp.broadcast_to(jnp.exp(lg * RET_BLOCK)[:, None, None], (RET_HEADS, 1, 128))
    row = lambda b, s: (b * nb + s, 0)
    c2 = lambda b, s: (0, 0)
    c3 = lambda b, s: (0, 0, 0)
    return pl.pallas_call(
        _ret_body,
        grid=(batch, nb),
        in_specs=[pl.BlockSpec((RET_BLOCK, 512), row), pl.BlockSpec((RET_BLOCK, 512), row),
                  pl.BlockSpec((RET_BLOCK, 512), row), pl.BlockSpec((RET_BLOCK, 512), row),
                  pl.BlockSpec((N_META, 512), c2), pl.BlockSpec((N_META, 512), c2),
                  pl.BlockSpec((RET_HEADS, RET_BLOCK, RET_BLOCK), c3),
                  pl.BlockSpec((RET_HEADS, RET_BLOCK, 128), c3),
                  pl.BlockSpec((RET_HEADS, RET_BLOCK, 128), c3),
                  pl.BlockSpec((RET_HEADS, N_META, 128), c3),
                  pl.BlockSpec((RET_HEADS, 1, 128), c3)],
        out_specs=pl.BlockSpec((RET_BLOCK, 512), row),
        out_shape=jax.ShapeDtypeStruct((n, 512), BF16),
        scratch_shapes=[pltpu.VMEM((RET_HEADS, 128, 128), F32)],
        compiler_params=_params(("arbitrary", "arbitrary")),
        name="retention",
    )(rq, rk, rv, rg, mk, mv, dmat, xi, zeta, mzeta, gsc)


def _dsa_body(qa_ref, qi_ref, misc_ref, k_ref, v_ref, ki_ref, wuv_ref, out_ref,
              isc_ref, bias_ref, m_ref, l_ref, acc_ref, *, topk):
    qb = pl.program_id(1)
    tq = DSA_BLOCK
    nh = DSA_HEADS

    wt = misc_ref[...].T
    qi = qi_ref[...].reshape(IDX_HEADS * tq, IDX_DIM)

    def isc_tile(koff, width, allowed):
        kt = ki_ref[pl.ds(koff, width), :]
        z = lax.dot_general(kt, qi, NT, preferred_element_type=F32)
        isc = None
        for h in range(IDX_HEADS):
            term = jnp.maximum(z[:, h * tq:(h + 1) * tq], 0.0) * wt[64 + h:65 + h, :]
            isc = term if isc is None else isc + term
        if allowed is not None:
            isc = jnp.where(allowed, isc, -jnp.inf)
        isc_ref[pl.ds(koff, width), :] = isc

    isc_tile(0, LANES, lax.broadcasted_iota(I32, (LANES, tq), 0) >= LANES - N_META)

    def body_a(t, c):
        isc_tile(pl.multiple_of(LANES + t * tq, LANES), tq, None)
        return c

    lax.fori_loop(0, qb, body_a, 0)
    keyc = lax.broadcasted_iota(I32, (tq, tq), 0) // CHUNK
    qryc = lax.broadcasted_iota(I32, (tq, tq), 1) // CHUNK
    diag_off = pl.multiple_of(LANES + qb * tq, LANES)
    isc_tile(diag_off, tq, keyc <= qryc)

    kf = float(topk)
    sub = SUBLANES

    def key_to_f32(key):
        return lax.bitcast_convert_type(jnp.where(key < 0, key ^ jnp.int32(0x7FFFFFFF), key), F32)

    def count(pred):
        def slab(off, a):
            hit = jnp.where(pred(isc_ref[pl.ds(off, LANES), :]), 1.0, 0.0)
            return a + jnp.sum(hit.reshape(LANES // sub, sub, tq), axis=0)

        def body(t, a):
            off = pl.multiple_of(LANES + t * (2 * LANES), LANES)
            return slab(pl.multiple_of(off + LANES, LANES), slab(off, a))

        a = lax.fori_loop(0, qb + 1, body, slab(0, jnp.zeros((sub, tq), F32)))
        return jnp.sum(a, axis=0, keepdims=True)

    def count_ge(cand):
        return count(lambda x: x >= cand)

    lowest = jnp.full((1, tq), -3.0e38, F32)
    n_real = count_ge(lowest)
    c_pos = count_ge(jnp.zeros((1, tq), F32))
    pos = c_pos >= kf
    base0 = jnp.where(pos, jnp.int32(0), jnp.int32(INT_MIN))
    cnt0 = jnp.where(n_real <= kf, kf, jnp.where(pos, c_pos, n_real))

    def unfinished(cnt):
        return jnp.max(jnp.where(cnt != kf, 1.0, 0.0))

    def try_bit(bit, base, cnt):
        cand = base | lax.shift_left(jnp.int32(1), bit)
        c = count_ge(key_to_f32(cand))
        ok = c >= kf
        return jnp.where(ok, cand, base), jnp.where(ok, c, cnt)

    base, cnt = base0, cnt0
    groups = DSA_KEY_BITS // DSA_BITS_PER_CHECK
    for bit in range(DSA_KEY_BITS - 1, DSA_BITS_PER_CHECK * groups - 1, -1):
        base, cnt = try_bit(bit, base, cnt)

    zero_tie = jnp.logical_and(pos, count(lambda x: x > 0.0) < kf)

    def group_cond(state):
        g, _, cnt = state
        return jnp.logical_and(g >= 0, unfinished(jnp.where(zero_tie, kf, cnt)) > 0.0)

    def group_body(state):
        g, base, cnt = state
        for k in range(DSA_BITS_PER_CHECK - 1, -1, -1):
            base, cnt = try_bit(g * DSA_BITS_PER_CHECK + k, base, cnt)
        return g - 1, base, cnt

    _, base, cnt = lax.while_loop(group_cond, group_body, (jnp.int32(groups - 1), base, cnt))

    def refine_cond(state):
        it, _, _, _, go = state
        return jnp.logical_and(it < DSA_REFINE_STEPS, go > 0.0)

    def refine_body(state):
        it, lo, hi, cnt, _ = state
        mid = lo + 0.5 * (hi - lo)
        c = count_ge(mid)
        ok = c >= kf
        cnt = jnp.where(ok, c, cnt)
        moving = jnp.logical_and(jnp.logical_and(mid > lo, mid < hi),
                                 jnp.logical_and(cnt != kf, jnp.logical_not(zero_tie)))
        return it + 1, jnp.where(ok, mid, lo), jnp.where(ok, hi, mid), cnt, jnp.max(jnp.where(moving, 1.0, 0.0))

    _, thr, _, cnt, _ = lax.while_loop(
        refine_cond, refine_body,
        (jnp.int32(0), key_to_f32(base), key_to_f32(base + 1), cnt, unfinished(jnp.where(zero_tie, kf, cnt))))
    thr = jnp.where(n_real <= kf, lowest, thr)

    need = kf - count(lambda x: x > thr)
    lower = (lax.broadcasted_iota(I32, (tq, tq), 1) <= lax.broadcasted_iota(I32, (tq, tq), 0)).astype(BF16)

    def mask_tile(off, width, ties_before):
        x = isc_ref[pl.ds(off, width), :]
        tie = x == thr
        tie_f = jnp.where(tie, 1.0, 0.0)
        upto = jnp.dot(lower[:width, :width], tie_f.astype(BF16), preferred_element_type=F32)
        keep = jnp.logical_or(x > thr, jnp.logical_and(tie, ties_before + upto - tie_f < need))
        bias_ref[:, pl.ds(off, width)] = jnp.where(keep, 0.0, NEG).T
        return ties_before + upto[width - 1:width, :]

    def body_m(t, ties_before):
        return mask_tile(pl.multiple_of(LANES + t * tq, LANES), tq, ties_before)

    lax.fori_loop(0, qb + 1, body_m, mask_tile(0, LANES, jnp.zeros((1, tq), F32)))

    m_ref[...] = jnp.full(m_ref.shape, NEG, F32)
    l_ref[...] = jnp.zeros(l_ref.shape, F32)
    acc_ref[...] = jnp.zeros(acc_ref.shape, F32)
    qa = qa_ref[...].reshape(nh * tq, DSA_DIM)

    def att_tile(koff, width):
        kt = k_ref[pl.ds(koff, width), :]
        vt = v_ref[pl.ds(koff, width), :]
        s = lax.dot_general(qa, kt, NT, preferred_element_type=F32).reshape(nh, tq, width)
        reps = width // LANES
        s = (s + bias_ref[:, pl.ds(koff, width)][None]).reshape(nh * tq, width)
        m_prev = m_ref[...]
        m_new = jnp.maximum(m_prev, jnp.max(s, axis=1, keepdims=True))
        alpha = jnp.exp(m_prev - m_new)
        m_w = m_new if reps == 1 else jnp.concatenate([m_new] * reps, axis=1)
        p = jnp.exp(s - m_w)
        l_ref[...] = alpha * l_ref[...] + jnp.sum(p, axis=1, keepdims=True)
        acc_ref[...] = alpha * acc_ref[...] + jnp.dot(p.astype(BF16), vt, preferred_element_type=F32)
        m_ref[...] = m_new

    att_tile(0, LANES)

    def body_c(t, c):
        att_tile(pl.multiple_of(LANES + t * (2 * tq), LANES), 2 * tq)
        return c

    lax.fori_loop(0, (qb + 1) // 2, body_c, 0)

    @pl.when(qb % 2 == 0)
    def _():
        att_tile(diag_off, tq)

    o = (acc_ref[...] / l_ref[...]).astype(BF16).reshape(nh, tq, DSA_DIM)
    for h in range(nh):
        out_ref[:, h * DSA_OUT_DIM:(h + 1) * DSA_OUT_DIM] = jnp.dot(
            o[h], wuv_ref[h], preferred_element_type=F32).astype(BF16)


def _dsa(dq, iq, misc, kpad, vpad, kipad, wuv, batch, seq, topk):
    n = misc.shape[0]
    nq = seq // DSA_BLOCK
    tp = kpad.shape[1]
    head = lambda b, q: (0, b * nq + q, 0)
    row = lambda b, q: (b * nq + q, 0)
    per_b = lambda b, q: (b, 0, 0)
    return pl.pallas_call(
        functools.partial(_dsa_body, topk=topk),
        grid=(batch, nq),
        in_specs=[pl.BlockSpec((DSA_HEADS, DSA_BLOCK, DSA_DIM), head),
                  pl.BlockSpec((IDX_HEADS, DSA_BLOCK, IDX_DIM), head),
                  pl.BlockSpec((DSA_BLOCK, 128), row),
                  pl.BlockSpec((None, tp, DSA_DIM), per_b),
                  pl.BlockSpec((None, tp, DSA_DIM), per_b),
                  pl.BlockSpec((None, tp, IDX_DIM), per_b),
                  pl.BlockSpec((DSA_HEADS, DSA_DIM, DSA_OUT_DIM), lambda b, q: (0, 0, 0))],
        out_specs=pl.BlockSpec((DSA_BLOCK, 512), row),
        out_shape=jax.ShapeDtypeStruct((n, 512), BF16),
        scratch_shapes=[pltpu.VMEM((tp, DSA_BLOCK), F32),
                        pltpu.VMEM((DSA_BLOCK, tp), F32),
                        pltpu.VMEM((DSA_HEADS * DSA_BLOCK, LANES), F32),
                        pltpu.VMEM((DSA_HEADS * DSA_BLOCK, LANES), F32),
                        pltpu.VMEM((DSA_HEADS * DSA_BLOCK, DSA_DIM), F32)],
        compiler_params=_params(("arbitrary", "arbitrary")),
        name="dsa",
    )(dq, iq, misc, kpad, vpad, kipad, wuv)


def _post_body(ret_ref, att_ref, x_ref, w_ref, g_ref, h1_ref, hnt_ref):
    y = jnp.dot(ret_ref[...], w_ref[0:512, :], preferred_element_type=F32)
    y = y + jnp.dot(att_ref[...], w_ref[512:1024, :], preferred_element_type=F32)
    h1 = x_ref[...] + y
    h1_ref[...] = h1
    hnt_ref[...] = _rms(h1, g_ref[...]).T.astype(BF16)


def _post(ret, att, x2d, w_out, g, tm):
    n = x2d.shape[0]
    row = lambda i: (i, 0)
    return pl.pallas_call(
        _post_body,
        grid=(n // tm,),
        in_specs=[pl.BlockSpec((tm, 512), row), pl.BlockSpec((tm, 512), row),
                  pl.BlockSpec((tm, D_MODEL), row),
                  pl.BlockSpec((D_MODEL, D_MODEL), lambda i: (0, 0)),
                  pl.BlockSpec((1, D_MODEL), lambda i: (0, 0))],
        out_specs=(pl.BlockSpec((tm, D_MODEL), row), pl.BlockSpec((D_MODEL, tm), lambda i: (0, i))),
        out_shape=(jax.ShapeDtypeStruct((n, D_MODEL), F32), jax.ShapeDtypeStruct((D_MODEL, n), BF16)),
        compiler_params=_params(("arbitrary",)),
        name="post",
    )(ret, att, x2d, w_out, g)


def _oddeven_merge_sort_pairs(n):
    pairs = []
    p = 1
    while p < n:
        k = p
        while k >= 1:
            for j in range(k % p, n - k, 2 * k):
                for i in range(min(k, n - j - k)):
                    if (i + j) // (2 * p) == (i + j + k) // (2 * p):
                        pairs.append((i + j, i + j + k))
            k //= 2
        p *= 2
    return tuple(pairs)


def _bitonic_merge_pairs(n):
    pairs = []
    d = n // 2
    while d >= 1:
        pairs += [(i, i + d) for i in range(n) if (i & d) == 0]
        d //= 2
    return tuple(pairs)


_SORT16 = _oddeven_merge_sort_pairs(PEER_TOPK)
_MERGE16 = _bitonic_merge_pairs(PEER_TOPK)


def _top16_rows(s):
    def exchange(x, pairs):
        for i, j in pairs:
            x[i], x[j] = jnp.maximum(x[i], x[j]), jnp.minimum(x[i], x[j])

    x = [s[SUBLANES * j:SUBLANES * (j + 1), :] for j in range(PEER_TOPK)]
    exchange(x, _SORT16)
    shift = SUBLANES // 2
    while shift >= 1:
        other = [pltpu.roll(v, shift, 0) for v in x]
        x = [jnp.maximum(x[i], other[PEER_TOPK - 1 - i]) for i in range(PEER_TOPK)]
        exchange(x, _MERGE16)
        shift //= 2
    return [v[0:1] for v in x]


def _prefix_count(pred, w):
    m8 = pred(w[7])
    m4 = pred(jnp.where(m8, w[11], w[3]))
    m2 = pred(jnp.where(m8, jnp.where(m4, w[13], w[9]), jnp.where(m4, w[5], w[1])))
    m1 = pred(jnp.where(m8, jnp.where(m4, jnp.where(m2, w[14], w[12]), jnp.where(m2, w[10], w[8])),
                        jnp.where(m4, jnp.where(m2, w[6], w[4]), jnp.where(m2, w[2], w[0]))))
    bit = lambda m, v: jnp.where(m, v, 0.0)
    return bit(m8, 8.0) + bit(m4, 4.0) + bit(m2, 2.0) + bit(m1, 1.0) + bit(pred(w[15]), 1.0)


def _top_desc(s, k, dst_ref, ts):
    cur = s
    for r in range(k):
        m = jnp.max(cur, axis=0, keepdims=True)
        dst_ref[r:r + 1, ts] = m
        if r + 1 < k:
            cur = jnp.where(cur == m, -jnp.inf, cur)


def _psel_body(hnt_ref, wq_ref, sk_ref, cnt_ref, e1_ref, r2_ref, e2_ref, s_ref, v1_ref, v2_ref, vc_ref):
    hnt = hnt_ref[...]
    for h in range(PEER_HEADS):
        q = jnp.dot(wq_ref[h * 256:(h + 1) * 256, :], hnt, preferred_element_type=F32)
        s_ref[0] = jnp.dot(sk_ref[h, 0], q[:128].astype(BF16), preferred_element_type=F32)
        s_ref[1] = jnp.dot(sk_ref[h, 1], q[128:].astype(BF16), preferred_element_type=F32)
        for tc in range(hnt.shape[1] // LANES):
            ts = slice(tc * LANES, (tc + 1) * LANES)
            s1 = s_ref[0, :, ts]
            s2 = s_ref[1, :, ts]
            w1 = _top16_rows(s1)
            w2 = _top16_rows(s2)
            for b in range(PEER_TOPK):
                v1_ref[b:b + 1, ts] = w1[b]
                v2_ref[b:b + 1, ts] = w2[b]
            v1 = v1_ref[:, ts]
            v2 = v2_ref[:, ts]
            cand = jnp.concatenate([v2 + v1[0:1]] + [v2[0:8] + v1[a:a + 1] for a in range(1, 8)]
                                   + [v1[8:16] + v2[0:1]], axis=0)
            _top_desc(cand, PEER_TOPK, vc_ref, ts)
            tau = vc_ref[PEER_TOPK - 1:PEER_TOPK, ts]
            top = vc_ref[0:1, ts]
            zsum = jnp.sum(jnp.where(cand >= tau, jnp.exp(cand - top), 0.0), axis=0, keepdims=True)
            cnt_ref[h, :, ts] = _prefix_count(lambda t: s1 + t >= tau, w2)
            e1_ref[h, :, ts] = jnp.exp(s1 - v1[0:1]) / zsum
            rank2 = _prefix_count(lambda t: t > s2, w2)
            r2_ref[h, :, ts] = pltpu.bitcast(rank2.astype(BF16), I32)
            e2_ref[h, :, ts] = pltpu.bitcast(jnp.exp(s2 - v2[0:1]).astype(BF16), I32)


def _psel(hnt, wq_t, sk, tn):
    n = hnt.shape[1]
    spec = pl.BlockSpec((PEER_HEADS, PEER_NKEYS, tn), lambda i: (0, 0, i))
    shp32 = jax.ShapeDtypeStruct((PEER_HEADS, PEER_NKEYS, n), F32)
    shp16 = jax.ShapeDtypeStruct((PEER_HEADS, PEER_NKEYS // 2, n), I32)
    spec16 = pl.BlockSpec((PEER_HEADS, PEER_NKEYS // 2, tn), lambda i: (0, 0, i))
    return pl.pallas_call(
        _psel_body,
        grid=(n // tn,),
        in_specs=[pl.BlockSpec((D_MODEL, tn), lambda i: (0, i)),
                  pl.BlockSpec((PEER_HEADS * 256, D_MODEL), lambda i: (0, 0)),
                  pl.BlockSpec((PEER_HEADS, 2, PEER_NKEYS, 128), lambda i: (0, 0, 0, 0))],
        out_specs=(spec, spec, spec16, spec16),
        out_shape=(shp32, shp32, shp16, shp16),
        scratch_shapes=[pltpu.VMEM((2, PEER_NKEYS, tn), F32)] + [pltpu.VMEM((PEER_TOPK, tn), F32)] * 3,
        compiler_params=_params(("arbitrary",)),
        name="psel",
    )(hnt, wq_t, sk)


def _pdense_body(*refs, eb, nchunk, nblk):
    hnt_ref, u_ref, vt_ref = refs[:3]
    cnt_refs = refs[3:3 + nchunk]
    e1_refs = refs[3 + nchunk:3 + 2 * nchunk]
    r2_ref, e2_ref, h1_ref, g_ref, out_ref, acc_ref, a_ref, coef_ref = refs[3 + 2 * nchunk:]
    s = pl.program_id(1)
    nsub = eb // PEER_NKEYS

    def row_bf16(ref, h, i1):
        return jnp.broadcast_to(ref[h, pl.ds(i1, 1), :], (PEER_NKEYS, LANES)).astype(BF16)

    @pl.when(s == 0)
    def _():
        acc_ref[...] = jnp.zeros(acc_ref.shape, F32)

    for part in range(eb // PDENSE_ROWS):
        rows = slice(part * PDENSE_ROWS, (part + 1) * PDENSE_ROWS)
        a_ref[rows, :] = jnp.dot(u_ref[rows, :], hnt_ref[...], preferred_element_type=F32)

    for c in range(nsub):
        rows = slice(c * PEER_NKEYS, (c + 1) * PEER_NKEYS)
        for tc in range(nchunk):
            ts = slice(tc * LANES, (tc + 1) * LANES)
            gate = None
            for h in range(PEER_HEADS):
                r2 = pltpu.bitcast(r2_ref[h, :, ts], BF16)
                e2 = pltpu.bitcast(e2_ref[h, :, ts], BF16)
                cnt = row_bf16(cnt_refs[tc], h, s * nsub + c)
                e1 = row_bf16(e1_refs[tc], h, s * nsub + c)
                term = jnp.where(r2 < cnt, e2, jnp.zeros((), BF16)) * e1
                gate = term if gate is None else gate + term
            at = a_ref[rows, ts]
            gelu = (0.5 * at) * (1.0 + lax.erf(at * (0.5 ** 0.5)))
            coef_ref[rows, ts] = gelu.astype(BF16) * gate

    acc_ref[...] += jnp.dot(vt_ref[...], coef_ref[...], preferred_element_type=F32)

    @pl.when(s == nblk - 1)
    def _():
        out_ref[...] = _rms(h1_ref[...] + acc_ref[...].T, g_ref[...])


def _pdense(hnt, u, vt, cnt, e1, r2, e2, h1, g, tn, eb):
    n = hnt.shape[1]
    nblk = u.shape[0] // eb
    assert nblk >= 2 and vt.shape == (nblk, D_MODEL, eb)
    nchunk = tn // LANES
    chunk = [pl.BlockSpec((PEER_HEADS, PEER_NKEYS, LANES), lambda i, s, c=c: (0, 0, i * nchunk + c))
             for c in range(nchunk)]
    sel16 = pl.BlockSpec((PEER_HEADS, PEER_NKEYS // 2, tn), lambda i, s: (0, 0, i))
    return pl.pallas_call(
        functools.partial(_pdense_body, eb=eb, nchunk=nchunk, nblk=nblk),
        grid=(n // tn, nblk),
        in_specs=[pl.BlockSpec((D_MODEL, tn), lambda i, s: (0, i)),
                  pl.BlockSpec((eb, D_MODEL), lambda i, s: (s, 0)),
                  pl.BlockSpec((None, D_MODEL, eb), lambda i, s: (s, 0, 0))]
                 + chunk + chunk
                 + [sel16, sel16,
                    pl.BlockSpec((tn, D_MODEL), lambda i, s: (i, 0)),
                    pl.BlockSpec((1, D_MODEL), lambda i, s: (0, 0))],
        out_specs=pl.BlockSpec((tn, D_MODEL), lambda i, s: (i, 0)),
        out_shape=jax.ShapeDtypeStruct((n, D_MODEL), F32),
        scratch_shapes=[pltpu.VMEM((D_MODEL, tn), F32),
                        pltpu.VMEM((eb, tn), F32), pltpu.VMEM((eb, tn), BF16)],
        compiler_params=_params(("arbitrary", "arbitrary")),
        name="pdense",
    )(hnt, u, vt, *([cnt] * nchunk), *([e1] * nchunk), r2, e2, h1, g)


def kernel(x, meta_tokens, norm_mix, w_in, w_uv, w_out, norm_ffn, peer_wq, peer_subkeys,
           peer_u, peer_v, norm_final):
    batch, seq, d = x.shape
    assert d == D_MODEL and norm_mix.shape[0] == 1 and seq % DSA_BLOCK == 0
    n = batch * seq
    topk = min(DSA_TOPK_MAX, seq // 4)
    tm = TOKEN_TILE if seq % TOKEN_TILE == 0 else DSA_BLOCK

    w = jnp.pad(w_in[0], ((0, 0), (0, IN_WIDTH_PAD - IN_WIDTH))).astype(BF16)
    g_mix = norm_mix[0][None, :]
    x2d = x.reshape(n, d)
    pos_f = jnp.arange(seq, dtype=F32) + float(N_META)
    pos_m = jnp.arange(N_META, dtype=F32)

    rq, rk, rv, rg, dq, dk, dv, iq, misc = _proj(x2d, g_mix, w, _rope_tables(pos_f), tm, seq // tm)
    _, mk, mv, _, _, mdk, mdv, _, mmisc = _proj(meta_tokens, g_mix, w, _rope_tables(pos_m), N_META, 1)

    ret = _retention(rq, rk, rv, rg, mk, mv, batch, seq)

    def keys(frames, meta):
        width = frames.shape[-1]
        lead = jnp.concatenate([jnp.zeros((LANES - N_META, width), frames.dtype), meta], axis=0)
        lead = jnp.broadcast_to(lead[None], (batch, LANES, width))
        return jnp.concatenate([lead, frames.reshape(batch, seq, width)], axis=1)

    kpad = keys(dk, mdk)
    vpad = keys(dv, mdv)
    kipad = keys(misc[:, :IDX_DIM].astype(BF16), mmisc[:, :IDX_DIM].astype(BF16))
    att = _dsa(dq, iq, misc, kpad, vpad, kipad, w_uv[0].astype(BF16), batch, seq, topk)

    h1, hnt = _post(ret, att, x2d, w_out[0].astype(BF16), norm_ffn[0][None, :], tm)

    wq_t = peer_wq[0].reshape(d, PEER_HEADS * 256).T.astype(BF16)
    cnt, e1, r2, e2 = _psel(hnt, wq_t, peer_subkeys[0].astype(BF16), tm)
    vt = peer_v[0].reshape(-1, PDENSE_EXPERTS, d).transpose(0, 2, 1).astype(BF16)
    out = _pdense(hnt, peer_u[0].astype(BF16), vt, cnt, e1, r2, e2, h1,
                  norm_final[None, :], tm, PDENSE_EXPERTS)
    return out.reshape(batch, seq, d)
```

```python
import functools

import jax
import jax.numpy as jnp
from jax import lax
from jax.experimental import pallas as pl
from jax.experimental.pallas import tpu as pltpu

F32 = jnp.float32
BF16 = jnp.bfloat16
I32 = jnp.int32

D_MODEL = 1024
N_META = 16
CHUNK = 64
RET_HEADS = 4
RET_DIM = 128
DSA_HEADS = 8
DSA_DIM = 128
DSA_OUT_DIM = 64
IDX_HEADS = 8
IDX_DIM = 64
DSA_TOPK_MAX = 256
ROPE_THETA = 10000.0
PEER_HEADS = 8
PEER_NKEYS = 128
PEER_TOPK = 16
EPS = 1e-6
NEG = -1e30
INT_MIN = -(2 ** 31)

IN_WIDTH = 3912
IN_WIDTH_PAD = 3968
LANES = 128
RET_BLOCK = 256
PDENSE_EXPERTS = 2048
PDENSE_ROWS = 512
DSA_BLOCK = 256
DSA_KEY_BITS = 31
DSA_BITS_PER_CHECK = 4
DSA_REFINE_STEPS = 40
SUBLANES = 8
TOKEN_TILE = 512
V7X_VMEM_BYTES = 64 * 1024 * 1024
VMEM_LIMIT = V7X_VMEM_BYTES * 7 // 8

NT = (((1,), (1,)), ((), ()))
TN = (((0,), (0,)), ((), ()))


def _rms(x, g):
    return x * lax.rsqrt(jnp.mean(x * x, axis=-1, keepdims=True) + EPS) * g


def _params(sem):
    return pltpu.CompilerParams(dimension_semantics=sem, vmem_limit_bytes=VMEM_LIMIT)


def _proj_body(x_ref, g_ref, w_ref, tab_ref, rq_ref, rk_ref, rv_ref, rg_ref, dq_ref, dk_ref,
               dv_ref, iq_ref, misc_ref):
    xn = _rms(x_ref[...], g_ref[...]).astype(BF16)
    cos_a, sin_a, cos_b, sin_b1, sin_b2, cos_m, sin_m1, sin_m2 = (tab_ref[i] for i in range(8))

    def mm(lo, n):
        return jnp.dot(xn, w_ref[:, lo:lo + n], preferred_element_type=F32)

    def rope_a(y):
        return y * cos_a + pltpu.roll(y, 64, 1) * sin_a

    def rope_b(y, c, s1, s2):
        return y * c + pltpu.roll(y, 96, 1) * s1 + pltpu.roll(y, 32, 1) * s2

    y = mm(0, 512)
    for h in range(RET_HEADS):
        rq_ref[:, h * 128:(h + 1) * 128] = rope_a(y[:, h * 128:(h + 1) * 128]).astype(BF16)
    y = mm(512, 512)
    for h in range(RET_HEADS):
        rk_ref[:, h * 128:(h + 1) * 128] = (
            rope_a(y[:, h * 128:(h + 1) * 128]) * (RET_DIM ** -0.5)).astype(BF16)
    rv_ref[...] = mm(1024, 512).astype(BF16)
    rg_ref[...] = mm(1536, 512)
    y = mm(2048, 1024)
    for h in range(DSA_HEADS):
        dq_ref[h] = (rope_a(y[:, h * 128:(h + 1) * 128]) * (DSA_DIM ** -0.5)).astype(BF16)
    dk_ref[...] = rope_a(mm(3072, 128)).astype(BF16)
    dv_ref[...] = mm(3200, 128).astype(BF16)
    y = mm(3328, 512)
    for p in range(IDX_HEADS // 2):
        o = (rope_b(y[:, p * 128:(p + 1) * 128], cos_b, sin_b1, sin_b2) * (IDX_DIM ** -0.5)).astype(BF16)
        iq_ref[2 * p] = o[:, :64]
        iq_ref[2 * p + 1] = o[:, 64:]
    misc_ref[...] = rope_b(mm(3840, 128), cos_m, sin_m1, sin_m2)


def _proj(x2d, g, w, tab, tm, tab_blocks):
    n = x2d.shape[0]
    row = lambda i: (i, 0)
    head = lambda i: (0, i, 0)
    out_shape = (
        jax.ShapeDtypeStruct((n, 512), BF16), jax.ShapeDtypeStruct((n, 512), BF16),
        jax.ShapeDtypeStruct((n, 512), BF16), jax.ShapeDtypeStruct((n, 512), F32),
        jax.ShapeDtypeStruct((DSA_HEADS, n, 128), BF16), jax.ShapeDtypeStruct((n, 128), BF16),
        jax.ShapeDtypeStruct((n, 128), BF16), jax.ShapeDtypeStruct((IDX_HEADS, n, 64), BF16),
        jax.ShapeDtypeStruct((n, 128), F32))
    out_specs = (
        pl.BlockSpec((tm, 512), row), pl.BlockSpec((tm, 512), row), pl.BlockSpec((tm, 512), row),
        pl.BlockSpec((tm, 512), row), pl.BlockSpec((DSA_HEADS, tm, 128), head),
        pl.BlockSpec((tm, 128), row), pl.BlockSpec((tm, 128), row),
        pl.BlockSpec((IDX_HEADS, tm, 64), head), pl.BlockSpec((tm, 128), row))
    return pl.pallas_call(
        _proj_body,
        grid=(n // tm,),
        in_specs=[pl.BlockSpec((tm, D_MODEL), row),
                  pl.BlockSpec((1, D_MODEL), lambda i: (0, 0)),
                  pl.BlockSpec((D_MODEL, IN_WIDTH_PAD), lambda i: (0, 0)),
                  pl.BlockSpec((8, tm, 128), lambda i: (0, i % tab_blocks, 0))],
        out_specs=out_specs,
        out_shape=out_shape,
        compiler_params=_params(("arbitrary",)),
        name="proj",
    )(x2d, g, w, tab)


def _rope_tables(pos):
    p = pos.shape[0]
    inv = ROPE_THETA ** (-jnp.arange(0, 64, dtype=F32) * 2.0 / 128)
    ang = pos[:, None] * inv[None, :]
    c, s = jnp.cos(ang), jnp.sin(ang)
    cos_a = jnp.concatenate([c, c], -1)
    sin_a = jnp.concatenate([-s, s], -1)
    inv = ROPE_THETA ** (-jnp.arange(0, 32, dtype=F32) * 2.0 / 64)
    ang = pos[:, None] * inv[None, :]
    c, s = jnp.cos(ang), jnp.sin(ang)
    z = jnp.zeros_like(s)
    z64 = jnp.zeros((p, 64), F32)
    cos_b = jnp.concatenate([c, c, c, c], -1)
    sin_b1 = jnp.concatenate([-s, z, -s, z], -1)
    sin_b2 = jnp.concatenate([z, s, z, s], -1)
    cos_m = jnp.concatenate([c, c, jnp.full((p, 64), IDX_HEADS ** -0.5, F32)], -1)
    sin_m1 = jnp.concatenate([-s, z, z64], -1)
    sin_m2 = jnp.concatenate([z, s, z64], -1)
    return jnp.stack([cos_a, sin_a, cos_b, sin_b1, sin_b2, cos_m, sin_m1, sin_m2])


def _ret_body(rq_ref, rk_ref, rv_ref, rg_ref, mk_ref, mv_ref, dmat_ref, xi_ref, zeta_ref,
              mzeta_ref, gsc_ref, out_ref, r_ref):
    @pl.when(pl.program_id(1) == 0)
    def _():
        for h in range(RET_HEADS):
            hs = slice(h * 128, (h + 1) * 128)
            kz = (mk_ref[:, hs].astype(F32) * mzeta_ref[h]).astype(BF16)
            r_ref[h] = lax.dot_general(kz, mv_ref[:, hs], TN, preferred_element_type=F32)

    for h in range(RET_HEADS):
        hs = slice(h * 128, (h + 1) * 128)
        q = rq_ref[:, hs]
        k = rk_ref[:, hs]
        v = rv_ref[:, hs]
        s = lax.dot_general(q, k, NT, preferred_element_type=F32) * dmat_ref[h]
        o = jnp.dot(s.astype(BF16), v, preferred_element_type=F32)
        r = r_ref[h]
        qx = (q.astype(F32) * xi_ref[h]).astype(BF16)
        o = o + jnp.dot(qx, r.astype(BF16), preferred_element_type=F32)
        kz = (k.astype(F32) * zeta_ref[h]).astype(BF16)
        u = lax.dot_general(kz, v, TN, preferred_element_type=F32)
        r_ref[h] = r * gsc_ref[h] + u
        mu = jnp.mean(o, axis=-1, keepdims=True)
        d = o - mu
        var = jnp.mean(d * d, axis=-1, keepdims=True)
        on = d * lax.rsqrt(var + EPS)
        g = rg_ref[:, hs]
        out_ref[:, hs] = (g * jax.nn.sigmoid(g) * on).astype(BF16)


def _retention(rq, rk, rv, rg, mk, mv, batch, seq):
    n = rq.shape[0]
    nb = seq // RET_BLOCK
    lg = jnp.log(1.0 - 2.0 ** (-5.0 - jnp.arange(RET_HEADS, dtype=F32)))
    i = jnp.arange(RET_BLOCK, dtype=F32)
    ci = jnp.arange(RET_BLOCK) // CHUNK
    vis = (ci[None, :] <= ci[:, None])
    dmat = jnp.where(vis[None], jnp.exp(lg[:, None, None] * jnp.abs(i[:, None] - i[None, :])), 0.0)
    xi = jnp.broadcast_to(jnp.exp(lg[:, None] * (i + 1.0)[None, :])[:, :, None], (RET_HEADS, RET_BLOCK, 128))
    zeta = jnp.broadcast_to(jnp.exp(lg[:, None] * (RET_BLOCK - 1.0 - i)[None, :])[:, :, None],
                            (RET_HEADS, RET_BLOCK, 128))
    im = jnp.arange(N_META, dtype=F32)
    mzeta = jnp.broadcast_to(jnp.exp(lg[:, None] * (N_META - 1.0 - im)[None, :])[:, :, None],
                             (RET_HEADS, N_META, 128))
    gsc = jnp.broadcast_to(jnp.exp(lg * RET_BLOCK)[:, None, None], (RET_HEADS, 1, 128))
    row = lambda b, s: (b * nb + s, 0)
    c2 = lambda b, s: (0, 0)
    c3 = lambda b, s: (0, 0, 0)
    return pl.pallas_call(
        _ret_body,
        grid=(batch, nb),
        in_specs=[pl.BlockSpec((RET_BLOCK, 512), row), pl.BlockSpec((RET_BLOCK, 512), row),
                  pl.BlockSpec((RET_BLOCK, 512), row), pl.BlockSpec((RET_BLOCK, 512), row),
                  pl.BlockSpec((N_META, 512), c2), pl.BlockSpec((N_META, 512), c2),
                  pl.BlockSpec((RET_HEADS, RET_BLOCK, RET_BLOCK), c3),
                  pl.BlockSpec((RET_HEADS, RET_BLOCK, 128), c3),
                  pl.BlockSpec((RET_HEADS, RET_BLOCK, 128), c3),
                  pl.BlockSpec((RET_HEADS, N_META, 128), c3),
                  pl.BlockSpec((RET_HEADS, 1, 128), c3)],
        out_specs=pl.BlockSpec((RET_BLOCK, 512), row),
        out_shape=jax.ShapeDtypeStruct((n, 512), BF16),
        scratch_shapes=[pltpu.VMEM((RET_HEADS, 128, 128), F32)],
        compiler_params=_params(("arbitrary", "arbitrary")),
        name="retention",
    )(rq, rk, rv, rg, mk, mv, dmat, xi, zeta, mzeta, gsc)


def _dsa_body(qa_ref, qi_ref, misc_ref, k_ref, v_ref, ki_ref, wuv_ref, out_ref,
              isc_ref, bias_ref, m_ref, l_ref, acc_ref, *, topk):
    qb = pl.program_id(1)
    tq = DSA_BLOCK
    nh = DSA_HEADS

    wt = misc_ref[...].T
    qi = qi_ref[...].reshape(IDX_HEADS * tq, IDX_DIM)

    def isc_tile(koff, width, allowed):
        kt = ki_ref[pl.ds(koff, width), :]
        z = lax.dot_general(kt, qi, NT, preferred_element_type=F32)
        isc = None
        for h in range(IDX_HEADS):
            term = jnp.maximum(z[:, h * tq:(h + 1) * tq], 0.0) * wt[64 + h:65 + h, :]
            isc = term if isc is None else isc + term
        if allowed is not None:
            isc = jnp.where(allowed, isc, -jnp.inf)
        isc_ref[pl.ds(koff, width), :] = isc

    isc_tile(0, LANES, lax.broadcasted_iota(I32, (LANES, tq), 0) >= LANES - N_META)

    def body_a(t, c):
        isc_tile(pl.multiple_of(LANES + t * tq, LANES), tq, None)
        return c

    lax.fori_loop(0, qb, body_a, 0)
    keyc = lax.broadcasted_iota(I32, (tq, tq), 0) // CHUNK
    qryc = lax.broadcasted_iota(I32, (tq, tq), 1) // CHUNK
    diag_off = pl.multiple_of(LANES + qb * tq, LANES)
    isc_tile(diag_off, tq, keyc <= qryc)

    kf = float(topk)
    sub = SUBLANES

    def key_to_f32(key):
        return lax.bitcast_convert_type(jnp.where(key < 0, key ^ jnp.int32(0x7FFFFFFF), key), F32)

    def count(pred):
        def slab(off, a):
            hit = jnp.where(pred(isc_ref[pl.ds(off, LANES), :]), 1.0, 0.0)
            return a + jnp.sum(hit.reshape(LANES // sub, sub, tq), axis=0)

        def tile(off, a):
            return slab(pl.multiple_of(off + LANES, LANES), slab(off, a))

        def body(t, a):
            off = pl.multiple_of(LANES + t * (2 * tq), LANES)
            return tile(pl.multiple_of(off + tq, LANES), tile(off, a))

        a = lax.fori_loop(0, (qb + 1) // 2, body, slab(0, jnp.zeros((sub, tq), F32)))
        a = lax.cond(qb % 2 == 0, lambda a: tile(diag_off, a), lambda a: a, a)
        return jnp.sum(a, axis=0, keepdims=True)

    def count_ge(cand):
        return count(lambda x: x >= cand)

    lowest = jnp.full((1, tq), -3.0e38, F32)
    n_real = count_ge(lowest)
    c_pos = count_ge(jnp.zeros((1, tq), F32))
    pos = c_pos >= kf
    base0 = jnp.where(pos, jnp.int32(0), jnp.int32(INT_MIN))
    cnt0 = jnp.where(n_real <= kf, kf, jnp.where(pos, c_pos, n_real))

    def unfinished(cnt):
        return jnp.max(jnp.where(cnt != kf, 1.0, 0.0))

    def try_bit(bit, base, cnt):
        cand = base | lax.shift_left(jnp.int32(1), bit)
        c = count_ge(key_to_f32(cand))
        ok = c >= kf
        return jnp.where(ok, cand, base), jnp.where(ok, c, cnt)

    base, cnt = base0, cnt0
    groups = DSA_KEY_BITS // DSA_BITS_PER_CHECK
    for bit in range(DSA_KEY_BITS - 1, DSA_BITS_PER_CHECK * groups - 1, -1):
        base, cnt = try_bit(bit, base, cnt)

    zero_tie = jnp.logical_and(pos, count(lambda x: x > 0.0) < kf)

    def group_cond(state):
        g, _, cnt = state
        return jnp.logical_and(g >= 0, unfinished(jnp.where(zero_tie, kf, cnt)) > 0.0)

    def group_body(state):
        g, base, cnt = state
        for k in range(DSA_BITS_PER_CHECK - 1, -1, -1):
            base, cnt = try_bit(g * DSA_BITS_PER_CHECK + k, base, cnt)
        return g - 1, base, cnt

    _, base, cnt = lax.while_loop(group_cond, group_body, (jnp.int32(groups - 1), base, cnt))

    def refine_cond(state):
        it, _, _, _, go = state
        return jnp.logical_and(it < DSA_REFINE_STEPS, go > 0.0)

    def refine_body(state):
        it, lo, hi, cnt, _ = state
        mid = lo + 0.5 * (hi - lo)
        c = count_ge(mid)
        ok = c >= kf
        cnt = jnp.where(ok, c, cnt)
        moving = jnp.logical_and(jnp.logical_and(mid > lo, mid < hi),
                                 jnp.logical_and(cnt != kf, jnp.logical_not(zero_tie)))
        return it + 1, jnp.where(ok, mid, lo), jnp.where(ok, hi, mid), cnt, jnp.max(jnp.where(moving, 1.0, 0.0))

    _, thr, _, cnt, _ = lax.while_loop(
        refine_cond, refine_body,
        (jnp.int32(0), key_to_f32(base), key_to_f32(base + 1), cnt, unfinished(jnp.where(zero_tie, kf, cnt))))
    thr = jnp.where(n_real <= kf, lowest, thr)

    need = kf - count(lambda x: x > thr)
    lower = (lax.broadcasted_iota(I32, (tq, tq), 1) <= lax.broadcasted_iota(I32, (tq, tq), 0)).astype(BF16)

    def mask_tile(off, width, ties_before):
        x = isc_ref[pl.ds(off, width), :]
        tie = x == thr
        tie_f = jnp.where(tie, 1.0, 0.0)
        upto = jnp.dot(lower[:width, :width], tie_f.astype(BF16), preferred_element_type=F32)
        keep = jnp.logical_or(x > thr, jnp.logical_and(tie, ties_before + upto - tie_f < need))
        bias_ref[:, pl.ds(off, width)] = jnp.where(keep, 0.0, NEG).T
        return ties_before + upto[width - 1:width, :]

    def body_m(t, ties_before):
        return mask_tile(pl.multiple_of(LANES + t * tq, LANES), tq, ties_before)

    lax.fori_loop(0, qb + 1, body_m, mask_tile(0, LANES, jnp.zeros((1, tq), F32)))

    m_ref[...] = jnp.full(m_ref.shape, NEG, F32)
    l_ref[...] = jnp.zeros(l_ref.shape, F32)
    acc_ref[...] = jnp.zeros(acc_ref.shape, F32)
    qa = qa_ref[...].reshape(nh * tq, DSA_DIM)

    def att_tile(koff, width):
        kt = k_ref[pl.ds(koff, width), :]
        vt = v_ref[pl.ds(koff, width), :]
        s = lax.dot_general(qa, kt, NT, preferred_element_type=F32).reshape(nh, tq, width)
        reps = width // LANES
        s = (s + bias_ref[:, pl.ds(koff, width)][None]).reshape(nh * tq, width)
        m_prev = m_ref[...]
        m_new = jnp.maximum(m_prev, jnp.max(s, axis=1, keepdims=True))
        alpha = jnp.exp(m_prev - m_new)
        m_w = m_new if reps == 1 else jnp.concatenate([m_new] * reps, axis=1)
        p = jnp.exp(s - m_w)
        l_ref[...] = alpha * l_ref[...] + jnp.sum(p, axis=1, keepdims=True)
        acc_ref[...] = alpha * acc_ref[...] + jnp.dot(p.astype(BF16), vt, preferred_element_type=F32)
        m_ref[...] = m_new

    att_tile(0, LANES)

    def body_c(t, c):
        att_tile(pl.multiple_of(LANES + t * (2 * tq), LANES), 2 * tq)
        return c

    lax.fori_loop(0, (qb + 1) // 2, body_c, 0)

    @pl.when(qb % 2 == 0)
    def _():
        att_tile(diag_off, tq)

    o = (acc_ref[...] / l_ref[...]).astype(BF16).reshape(nh, tq, DSA_DIM)
    for h in range(nh):
        out_ref[:, h * DSA_OUT_DIM:(h + 1) * DSA_OUT_DIM] = jnp.dot(
            o[h], wuv_ref[h], preferred_element_type=F32).astype(BF16)


def _dsa(dq, iq, misc, kpad, vpad, kipad, wuv, batch, seq, topk):
    n = misc.shape[0]
    nq = seq // DSA_BLOCK
    tp = kpad.shape[1]
    head = lambda b, q: (0, b * nq + q, 0)
    row = lambda b, q: (b * nq + q, 0)
    per_b = lambda b, q: (b, 0, 0)
    return pl.pallas_call(
        functools.partial(_dsa_body, topk=topk),
        grid=(batch, nq),
        in_specs=[pl.BlockSpec((DSA_HEADS, DSA_BLOCK, DSA_DIM), head),
                  pl.BlockSpec((IDX_HEADS, DSA_BLOCK, IDX_DIM), head),
                  pl.BlockSpec((DSA_BLOCK, 128), row),
                  pl.BlockSpec((None, tp, DSA_DIM), per_b),
                  pl.BlockSpec((None, tp, DSA_DIM), per_b),
                  pl.BlockSpec((None, tp, IDX_DIM), per_b),
                  pl.BlockSpec((DSA_HEADS, DSA_DIM, DSA_OUT_DIM), lambda b, q: (0, 0, 0))],
        out_specs=pl.BlockSpec((DSA_BLOCK, 512), row),
        out_shape=jax.ShapeDtypeStruct((n, 512), BF16),
        scratch_shapes=[pltpu.VMEM((tp, DSA_BLOCK), F32),
                        pltpu.VMEM((DSA_BLOCK, tp), F32),
                        pltpu.VMEM((DSA_HEADS * DSA_BLOCK, LANES), F32),
                        pltpu.VMEM((DSA_HEADS * DSA_BLOCK, LANES), F32),
                        pltpu.VMEM((DSA_HEADS * DSA_BLOCK, DSA_DIM), F32)],
        compiler_params=_params(("arbitrary", "arbitrary")),
        name="dsa",
    )(dq, iq, misc, kpad, vpad, kipad, wuv)


def _post_body(ret_ref, att_ref, x_ref, w_ref, g_ref, h1_ref, hnt_ref):
    y = jnp.dot(ret_ref[...], w_ref[0:512, :], preferred_element_type=F32)
    y = y + jnp.dot(att_ref[...], w_ref[512:1024, :], preferred_element_type=F32)
    h1 = x_ref[...] + y
    h1_ref[...] = h1
    hnt_ref[...] = _rms(h1, g_ref[...]).T.astype(BF16)


def _post(ret, att, x2d, w_out, g, tm):
    n = x2d.shape[0]
    row = lambda i: (i, 0)
    return pl.pallas_call(
        _post_body,
        grid=(n // tm,),
        in_specs=[pl.BlockSpec((tm, 512), row), pl.BlockSpec((tm, 512), row),
                  pl.BlockSpec((tm, D_MODEL), row),
                  pl.BlockSpec((D_MODEL, D_MODEL), lambda i: (0, 0)),
                  pl.BlockSpec((1, D_MODEL), lambda i: (0, 0))],
        out_specs=(pl.BlockSpec((tm, D_MODEL), row), pl.BlockSpec((D_MODEL, tm), lambda i: (0, i))),
        out_shape=(jax.ShapeDtypeStruct((n, D_MODEL), F32), jax.ShapeDtypeStruct((D_MODEL, n), BF16)),
        compiler_params=_params(("arbitrary",)),
        name="post",
    )(ret, att, x2d, w_out, g)


def _oddeven_merge_sort_pairs(n):
    pairs = []
    p = 1
    while p < n:
        k = p
        while k >= 1:
            for j in range(k % p, n - k, 2 * k):
                for i in range(min(k, n - j - k)):
                    if (i + j) // (2 * p) == (i + j + k) // (2 * p):
                        pairs.append((i + j, i + j + k))
            k //= 2
        p *= 2
    return tuple(pairs)


def _bitonic_merge_pairs(n):
    pairs = []
    d = n // 2
    while d >= 1:
        pairs += [(i, i + d) for i in range(n) if (i & d) == 0]
        d //= 2
    return tuple(pairs)


_SORT16 = _oddeven_merge_sort_pairs(PEER_TOPK)
_MERGE16 = _bitonic_merge_pairs(PEER_TOPK)


def _top16_rows(s):
    def exchange(x, pairs):
        for i, j in pairs:
            x[i], x[j] = jnp.maximum(x[i], x[j]), jnp.minimum(x[i], x[j])

    x = [s[SUBLANES * j:SUBLANES * (j + 1), :] for j in range(PEER_TOPK)]
    exchange(x, _SORT16)
    shift = SUBLANES // 2
    while shift >= 1:
        other = [pltpu.roll(v, shift, 0) for v in x]
        x = [jnp.maximum(x[i], other[PEER_TOPK - 1 - i]) for i in range(PEER_TOPK)]
        exchange(x, _MERGE16)
        shift //= 2
    return [v[0:1] for v in x]


def _prefix_count(pred, w):
    m8 = pred(w[7])
    m4 = pred(jnp.where(m8, w[11], w[3]))
    m2 = pred(jnp.where(m8, jnp.where(m4, w[13], w[9]), jnp.where(m4, w[5], w[1])))
    m1 = pred(jnp.where(m8, jnp.where(m4, jnp.where(m2, w[14], w[12]), jnp.where(m2, w[10], w[8])),
                        jnp.where(m4, jnp.where(m2, w[6], w[4]), jnp.where(m2, w[2], w[0]))))
    bit = lambda m, v: jnp.where(m, v, 0.0)
    return bit(m8, 8.0) + bit(m4, 4.0) + bit(m2, 2.0) + bit(m1, 1.0) + bit(pred(w[15]), 1.0)


def _top_desc(s, k, dst_ref, ts):
    cur = s
    for r in range(k):
        m = jnp.max(cur, axis=0, keepdims=True)
        dst_ref[r:r + 1, ts] = m
        if r + 1 < k:
            cur = jnp.where(cur == m, -jnp.inf, cur)


def _psel_body(hnt_ref, wq_ref, sk_ref, cnt_ref, e1_ref, r2_ref, e2_ref, s_ref, v1_ref, v2_ref, vc_ref):
    hnt = hnt_ref[...]
    for h in range(PEER_HEADS):
        q = jnp.dot(wq_ref[h * 256:(h + 1) * 256, :], hnt, preferred_element_type=F32)
        s_ref[0] = jnp.dot(sk_ref[h, 0], q[:128].astype(BF16), preferred_element_type=F32)
        s_ref[1] = jnp.dot(sk_ref[h, 1], q[128:].astype(BF16), preferred_element_type=F32)
        for tc in range(hnt.shape[1] // LANES):
            ts = slice(tc * LANES, (tc + 1) * LANES)
            s1 = s_ref[0, :, ts]
            s2 = s_ref[1, :, ts]
            w1 = _top16_rows(s1)
            w2 = _top16_rows(s2)
            for b in range(PEER_TOPK):
                v1_ref[b:b + 1, ts] = w1[b]
                v2_ref[b:b + 1, ts] = w2[b]
            v1 = v1_ref[:, ts]
            v2 = v2_ref[:, ts]
            cand = jnp.concatenate([v2 + v1[0:1]] + [v2[0:8] + v1[a:a + 1] for a in range(1, 8)]
                                   + [v1[8:16] + v2[0:1]], axis=0)
            _top_desc(cand, PEER_TOPK, vc_ref, ts)
            tau = vc_ref[PEER_TOPK - 1:PEER_TOPK, ts]
            top = vc_ref[0:1, ts]
            zsum = jnp.sum(jnp.where(cand >= tau, jnp.exp(cand - top), 0.0), axis=0, keepdims=True)
            cnt_ref[h, :, ts] = _prefix_count(lambda t: s1 + t >= tau, w2)
            e1_ref[h, :, ts] = jnp.exp(s1 - v1[0:1]) / zsum
            rank2 = _prefix_count(lambda t: t > s2, w2)
            r2_ref[h, :, ts] = pltpu.bitcast(rank2.astype(BF16), I32)
            e2_ref[h, :, ts] = pltpu.bitcast(jnp.exp(s2 - v2[0:1]).astype(BF16), I32)


def _psel(hnt, wq_t, sk, tn):
    n = hnt.shape[1]
    spec = pl.BlockSpec((PEER_HEADS, PEER_NKEYS, tn), lambda i: (0, 0, i))
    shp32 = jax.ShapeDtypeStruct((PEER_HEADS, PEER_NKEYS, n), F32)
    shp16 = jax.ShapeDtypeStruct((PEER_HEADS, PEER_NKEYS // 2, n), I32)
    spec16 = pl.BlockSpec((PEER_HEADS, PEER_NKEYS // 2, tn), lambda i: (0, 0, i))
    return pl.pallas_call(
        _psel_body,
        grid=(n // tn,),
        in_specs=[pl.BlockSpec((D_MODEL, tn), lambda i: (0, i)),
                  pl.BlockSpec((PEER_HEADS * 256, D_MODEL), lambda i: (0, 0)),
                  pl.BlockSpec((PEER_HEADS, 2, PEER_NKEYS, 128), lambda i: (0, 0, 0, 0))],
        out_specs=(spec, spec, spec16, spec16),
        out_shape=(shp32, shp32, shp16, shp16),
        scratch_shapes=[pltpu.VMEM((2, PEER_NKEYS, tn), F32)] + [pltpu.VMEM((PEER_TOPK, tn), F32)] * 3,
        compiler_params=_params(("arbitrary",)),
        name="psel",
    )(hnt, wq_t, sk)


def _pdense_body(*refs, eb, nchunk, nblk):
    hnt_ref, u_ref, vt_ref = refs[:3]
    cnt_refs = refs[3:3 + nchunk]
    e1_refs = refs[3 + nchunk:3 + 2 * nchunk]
    r2_ref, e2_ref, h1_ref, g_ref, out_ref, acc_ref, a_ref, coef_ref = refs[3 + 2 * nchunk:]
    s = pl.program_id(1)
    nsub = eb // PEER_NKEYS

    def row_bf16(ref, h, i1):
        return jnp.broadcast_to(ref[h, pl.ds(i1, 1), :], (PEER_NKEYS, LANES)).astype(BF16)

    @pl.when(s == 0)
    def _():
        acc_ref[...] = jnp.zeros(acc_ref.shape, F32)

    for part in range(eb // PDENSE_ROWS):
        rows = slice(part * PDENSE_ROWS, (part + 1) * PDENSE_ROWS)
        a_ref[rows, :] = jnp.dot(u_ref[rows, :], hnt_ref[...], preferred_element_type=F32)

    for c in range(nsub):
        rows = slice(c * PEER_NKEYS, (c + 1) * PEER_NKEYS)
        for tc in range(nchunk):
            ts = slice(tc * LANES, (tc + 1) * LANES)
            gate = None
            for h in range(PEER_HEADS):
                r2 = pltpu.bitcast(r2_ref[h, :, ts], BF16)
                e2 = pltpu.bitcast(e2_ref[h, :, ts], BF16)
                cnt = row_bf16(cnt_refs[tc], h, s * nsub + c)
                e1 = row_bf16(e1_refs[tc], h, s * nsub + c)
                term = jnp.where(r2 < cnt, e2, jnp.zeros((), BF16)) * e1
                gate = term if gate is None else gate + term
            at = a_ref[rows, ts]
            gelu = (0.5 * at) * (1.0 + lax.erf(at * (0.5 ** 0.5)))
            coef_ref[rows, ts] = gelu.astype(BF16) * gate

    acc_ref[...] += jnp.dot(vt_ref[...], coef_ref[...], preferred_element_type=F32)

    @pl.when(s == nblk - 1)
    def _():
        out_ref[...] = _rms(h1_ref[...] + acc_ref[...].T, g_ref[...])


def _pdense(hnt, u, vt, cnt, e1, r2, e2, h1, g, tn, eb):
    n = hnt.shape[1]
    nblk = u.shape[0] // eb
    assert nblk >= 2 and vt.shape == (nblk, D_MODEL, eb)
    nchunk = tn // LANES
    chunk = [pl.BlockSpec((PEER_HEADS, PEER_NKEYS, LANES), lambda i, s, c=c: (0, 0, i * nchunk + c))
             for c in range(nchunk)]
    sel16 = pl.BlockSpec((PEER_HEADS, PEER_NKEYS // 2, tn), lambda i, s: (0, 0, i))
    return pl.pallas_call(
        functools.partial(_pdense_body, eb=eb, nchunk=nchunk, nblk=nblk),
        grid=(n // tn, nblk),
        in_specs=[pl.BlockSpec((D_MODEL, tn), lambda i, s: (0, i)),
                  pl.BlockSpec((eb, D_MODEL), lambda i, s: (s, 0)),
                  pl.BlockSpec((None, D_MODEL, eb), lambda i, s: (s, 0, 0))]
                 + chunk + chunk
                 + [sel16, sel16,
                    pl.BlockSpec((tn, D_MODEL), lambda i, s: (i, 0)),
                    pl.BlockSpec((1, D_MODEL), lambda i, s: (0, 0))],
        out_specs=pl.BlockSpec((tn, D_MODEL), lambda i, s: (i, 0)),
        out_shape=jax.ShapeDtypeStruct((n, D_MODEL), F32),
        scratch_shapes=[pltpu.VMEM((D_MODEL, tn), F32),
                        pltpu.VMEM((eb, tn), F32), pltpu.VMEM((eb, tn), BF16)],
        compiler_params=_params(("arbitrary", "arbitrary")),
        name="pdense",
    )(hnt, u, vt, *([cnt] * nchunk), *([e1] * nchunk), r2, e2, h1, g)


def kernel(x, meta_tokens, norm_mix, w_in, w_uv, w_out, norm_ffn, peer_wq, peer_subkeys,
           peer_u, peer_v, norm_final):
    batch, seq, d = x.shape
    assert d == D_MODEL and norm_mix.shape[0] == 1 and seq % DSA_BLOCK == 0
    n = batch * seq
    topk = min(DSA_TOPK_MAX, seq // 4)
    tm = TOKEN_TILE if seq % TOKEN_TILE == 0 else DSA_BLOCK

    w = jnp.pad(w_in[0], ((0, 0), (0, IN_WIDTH_PAD - IN_WIDTH))).astype(BF16)
    g_mix = norm_mix[0][None, :]
    x2d = x.reshape(n, d)
    pos_f = jnp.arange(seq, dtype=F32) + float(N_META)
    pos_m = jnp.arange(N_META, dtype=F32)

    rq, rk, rv, rg, dq, dk, dv, iq, misc = _proj(x2d, g_mix, w, _rope_tables(pos_f), tm, seq // tm)
    _, mk, mv, _, _, mdk, mdv, _, mmisc = _proj(meta_tokens, g_mix, w, _rope_tables(pos_m), N_META, 1)

    ret = _retention(rq, rk, rv, rg, mk, mv, batch, seq)

    def keys(frames, meta):
        width = frames.shape[-1]
        lead = jnp.concatenate([jnp.zeros((LANES - N_META, width), frames.dtype), meta], axis=0)
        lead = jnp.broadcast_to(lead[None], (batch, LANES, width))
        return jnp.concatenate([lead, frames.reshape(batch, seq, width)], axis=1)

    kpad = keys(dk, mdk)
    vpad = keys(dv, mdv)
    kipad = keys(misc[:, :IDX_DIM].astype(BF16), mmisc[:, :IDX_DIM].astype(BF16))
    att = _dsa(dq, iq, misc, kpad, vpad, kipad, w_uv[0].astype(BF16), batch, seq, topk)

    h1, hnt = _post(ret, att, x2d, w_out[0].astype(BF16), norm_ffn[0][None, :], tm)

    wq_t = peer_wq[0].reshape(d, PEER_HEADS * 256).T.astype(BF16)
    cnt, e1, r2, e2 = _psel(hnt, wq_t, peer_subkeys[0].astype(BF16), tm)
    vt = peer_v[0].reshape(-1, PDENSE_EXPERTS, d).transpose(0, 2, 1).astype(BF16)
    out = _pdense(hnt, peer_u[0].astype(BF16), vt, cnt, e1, r2, e2, h1,
                  norm_final[None, :], tm, PDENSE_EXPERTS)
    return out.reshape(batch, seq, d)
```

```python
import functools

import jax
import jax.numpy as jnp
from jax import lax
from jax.experimental import pallas as pl
from jax.experimental.pallas import tpu as pltpu

F32 = jnp.float32
BF16 = jnp.bfloat16
I32 = jnp.int32

D_MODEL = 1024
N_META = 16
CHUNK = 64
RET_HEADS = 4
RET_DIM = 128
DSA_HEADS = 8
DSA_DIM = 128
DSA_OUT_DIM = 64
IDX_HEADS = 8
IDX_DIM = 64
DSA_TOPK_MAX = 256
ROPE_THETA = 10000.0
PEER_HEADS = 8
PEER_NKEYS = 128
PEER_TOPK = 16
EPS = 1e-6
NEG = -1e30
INT_MIN = -(2 ** 31)

IN_WIDTH = 3912
IN_WIDTH_PAD = 3968
LANES = 128
RET_BLOCK = 256
PDENSE_EXPERTS = 4096
PDENSE_ROWS = 512
DSA_BLOCK = 256
DSA_KEY_BITS = 31
DSA_BITS_PER_CHECK = 4
DSA_REFINE_STEPS = 40
SUBLANES = 8
TOKEN_TILE = 512
V7X_VMEM_BYTES = 64 * 1024 * 1024
VMEM_LIMIT = V7X_VMEM_BYTES * 7 // 8

NT = (((1,), (1,)), ((), ()))
TN = (((0,), (0,)), ((), ()))


def _rms(x, g):
    return x * lax.rsqrt(jnp.mean(x * x, axis=-1, keepdims=True) + EPS) * g


def _params(sem):
    return pltpu.CompilerParams(dimension_semantics=sem, vmem_limit_bytes=VMEM_LIMIT)


def _proj_body(x_ref, g_ref, w_ref, tab_ref, rq_ref, rk_ref, rv_ref, rg_ref, dq_ref, dk_ref,
               dv_ref, iq_ref, misc_ref):
    xn = _rms(x_ref[...], g_ref[...]).astype(BF16)
    cos_a, sin_a, cos_b, sin_b1, sin_b2, cos_m, sin_m1, sin_m2 = (tab_ref[i] for i in range(8))

    def mm(lo, n):
        return jnp.dot(xn, w_ref[:, lo:lo + n], preferred_element_type=F32)

    def rope_a(y):
        return y * cos_a + pltpu.roll(y, 64, 1) * sin_a

    def rope_b(y, c, s1, s2):
        return y * c + pltpu.roll(y, 96, 1) * s1 + pltpu.roll(y, 32, 1) * s2

    y = mm(0, 512)
    for h in range(RET_HEADS):
        rq_ref[:, h * 128:(h + 1) * 128] = rope_a(y[:, h * 128:(h + 1) * 128]).astype(BF16)
    y = mm(512, 512)
    for h in range(RET_HEADS):
        rk_ref[:, h * 128:(h + 1) * 128] = (
            rope_a(y[:, h * 128:(h + 1) * 128]) * (RET_DIM ** -0.5)).astype(BF16)
    rv_ref[...] = mm(1024, 512).astype(BF16)
    rg_ref[...] = mm(1536, 512)
    y = mm(2048, 1024)
    for h in range(DSA_HEADS):
        dq_ref[h] = (rope_a(y[:, h * 128:(h + 1) * 128]) * (DSA_DIM ** -0.5)).astype(BF16)
    dk_ref[...] = rope_a(mm(3072, 128)).astype(BF16)
    dv_ref[...] = mm(3200, 128).astype(BF16)
    y = mm(3328, 512)
    for p in range(IDX_HEADS // 2):
        o = (rope_b(y[:, p * 128:(p + 1) * 128], cos_b, sin_b1, sin_b2) * (IDX_DIM ** -0.5)).astype(BF16)
        iq_ref[2 * p] = o[:, :64]
        iq_ref[2 * p + 1] = o[:, 64:]
    misc_ref[...] = rope_b(mm(3840, 128), cos_m, sin_m1, sin_m2)


def _proj(x2d, g, w, tab, tm, tab_blocks):
    n = x2d.shape[0]
    row = lambda i: (i, 0)
    head = lambda i: (0, i, 0)
    out_shape = (
        jax.ShapeDtypeStruct((n, 512), BF16), jax.ShapeDtypeStruct((n, 512), BF16),
        jax.ShapeDtypeStruct((n, 512), BF16), jax.ShapeDtypeStruct((n, 512), F32),
        jax.ShapeDtypeStruct((DSA_HEADS, n, 128), BF16), jax.ShapeDtypeStruct((n, 128), BF16),
        jax.ShapeDtypeStruct((n, 128), BF16), jax.ShapeDtypeStruct((IDX_HEADS, n, 64), BF16),
        jax.ShapeDtypeStruct((n, 128), F32))
    out_specs = (
        pl.BlockSpec((tm, 512), row), pl.BlockSpec((tm, 512), row), pl.BlockSpec((tm, 512), row),
        pl.BlockSpec((tm, 512), row), pl.BlockSpec((DSA_HEADS, tm, 128), head),
        pl.BlockSpec((tm, 128), row), pl.BlockSpec((tm, 128), row),
        pl.BlockSpec((IDX_HEADS, tm, 64), head), pl.BlockSpec((tm, 128), row))
    return pl.pallas_call(
        _proj_body,
        grid=(n // tm,),
        in_specs=[pl.BlockSpec((tm, D_MODEL), row),
                  pl.BlockSpec((1, D_MODEL), lambda i: (0, 0)),
                  pl.BlockSpec((D_MODEL, IN_WIDTH_PAD), lambda i: (0, 0)),
                  pl.BlockSpec((8, tm, 128), lambda i: (0, i % tab_blocks, 0))],
        out_specs=out_specs,
        out_shape=out_shape,
        compiler_params=_params(("arbitrary",)),
        name="proj",
    )(x2d, g, w, tab)


def _rope_tables(pos):
    p = pos.shape[0]
    inv = ROPE_THETA ** (-jnp.arange(0, 64, dtype=F32) * 2.0 / 128)
    ang = pos[:, None] * inv[None, :]
    c, s = jnp.cos(ang), jnp.sin(ang)
    cos_a = jnp.concatenate([c, c], -1)
    sin_a = jnp.concatenate([-s, s], -1)
    inv = ROPE_THETA ** (-jnp.arange(0, 32, dtype=F32) * 2.0 / 64)
    ang = pos[:, None] * inv[None, :]
    c, s = jnp.cos(ang), jnp.sin(ang)
    z = jnp.zeros_like(s)
    z64 = jnp.zeros((p, 64), F32)
    cos_b = jnp.concatenate([c, c, c, c], -1)
    sin_b1 = jnp.concatenate([-s, z, -s, z], -1)
    sin_b2 = jnp.concatenate([z, s, z, s], -1)
    cos_m = jnp.concatenate([c, c, jnp.full((p, 64), IDX_HEADS ** -0.5, F32)], -1)
    sin_m1 = jnp.concatenate([-s, z, z64], -1)
    sin_m2 = jnp.concatenate([z, s, z64], -1)
    return jnp.stack([cos_a, sin_a, cos_b, sin_b1, sin_b2, cos_m, sin_m1, sin_m2])


def _ret_body(rq_ref, rk_ref, rv_ref, rg_ref, mk_ref, mv_ref, dmat_ref, xi_ref, zeta_ref,
              mzeta_ref, gsc_ref, out_ref, r_ref):
    @pl.when(pl.program_id(1) == 0)
    def _():
        for h in range(RET_HEADS):
            hs = slice(h * 128, (h + 1) * 128)
            kz = (mk_ref[:, hs].astype(F32) * mzeta_ref[h]).astype(BF16)
            r_ref[h] = lax.dot_general(kz, mv_ref[:, hs], TN, preferred_element_type=F32)

    for h in range(RET_HEADS):
        hs = slice(h * 128, (h + 1) * 128)
        q = rq_ref[:, hs]
        k = rk_ref[:, hs]
        v = rv_ref[:, hs]
        s = lax.dot_general(q, k, NT, preferred_element_type=F32) * dmat_ref[h]
        o = jnp.dot(s.astype(BF16), v, preferred_element_type=F32)
        r = r_ref[h]
        qx = (q.astype(F32) * xi_ref[h]).astype(BF16)
        o = o + jnp.dot(qx, r.astype(BF16), preferred_element_type=F32)
        kz = (k.astype(F32) * zeta_ref[h]).astype(BF16)
        u = lax.dot_general(kz, v, TN, preferred_element_type=F32)
        r_ref[h] = r * gsc_ref[h] + u
        mu = jnp.mean(o, axis=-1, keepdims=True)
        d = o - mu
        var = jnp.mean(d * d, axis=-1, keepdims=True)
        on = d * lax.rsqrt(var + EPS)
        g = rg_ref[:, hs]
        out_ref[:, hs] = (g * jax.nn.sigmoid(g) * on).astype(BF16)


def _retention(rq, rk, rv, rg, mk, mv, batch, seq):
    n = rq.shape[0]
    nb = seq // RET_BLOCK
    lg = jnp.log(1.0 - 2.0 ** (-5.0 - jnp.arange(RET_HEADS, dtype=F32)))
    i = jnp.arange(RET_BLOCK, dtype=F32)
    ci = jnp.arange(RET_BLOCK) // CHUNK
    vis = (ci[None, :] <= ci[:, None])
    dmat = jnp.where(vis[None], jnp.exp(lg[:, None, None] * jnp.abs(i[:, None] - i[None, :])), 0.0)
    xi = jnp.broadcast_to(jnp.exp(lg[:, None] * (i + 1.0)[None, :])[:, :, None], (RET_HEADS, RET_BLOCK, 128))
    zeta = jnp.broadcast_to(jnp.exp(lg[:, None] * (RET_BLOCK - 1.0 - i)[None, :])[:, :, None],
                            (RET_HEADS, RET_BLOCK, 128))
    im = jnp.arange(N_META, dtype=F32)
    mzeta = jnp.broadcast_to(jnp.exp(lg[:, None] * (N_META - 1.0 - im)[None, :])[:, :, None],
                             (RET_HEADS, N_META, 128))
    gsc = jnp.broadcast_to(jnp.exp(lg * RET_BLOCK)[:, None, None], (RET_HEADS, 1, 128))
    row = lambda b, s: (b * nb + s, 0)
    c2 = lambda b, s: (0, 0)
    c3 = lambda b, s: (0, 0, 0)
    return pl.pallas_call(
        _ret_body,
        grid=(batch, nb),
        in_specs=[pl.BlockSpec((RET_BLOCK, 512), row), pl.BlockSpec((RET_BLOCK, 512), row),
                  pl.BlockSpec((RET_BLOCK, 512), row), pl.BlockSpec((RET_BLOCK, 512), row),
                  pl.BlockSpec((N_META, 512), c2), pl.BlockSpec((N_META, 512), c2),
                  pl.BlockSpec((RET_HEADS, RET_BLOCK, RET_BLOCK), c3),
                  pl.BlockSpec((RET_HEADS, RET_BLOCK, 128), c3),
                  pl.BlockSpec((RET_HEADS, RET_BLOCK, 128), c3),
                  pl.BlockSpec((RET_HEADS, N_META, 128), c3),
                  pl.BlockSpec((RET_HEADS, 1, 128), c3)],
        out_specs=pl.BlockSpec((RET_BLOCK, 512), row),
        out_shape=jax.ShapeDtypeStruct((n, 512), BF16),
        scratch_shapes=[pltpu.VMEM((RET_HEADS, 128, 128), F32)],
        compiler_params=_params(("arbitrary", "arbitrary")),
        name="retention",
    )(rq, rk, rv, rg, mk, mv, dmat, xi, zeta, mzeta, gsc)


def _dsa_body(qa_ref, qi_ref, misc_ref, k_ref, v_ref, ki_ref, wuv_ref, out_ref,
              isc_ref, bias_ref, m_ref, l_ref, acc_ref, *, topk):
    qb = pl.program_id(1)
    tq = DSA_BLOCK
    nh = DSA_HEADS

    wt = misc_ref[...].T
    qi = qi_ref[...].reshape(IDX_HEADS * tq, IDX_DIM)

    def isc_tile(koff, width, allowed):
        kt = ki_ref[pl.ds(koff, width), :]
        z = lax.dot_general(kt, qi, NT, preferred_element_type=F32)
        isc = None
        for h in range(IDX_HEADS):
            term = jnp.maximum(z[:, h * tq:(h + 1) * tq], 0.0) * wt[64 + h:65 + h, :]
            isc = term if isc is None else isc + term
        if allowed is not None:
            isc = jnp.where(allowed, isc, -jnp.inf)
        isc_ref[pl.ds(koff, width), :] = isc

    isc_tile(0, LANES, lax.broadcasted_iota(I32, (LANES, tq), 0) >= LANES - N_META)

    def body_a(t, c):
        isc_tile(pl.multiple_of(LANES + t * tq, LANES), tq, None)
        return c

    lax.fori_loop(0, qb, body_a, 0)
    keyc = lax.broadcasted_iota(I32, (tq, tq), 0) // CHUNK
    qryc = lax.broadcasted_iota(I32, (tq, tq), 1) // CHUNK
    diag_off = pl.multiple_of(LANES + qb * tq, LANES)
    isc_tile(diag_off, tq, keyc <= qryc)

    kf = float(topk)
    sub = SUBLANES

    def key_to_f32(key):
        return lax.bitcast_convert_type(jnp.where(key < 0, key ^ jnp.int32(0x7FFFFFFF), key), F32)

    def count(pred):
        def slab(off, a):
            hit = jnp.where(pred(isc_ref[pl.ds(off, LANES), :]), 1.0, 0.0)
            return a + jnp.sum(hit.reshape(LANES // sub, sub, tq), axis=0)

        def tile(off, a):
            return slab(pl.multiple_of(off + LANES, LANES), slab(off, a))

        def body(t, a):
            off = pl.multiple_of(LANES + t * (2 * tq), LANES)
            return tile(pl.multiple_of(off + tq, LANES), tile(off, a))

        a = lax.fori_loop(0, (qb + 1) // 2, body, slab(0, jnp.zeros((sub, tq), F32)))
        a = lax.cond(qb % 2 == 0, lambda a: tile(diag_off, a), lambda a: a, a)
        return jnp.sum(a, axis=0, keepdims=True)

    def count_ge(cand):
        return count(lambda x: x >= cand)

    lowest = jnp.full((1, tq), -3.0e38, F32)
    n_real = count_ge(lowest)
    c_pos = count_ge(jnp.zeros((1, tq), F32))
    pos = c_pos >= kf
    base0 = jnp.where(pos, jnp.int32(0), jnp.int32(INT_MIN))
    cnt0 = jnp.where(n_real <= kf, kf, jnp.where(pos, c_pos, n_real))

    def unfinished(cnt):
        return jnp.max(jnp.where(cnt != kf, 1.0, 0.0))

    def try_bit(bit, base, cnt):
        cand = base | lax.shift_left(jnp.int32(1), bit)
        c = count_ge(key_to_f32(cand))
        ok = c >= kf
        return jnp.where(ok, cand, base), jnp.where(ok, c, cnt)

    base, cnt = base0, cnt0
    groups = DSA_KEY_BITS // DSA_BITS_PER_CHECK
    for bit in range(DSA_KEY_BITS - 1, DSA_BITS_PER_CHECK * groups - 1, -1):
        base, cnt = try_bit(bit, base, cnt)

    zero_tie = jnp.logical_and(pos, count(lambda x: x > 0.0) < kf)

    def group_cond(state):
        g, _, cnt = state
        return jnp.logical_and(g >= 0, unfinished(jnp.where(zero_tie, kf, cnt)) > 0.0)

    def group_body(state):
        g, base, cnt = state
        for k in range(DSA_BITS_PER_CHECK - 1, -1, -1):
            base, cnt = try_bit(g * DSA_BITS_PER_CHECK + k, base, cnt)
        return g - 1, base, cnt

    _, base, cnt = lax.while_loop(group_cond, group_body, (jnp.int32(groups - 1), base, cnt))

    def refine_cond(state):
        it, _, _, _, go = state
        return jnp.logical_and(it < DSA_REFINE_STEPS, go > 0.0)

    def refine_body(state):
        it, lo, hi, cnt, _ = state
        mid = lo + 0.5 * (hi - lo)
        c = count_ge(mid)
        ok = c >= kf
        cnt = jnp.where(ok, c, cnt)
        moving = jnp.logical_and(jnp.logical_and(mid > lo, mid < hi),
                                 jnp.logical_and(cnt != kf, jnp.logical_not(zero_tie)))
        return it + 1, jnp.where(ok, mid, lo), jnp.where(ok, hi, mid), cnt, jnp.max(jnp.where(moving, 1.0, 0.0))

    _, thr, _, cnt, _ = lax.while_loop(
        refine_cond, refine_body,
        (jnp.int32(0), key_to_f32(base), key_to_f32(base + 1), cnt, unfinished(jnp.where(zero_tie, kf, cnt))))
    thr = jnp.where(n_real <= kf, lowest, thr)

    need = kf - count(lambda x: x > thr)
    lower = (lax.broadcasted_iota(I32, (tq, tq), 1) <= lax.broadcasted_iota(I32, (tq, tq), 0)).astype(BF16)

    def mask_tile(off, width, ties_before):
        x = isc_ref[pl.ds(off, width), :]
        tie = x == thr
        tie_f = jnp.where(tie, 1.0, 0.0)
        upto = jnp.dot(lower[:width, :width], tie_f.astype(BF16), preferred_element_type=F32)
        keep = jnp.logical_or(x > thr, jnp.logical_and(tie, ties_before + upto - tie_f < need))
        bias_ref[:, pl.ds(off, width)] = jnp.where(keep, 0.0, NEG).T
        return ties_before + upto[width - 1:width, :]

    def body_m(t, ties_before):
        return mask_tile(pl.multiple_of(LANES + t * tq, LANES), tq, ties_before)

    lax.fori_loop(0, qb + 1, body_m, mask_tile(0, LANES, jnp.zeros((1, tq), F32)))

    m_ref[...] = jnp.full(m_ref.shape, NEG, F32)
    l_ref[...] = jnp.zeros(l_ref.shape, F32)
    acc_ref[...] = jnp.zeros(acc_ref.shape, F32)
    qa = qa_ref[...].reshape(nh * tq, DSA_DIM)

    def att_tile(koff, width):
        kt = k_ref[pl.ds(koff, width), :]
        vt = v_ref[pl.ds(koff, width), :]
        s = lax.dot_general(qa, kt, NT, preferred_element_type=F32).reshape(nh, tq, width)
        reps = width // LANES
        s = (s + bias_ref[:, pl.ds(koff, width)][None]).reshape(nh * tq, width)
        m_prev = m_ref[...]
        m_new = jnp.maximum(m_prev, jnp.max(s, axis=1, keepdims=True))
        alpha = jnp.exp(m_prev - m_new)
        m_w = m_new if reps == 1 else jnp.concatenate([m_new] * reps, axis=1)
        p = jnp.exp(s - m_w)
        l_ref[...] = alpha * l_ref[...] + jnp.sum(p, axis=1, keepdims=True)
        acc_ref[...] = alpha * acc_ref[...] + jnp.dot(p.astype(BF16), vt, preferred_element_type=F32)
        m_ref[...] = m_new

    att_tile(0, LANES)

    def body_c(t, c):
        att_tile(pl.multiple_of(LANES + t * (2 * tq), LANES), 2 * tq)
        return c

    lax.fori_loop(0, (qb + 1) // 2, body_c, 0)

    @pl.when(qb % 2 == 0)
    def _():
        att_tile(diag_off, tq)

    o = (acc_ref[...] / l_ref[...]).astype(BF16).reshape(nh, tq, DSA_DIM)
    for h in range(nh):
        out_ref[:, h * DSA_OUT_DIM:(h + 1) * DSA_OUT_DIM] = jnp.dot(
            o[h], wuv_ref[h], preferred_element_type=F32).astype(BF16)


def _dsa(dq, iq, misc, kpad, vpad, kipad, wuv, batch, seq, topk):
    n = misc.shape[0]
    nq = seq // DSA_BLOCK
    tp = kpad.shape[1]
    head = lambda b, q: (0, b * nq + q, 0)
    row = lambda b, q: (b * nq + q, 0)
    per_b = lambda b, q: (b, 0, 0)
    return pl.pallas_call(
        functools.partial(_dsa_body, topk=topk),
        grid=(batch, nq),
        in_specs=[pl.BlockSpec((DSA_HEADS, DSA_BLOCK, DSA_DIM), head),
                  pl.BlockSpec((IDX_HEADS, DSA_BLOCK, IDX_DIM), head),
                  pl.BlockSpec((DSA_BLOCK, 128), row),
                  pl.BlockSpec((None, tp, DSA_DIM), per_b),
                  pl.BlockSpec((None, tp, DSA_DIM), per_b),
                  pl.BlockSpec((None, tp, IDX_DIM), per_b),
                  pl.BlockSpec((DSA_HEADS, DSA_DIM, DSA_OUT_DIM), lambda b, q: (0, 0, 0))],
        out_specs=pl.BlockSpec((DSA_BLOCK, 512), row),
        out_shape=jax.ShapeDtypeStruct((n, 512), BF16),
        scratch_shapes=[pltpu.VMEM((tp, DSA_BLOCK), F32),
                        pltpu.VMEM((DSA_BLOCK, tp), F32),
                        pltpu.VMEM((DSA_HEADS * DSA_BLOCK, LANES), F32),
                        pltpu.VMEM((DSA_HEADS * DSA_BLOCK, LANES), F32),
                        pltpu.VMEM((DSA_HEADS * DSA_BLOCK, DSA_DIM), F32)],
        compiler_params=_params(("arbitrary", "arbitrary")),
        name="dsa",
    )(dq, iq, misc, kpad, vpad, kipad, wuv)


def _post_body(ret_ref, att_ref, x_ref, w_ref, g_ref, h1_ref, hnt_ref):
    y = jnp.dot(ret_ref[...], w_ref[0:512, :], preferred_element_type=F32)
    y = y + jnp.dot(att_ref[...], w_ref[512:1024, :], preferred_element_type=F32)
    h1 = x_ref[...] + y
    h1_ref[...] = h1
    hnt_ref[...] = _rms(h1, g_ref[...]).T.astype(BF16)


def _post(ret, att, x2d, w_out, g, tm):
    n = x2d.shape[0]
    row = lambda i: (i, 0)
    return pl.pallas_call(
        _post_body,
        grid=(n // tm,),
        in_specs=[pl.BlockSpec((tm, 512), row), pl.BlockSpec((tm, 512), row),
                  pl.BlockSpec((tm, D_MODEL), row),
                  pl.BlockSpec((D_MODEL, D_MODEL), lambda i: (0, 0)),
                  pl.BlockSpec((1, D_MODEL), lambda i: (0, 0))],
        out_specs=(pl.BlockSpec((tm, D_MODEL), row), pl.BlockSpec((D_MODEL, tm), lambda i: (0, i))),
        out_shape=(jax.ShapeDtypeStruct((n, D_MODEL), F32), jax.ShapeDtypeStruct((D_MODEL, n), BF16)),
        compiler_params=_params(("arbitrary",)),
        name="post",
    )(ret, att, x2d, w_out, g)


def _oddeven_merge_sort_pairs(n):
    pairs = []
    p = 1
    while p < n:
        k = p
        while k >= 1:
            for j in range(k % p, n - k, 2 * k):
                for i in range(min(k, n - j - k)):
                    if (i + j) // (2 * p) == (i + j + k) // (2 * p):
                        pairs.append((i + j, i + j + k))
            k //= 2
        p *= 2
    return tuple(pairs)


def _bitonic_merge_pairs(n):
    pairs = []
    d = n // 2
    while d >= 1:
        pairs += [(i, i + d) for i in range(n) if (i & d) == 0]
        d //= 2
    return tuple(pairs)


_SORT16 = _oddeven_merge_sort_pairs(PEER_TOPK)
_MERGE16 = _bitonic_merge_pairs(PEER_TOPK)


def _top16_rows(s):
    def exchange(x, pairs):
        for i, j in pairs:
            x[i], x[j] = jnp.maximum(x[i], x[j]), jnp.minimum(x[i], x[j])

    x = [s[SUBLANES * j:SUBLANES * (j + 1), :] for j in range(PEER_TOPK)]
    exchange(x, _SORT16)
    shift = SUBLANES // 2
    while shift >= 1:
        other = [pltpu.roll(v, shift, 0) for v in x]
        x = [jnp.maximum(x[i], other[PEER_TOPK - 1 - i]) for i in range(PEER_TOPK)]
        exchange(x, _MERGE16)
        shift //= 2
    return [v[0:1] for v in x]


def _prefix_count(pred, w):
    m8 = pred(w[7])
    m4 = pred(jnp.where(m8, w[11], w[3]))
    m2 = pred(jnp.where(m8, jnp.where(m4, w[13], w[9]), jnp.where(m4, w[5], w[1])))
    m1 = pred(jnp.where(m8, jnp.where(m4, jnp.where(m2, w[14], w[12]), jnp.where(m2, w[10], w[8])),
                        jnp.where(m4, jnp.where(m2, w[6], w[4]), jnp.where(m2, w[2], w[0]))))
    bit = lambda m, v: jnp.where(m, v, 0.0)
    return bit(m8, 8.0) + bit(m4, 4.0) + bit(m2, 2.0) + bit(m1, 1.0) + bit(pred(w[15]), 1.0)


def _top_desc(s, k, dst_ref, ts):
    cur = s
    for r in range(k):
        m = jnp.max(cur, axis=0, keepdims=True)
        dst_ref[r:r + 1, ts] = m
        if r + 1 < k:
            cur = jnp.where(cur == m, -jnp.inf, cur)


def _psel_body(hnt_ref, wq_ref, sk_ref, cnt_ref, e1_ref, r2_ref, e2_ref, s_ref, v1_ref, v2_ref, vc_ref):
    hnt = hnt_ref[...]
    for h in range(PEER_HEADS):
        q = jnp.dot(wq_ref[h * 256:(h + 1) * 256, :], hnt, preferred_element_type=F32)
        s_ref[0] = jnp.dot(sk_ref[h, 0], q[:128].astype(BF16), preferred_element_type=F32)
        s_ref[1] = jnp.dot(sk_ref[h, 1], q[128:].astype(BF16), preferred_element_type=F32)
        for tc in range(hnt.shape[1] // LANES):
            ts = slice(tc * LANES, (tc + 1) * LANES)
            s1 = s_ref[0, :, ts]
            s2 = s_ref[1, :, ts]
            w1 = _top16_rows(s1)
            w2 = _top16_rows(s2)
            for b in range(PEER_TOPK):
                v1_ref[b:b + 1, ts] = w1[b]
                v2_ref[b:b + 1, ts] = w2[b]
            v1 = v1_ref[:, ts]
            v2 = v2_ref[:, ts]
            cand = jnp.concatenate([v2 + v1[0:1]] + [v2[0:8] + v1[a:a + 1] for a in range(1, 8)]
                                   + [v1[8:16] + v2[0:1]], axis=0)
            _top_desc(cand, PEER_TOPK, vc_ref, ts)
            tau = vc_ref[PEER_TOPK - 1:PEER_TOPK, ts]
            top = vc_ref[0:1, ts]
            zsum = jnp.sum(jnp.where(cand >= tau, jnp.exp(cand - top), 0.0), axis=0, keepdims=True)
            cnt_ref[h, :, ts] = _prefix_count(lambda t: s1 + t >= tau, w2)
            e1_ref[h, :, ts] = jnp.exp(s1 - v1[0:1]) / zsum
            rank2 = _prefix_count(lambda t: t > s2, w2)
            r2_ref[h, :, ts] = pltpu.bitcast(rank2.astype(BF16), I32)
            e2_ref[h, :, ts] = pltpu.bitcast(jnp.exp(s2 - v2[0:1]).astype(BF16), I32)


def _psel(hnt, wq_t, sk, tn):
    n = hnt.shape[1]
    spec = pl.BlockSpec((PEER_HEADS, PEER_NKEYS, tn), lambda i: (0, 0, i))
    shp32 = jax.ShapeDtypeStruct((PEER_HEADS, PEER_NKEYS, n), F32)
    shp16 = jax.ShapeDtypeStruct((PEER_HEADS, PEER_NKEYS // 2, n), I32)
    spec16 = pl.BlockSpec((PEER_HEADS, PEER_NKEYS // 2, tn), lambda i: (0, 0, i))
    return pl.pallas_call(
        _psel_body,
        grid=(n // tn,),
        in_specs=[pl.BlockSpec((D_MODEL, tn), lambda i: (0, i)),
                  pl.BlockSpec((PEER_HEADS * 256, D_MODEL), lambda i: (0, 0)),
                  pl.BlockSpec((PEER_HEADS, 2, PEER_NKEYS, 128), lambda i: (0, 0, 0, 0))],
        out_specs=(spec, spec, spec16, spec16),
        out_shape=(shp32, shp32, shp16, shp16),
        scratch_shapes=[pltpu.VMEM((2, PEER_NKEYS, tn), F32)] + [pltpu.VMEM((PEER_TOPK, tn), F32)] * 3,
        compiler_params=_params(("arbitrary",)),
        name="psel",
    )(hnt, wq_t, sk)


def _pdense_body(*refs, eb, nchunk, nblk):
    hnt_ref, u_ref, vt_ref = refs[:3]
    cnt_refs = refs[3:3 + nchunk]
    e1_refs = refs[3 + nchunk:3 + 2 * nchunk]
    r2_ref, e2_ref, h1_ref, g_ref, out_ref, acc_ref, a_ref, coef_ref = refs[3 + 2 * nchunk:]
    s = pl.program_id(1)
    nsub = eb // PEER_NKEYS

    def row_bf16(ref, h, i1):
        return jnp.broadcast_to(ref[h, pl.ds(i1, 1), :], (PEER_NKEYS, LANES)).astype(BF16)

    @pl.when(s == 0)
    def _():
        acc_ref[...] = jnp.zeros(acc_ref.shape, F32)

    for part in range(eb // PDENSE_ROWS):
        rows = slice(part * PDENSE_ROWS, (part + 1) * PDENSE_ROWS)
        a_ref[rows, :] = jnp.dot(u_ref[rows, :], hnt_ref[...], preferred_element_type=F32)

    for c in range(nsub):
        rows = slice(c * PEER_NKEYS, (c + 1) * PEER_NKEYS)
        for tc in range(nchunk):
            ts = slice(tc * LANES, (tc + 1) * LANES)
            gate = None
            for h in range(PEER_HEADS):
                r2 = pltpu.bitcast(r2_ref[h, :, ts], BF16)
                e2 = pltpu.bitcast(e2_ref[h, :, ts], BF16)
                cnt = row_bf16(cnt_refs[tc], h, s * nsub + c)
                e1 = row_bf16(e1_refs[tc], h, s * nsub + c)
                term = jnp.where(r2 < cnt, e2, jnp.zeros((), BF16)) * e1
                gate = term if gate is None else gate + term
            at = a_ref[rows, ts]
            gelu = (0.5 * at) * (1.0 + lax.erf(at * (0.5 ** 0.5)))
            coef_ref[rows, ts] = gelu.astype(BF16) * gate

    acc_ref[...] += jnp.dot(vt_ref[...], coef_ref[...], preferred_element_type=F32)

    @pl.when(s == nblk - 1)
    def _():
        out_ref[...] = _rms(h1_ref[...] + acc_ref[...].T, g_ref[...])


def _pdense(hnt, u, vt, cnt, e1, r2, e2, h1, g, tn, eb):
    n = hnt.shape[1]
    nblk = u.shape[0] // eb
    assert nblk >= 2 and vt.shape == (nblk, D_MODEL, eb)
    nchunk = tn // LANES
    chunk = [pl.BlockSpec((PEER_HEADS, PEER_NKEYS, LANES), lambda i, s, c=c: (0, 0, i * nchunk + c))
             for c in range(nchunk)]
    sel16 = pl.BlockSpec((PEER_HEADS, PEER_NKEYS // 2, tn), lambda i, s: (0, 0, i))
    return pl.pallas_call(
        functools.partial(_pdense_body, eb=eb, nchunk=nchunk, nblk=nblk),
        grid=(n // tn, nblk),
        in_specs=[pl.BlockSpec((D_MODEL, tn), lambda i, s: (0, i)),
                  pl.BlockSpec((eb, D_MODEL), lambda i, s: (s, 0)),
                  pl.BlockSpec((None, D_MODEL, eb), lambda i, s: (s, 0, 0))]
                 + chunk + chunk
                 + [sel16, sel16,
                    pl.BlockSpec((tn, D_MODEL), lambda i, s: (i, 0)),
                    pl.BlockSpec((1, D_MODEL), lambda i, s: (0, 0))],
        out_specs=pl.BlockSpec((tn, D_MODEL), lambda i, s: (i, 0)),
        out_shape=jax.ShapeDtypeStruct((n, D_MODEL), F32),
        scratch_shapes=[pltpu.VMEM((D_MODEL, tn), F32),
                        pltpu.VMEM((eb, tn), F32), pltpu.VMEM((eb, tn), BF16)],
        compiler_params=_params(("arbitrary", "arbitrary")),
        name="pdense",
    )(hnt, u, vt, *([cnt] * nchunk), *([e1] * nchunk), r2, e2, h1, g)


def kernel(x, meta_tokens, norm_mix, w_in, w_uv, w_out, norm_ffn, peer_wq, peer_subkeys,
           peer_u, peer_v, norm_final):
    batch, seq, d = x.shape
    assert d == D_MODEL and norm_mix.shape[0] == 1 and seq % DSA_BLOCK == 0
    n = batch * seq
    topk = min(DSA_TOPK_MAX, seq // 4)
    tm = TOKEN_TILE if seq % TOKEN_TILE == 0 else DSA_BLOCK

    w = jnp.pad(w_in[0], ((0, 0), (0, IN_WIDTH_PAD - IN_WIDTH))).astype(BF16)
    g_mix = norm_mix[0][None, :]
    x2d = x.reshape(n, d)
    pos_f = jnp.arange(seq, dtype=F32) + float(N_META)
    pos_m = jnp.arange(N_META, dtype=F32)

    rq, rk, rv, rg, dq, dk, dv, iq, misc = _proj(x2d, g_mix, w, _rope_tables(pos_f), tm, seq // tm)
    _, mk, mv, _, _, mdk, mdv, _, mmisc = _proj(meta_tokens, g_mix, w, _rope_tables(pos_m), N_META, 1)

    ret = _retention(rq, rk, rv, rg, mk, mv, batch, seq)

    def keys(frames, meta):
        width = frames.shape[-1]
        lead = jnp.concatenate([jnp.zeros((LANES - N_META, width), frames.dtype), meta], axis=0)
        lead = jnp.broadcast_to(lead[None], (batch, LANES, width))
        return jnp.concatenate([lead, frames.reshape(batch, seq, width)], axis=1)

    kpad = keys(dk, mdk)
    vpad = keys(dv, mdv)
    kipad = keys(misc[:, :IDX_DIM].astype(BF16), mmisc[:, :IDX_DIM].astype(BF16))
    att = _dsa(dq, iq, misc, kpad, vpad, kipad, w_uv[0].astype(BF16), batch, seq, topk)

    h1, hnt = _post(ret, att, x2d, w_out[0].astype(BF16), norm_ffn[0][None, :], tm)

    wq_t = peer_wq[0].reshape(d, PEER_HEADS * 256).T.astype(BF16)
    cnt, e1, r2, e2 = _psel(hnt, wq_t, peer_subkeys[0].astype(BF16), tm)
    vt = peer_v[0].reshape(-1, PDENSE_EXPERTS, d).transpose(0, 2, 1).astype(BF16)
    out = _pdense(hnt, peer_u[0].astype(BF16), vt, cnt, e1, r2, e2, h1,
                  norm_final[None, :], DSA_BLOCK, PDENSE_EXPERTS)
    return out.reshape(batch, seq, d)
```
